```python
import jax, jax.numpy as jnp
from jax import lax
import numpy as np

D_MODEL = 1024
BATCH = 16
SEQ = 2048
DEPTH = 1

D_MIX = D_MODEL
D_RWKV = D_MIX // 2
D_CONV = D_MIX - D_RWKV
HEAD_SIZE = 64
N_RWKV_HEADS = D_RWKV // HEAD_SIZE
CONV_GROUPS = 8
CONV_WIDTH = 3
D_DECAY_LORA = 64
D_AAA_LORA = 64
D_GATE_LORA = 160
D_FF = -(-(8 * D_MODEL) // (3 * 256)) * 256
D_SHIFTED = 3 * D_RWKV + D_DECAY_LORA + D_AAA_LORA + D_GATE_LORA
D_IN = D_SHIFTED + 3 * D_CONV
LOG_DECAY_SCALE = 0.606531
RMS_EPS = 1e-6
GN_EPS = 64e-5
NORM_EPS = 1e-12

kernel_name = "hybrid_rwkv7_shortconv_encoder_block"


def _rmsnorm(x, w):
    x32 = x.astype(jnp.float32)
    y = x32 * lax.rsqrt(jnp.mean(x32 * x32, axis=-1, keepdims=True) + RMS_EPS)
    return (y * w.astype(jnp.float32)).astype(x.dtype)


def _token_shift(p, mu):
    zeros = jnp.zeros_like(p[:, :1])
    prev = jnp.concatenate([zeros, p[:, :-1]], axis=1)
    nxt = jnp.concatenate([p[:, 1:], zeros], axis=1)
    return p + mu * (0.5 * (prev + nxt) - p)


def _to_heads(t):
    b, s, _ = t.shape
    return t.reshape(b, s, N_RWKV_HEADS, HEAD_SIZE)


def _wkv7_scan(r, w, k, v, kk, a, reverse):
    b = r.shape[0]
    xs = tuple(jnp.swapaxes(t, 0, 1) for t in (r, w, k, v, kk, a))

    def step(state, inp):
        r_t, w_t, k_t, v_t, kk_t, a_t = inp
        sa = jnp.einsum('bhvk,bhk->bhv', state, kk_t)
        state = (state * w_t[:, :, None, :]
                 - sa[..., None] * (kk_t * a_t)[:, :, None, :]
                 + v_t[..., None] * k_t[:, :, None, :])
        y_t = jnp.einsum('bhvk,bhk->bhv', state, r_t)
        return state, y_t

    s0 = jnp.zeros((b, N_RWKV_HEADS, HEAD_SIZE, HEAD_SIZE), jnp.float32)
    _, ys = lax.scan(step, s0, xs, reverse=reverse)
    return jnp.swapaxes(ys, 0, 1)


def _rwkv7_direction(r, k, v, kk, xw, xa, w_up, w0, a_up, a0, k_a, r_k, reverse):
    logit_w = w0 + jnp.tanh(xw) @ w_up.astype(jnp.float32)
    w = jnp.exp(-LOG_DECAY_SCALE * jax.nn.sigmoid(logit_w))
    a = jax.nn.sigmoid(a0 + xa @ a_up.astype(jnp.float32))
    kd = k * (1.0 + (a - 1.0) * k_a)
    rh, kh, vh = _to_heads(r), _to_heads(kd), _to_heads(v)
    y = _wkv7_scan(rh, _to_heads(w), kh, vh, kk, _to_heads(a), reverse)
    bonus = jnp.sum(rh * kh * r_k, axis=-1, keepdims=True) * vh
    return y, bonus


def _rwkv7_mixer(r, k, v, xw, xa, xg, w_up_f, w0_f, w_up_b, w0_b, a_up_f, a0_f,
                 a_up_b, a0_b, g_up, k_k, k_a_f, k_a_b, r_k_f, r_k_b, gn_w, gn_b):
    out_dtype = r.dtype
    r, k, v, xw, xa, xg = (t.astype(jnp.float32) for t in (r, k, v, xw, xa, xg))
    kk = _to_heads(k * k_k)
    kk = kk / jnp.maximum(jnp.linalg.norm(kk, axis=-1, keepdims=True), NORM_EPS)
    y_f, bonus_f = _rwkv7_direction(r, k, v, kk, xw, xa, w_up_f, w0_f, a_up_f, a0_f,
                                    k_a_f, r_k_f, reverse=False)
    y_b, bonus_b = _rwkv7_direction(r, k, v, kk, xw, xa, w_up_b, w0_b, a_up_b, a0_b,
                                    k_a_b, r_k_b, reverse=True)
    y = y_f + y_b
    mean = jnp.mean(y, axis=-1, keepdims=True)
    var = jnp.mean(jnp.square(y - mean), axis=-1, keepdims=True)
    y = (y - mean) * lax.rsqrt(var + GN_EPS)
    y = y * gn_w.reshape(N_RWKV_HEADS, HEAD_SIZE) + gn_b.reshape(N_RWKV_HEADS, HEAD_SIZE)
    y = y + bonus_f + bonus_b
    g = jax.nn.sigmoid(xg) @ g_up.astype(jnp.float32)
    b, s = r.shape[:2]
    return (y.reshape(b, s, D_RWKV) * g).astype(out_dtype)


def _short_conv(pc, conv_w):
    gate_b, gate_c, h = jnp.split(pc, [D_CONV, 2 * D_CONV], axis=-1)
    u = gate_c * h
    u = lax.conv_general_dilated(
        u, conv_w.reshape(CONV_WIDTH, 1, D_CONV).astype(u.dtype),
        window_strides=(1,), padding=[((CONV_WIDTH - 1) // 2, (CONV_WIDTH - 1) // 2)],
        dimension_numbers=('NWC', 'WIO', 'NWC'), feature_group_count=D_CONV)
    return gate_b * u


def _swiglu(h, w_gate, w_up, w_down):
    return (jax.nn.silu(h @ w_gate) * (h @ w_up)) @ w_down


def setup_inputs(seed: int = 0) -> dict:
    key = jax.random.key(seed)
    ks = jax.random.split(key, 32)

    def nrm(k, shape, scale):
        return jax.random.normal(k, shape, jnp.float32) * scale

    L = DEPTH
    return {
        "x": nrm(ks[0], (BATCH, SEQ, D_MODEL), 1.0),
        "norm1_w": 1.0 + nrm(ks[1], (L, D_MODEL), 0.02),
        "w_in": nrm(ks[2], (L, D_MODEL, D_IN), D_MODEL ** -0.5),
        "mu_shift": jax.random.uniform(ks[3], (L, D_SHIFTED), jnp.float32),
        "w_up_f": nrm(ks[4], (L, D_DECAY_LORA, D_RWKV), D_DECAY_LORA ** -0.5),
        "w0_f": nrm(ks[5], (L, D_RWKV), 1.0),
        "w_up_b": nrm(ks[6], (L, D_DECAY_LORA, D_RWKV), D_DECAY_LORA ** -0.5),
        "w0_b": nrm(ks[7], (L, D_RWKV), 1.0),
        "a_up_f": nrm(ks[8], (L, D_AAA_LORA, D_RWKV), D_AAA_LORA ** -0.5),
        "a0_f": nrm(ks[9], (L, D_RWKV), 0.5),
        "a_up_b": nrm(ks[10], (L, D_AAA_LORA, D_RWKV), D_AAA_LORA ** -0.5),
        "a0_b": nrm(ks[11], (L, D_RWKV), 0.5),
        "g_up": nrm(ks[12], (L, D_GATE_LORA, D_RWKV), D_GATE_LORA ** -0.5),
        "k_k": 0.85 + nrm(ks[13], (L, D_RWKV), 0.05),
        "k_a_f": 1.0 + nrm(ks[14], (L, D_RWKV), 0.05),
        "k_a_b": 1.0 + nrm(ks[15], (L, D_RWKV), 0.05),
        "r_k_f": nrm(ks[16], (L, N_RWKV_HEADS, HEAD_SIZE), 0.1),
        "r_k_b": nrm(ks[17], (L, N_RWKV_HEADS, HEAD_SIZE), 0.1),
        "gn_w": 1.0 + nrm(ks[18], (L, D_RWKV), 0.02),
        "gn_b": nrm(ks[19], (L, D_RWKV), 0.02),
        "conv_w": nrm(ks[20], (L, CONV_WIDTH, D_CONV), CONV_WIDTH ** -0.5),
        "w_out": nrm(ks[21], (L, D_MIX, D_MODEL), D_MIX ** -0.5),
        "norm2_w": 1.0 + nrm(ks[22], (L, D_MODEL), 0.02),
        "w_gate": nrm(ks[23], (L, D_MODEL, D_FF), D_MODEL ** -0.5),
        "w_up": nrm(ks[24], (L, D_MODEL, D_FF), D_MODEL ** -0.5),
        "w_down": nrm(ks[25], (L, D_FF, D_MODEL), D_FF ** -0.5),
        "norm_f_w": 1.0 + nrm(ks[26], (D_MODEL,), 0.02),
    }


def reference(x, norm1_w, w_in, mu_shift, w_up_f, w0_f, w_up_b, w0_b, a_up_f, a0_f,
              a_up_b, a0_b, g_up, k_k, k_a_f, k_a_b, r_k_f, r_k_b, gn_w, gn_b, conv_w,
              w_out, norm2_w, w_gate, w_up, w_down, norm_f_w):
    split_pts = [D_RWKV, 2 * D_RWKV, 3 * D_RWKV, 3 * D_RWKV + D_DECAY_LORA,
                 3 * D_RWKV + D_DECAY_LORA + D_AAA_LORA]
    for l in range(DEPTH):
        h = _rmsnorm(x, norm1_w[l])
        p = h @ w_in[l]
        ps, pc = p[..., :D_SHIFTED], p[..., D_SHIFTED:]
        ps = _token_shift(ps, mu_shift[l])
        r, k, v, xw, xa, xg = jnp.split(ps, split_pts, axis=-1)
        o_rwkv = _rwkv7_mixer(r, k, v, xw, xa, xg, w_up_f[l], w0_f[l], w_up_b[l], w0_b[l],
                              a_up_f[l], a0_f[l], a_up_b[l], a0_b[l], g_up[l], k_k[l],
                              k_a_f[l], k_a_b[l], r_k_f[l], r_k_b[l], gn_w[l], gn_b[l])
        o_conv = _short_conv(pc, conv_w[l])
        x = x + jnp.concatenate([o_rwkv, o_conv], axis=-1) @ w_out[l]
        x = x + _swiglu(_rmsnorm(x, norm2_w[l]), w_gate[l], w_up[l], w_down[l])
    return _rmsnorm(x, norm_f_w)
```

```python
import functools

import jax
import jax.numpy as jnp
from jax import lax
from jax.experimental import pallas as pl
from jax.experimental.pallas import tpu as pltpu

F32 = jnp.float32
BF16 = jnp.bfloat16

HEAD = 64
CHUNK = 64
MXU_DIM = 256
QUAD = MXU_DIM // CHUNK
COL_TILE = 256
LOG_DECAY_SCALE = 0.606531
RMS_EPS = 1e-6
GN_EPS = 64e-5
NORM_EPS = 1e-12
VMEM_LIMIT = 56 * 1024 * 1024

P_KK, P_BF, P_BB, P_KDF, P_KDB, P_LWF, P_LWB, P_G, P_BONUS, P_NUM = range(10)
V_W0F, V_W0B, V_A0F, V_A0B, V_KK, V_KAF, V_KAB, V_RKF, V_RKB, V_NUM = range(10)


def _dot(a, b):
    return jnp.dot(a.astype(BF16), b.astype(BF16), preferred_element_type=F32)


def _dot_nt(a, b):
    return lax.dot_general(a.astype(BF16), b.astype(BF16), (((1,), (1,)), ((), ())),
                           preferred_element_type=F32)


def _split2(x):
    hi = x.astype(BF16)
    lo = (x - hi.astype(F32)).astype(BF16)
    return hi, lo


def _split3(x):
    hi = x.astype(BF16)
    r1 = x - hi.astype(F32)
    mid = r1.astype(BF16)
    lo = (r1 - mid.astype(F32)).astype(BF16)
    return hi, mid, lo


def _head_sum(x, ones_bd):
    hi, lo = _split2(x)
    return (jnp.dot(hi, ones_bd, preferred_element_type=F32)
            + jnp.dot(lo, ones_bd, preferred_element_type=F32))


def _rmsnorm(x, w):
    ms = jnp.mean(x * x, axis=-1, keepdims=True)
    return x * lax.rsqrt(ms + RMS_EPS) * w


def _inproj_body(x_ref, n1_ref, w_ref, mu_ref, cw_ref, rkvl_ref, oconv_ref,
                 h_s, gb_s, gc_s, *, n_shift, seq, row_blk):
    j = pl.program_id(1)

    @pl.when(j == 0)
    def _():
        def body(i, c):
            rows = pl.ds(pl.multiple_of(i * row_blk, row_blk), row_blk)
            h_s[rows, :] = _rmsnorm(x_ref[rows, :], n1_ref[...]).astype(BF16)
            return c
        lax.fori_loop(0, seq // row_blk, body, 0)

    p = jnp.dot(h_s[...], w_ref[...], preferred_element_type=F32)
    row = lax.broadcasted_iota(jnp.int32, p.shape, 0)

    def prev(u):
        return jnp.where(row == 0, 0.0, pltpu.roll(u, 1, 0))

    def nxt(u):
        return jnp.where(row == seq - 1, 0.0, pltpu.roll(u, seq - 1, 0))

    @pl.when(j < n_shift)
    def _():
        rkvl_ref[...] = p + mu_ref[...] * (0.5 * (prev(p) + nxt(p)) - p)

    phase = lax.rem(jnp.maximum(j - n_shift, 0), 3)
    is_conv = j >= n_shift

    @pl.when(is_conv & (phase == 0))
    def _():
        gb_s[...] = p

    @pl.when(is_conv & (phase == 1))
    def _():
        gc_s[...] = p

    @pl.when(is_conv & (phase == 2))
    def _():
        u = gc_s[...] * p
        cw = cw_ref[...]
        y = cw[0:1, :] * prev(u) + cw[1:2, :] * u + cw[2:3, :] * nxt(u)
        oconv_ref[...] = (gb_s[...] * y).astype(oconv_ref.dtype)


def _inproj(x, n1, w_p, mu_p, conv_w, *, n_shift):
    bsz, seq, d = x.shape
    n_tiles = w_p.shape[1] // COL_TILE
    d_conv = conv_w.shape[-1]

    def conv_blk(j):
        return jnp.maximum(j - n_shift, 0) // 3

    return pl.pallas_call(
        functools.partial(_inproj_body, n_shift=n_shift, seq=seq, row_blk=256),
        grid=(bsz, n_tiles),
        in_specs=[
            pl.BlockSpec((None, seq, d), lambda b, j: (b, 0, 0)),
            pl.BlockSpec((1, d), lambda b, j: (0, 0)),
            pl.BlockSpec((d, COL_TILE), lambda b, j: (0, j)),
            pl.BlockSpec((1, COL_TILE), lambda b, j: (0, jnp.minimum(j, n_shift - 1))),
            pl.BlockSpec((3, COL_TILE), lambda b, j: (0, conv_blk(j))),
        ],
        out_specs=[
            pl.BlockSpec((None, seq, COL_TILE), lambda b, j: (b, 0, jnp.minimum(j, n_shift - 1))),
            pl.BlockSpec((None, seq, COL_TILE), lambda b, j: (b, 0, conv_blk(j))),
        ],
        out_shape=[
            jax.ShapeDtypeStruct((bsz, seq, n_shift * COL_TILE), F32),
            jax.ShapeDtypeStruct((bsz, seq, d_conv), BF16),
        ],
        scratch_shapes=[
            pltpu.VMEM((seq, d), BF16),
            pltpu.VMEM((seq, COL_TILE), F32),
            pltpu.VMEM((seq, COL_TILE), F32),
        ],
        compiler_params=pltpu.CompilerParams(
            dimension_semantics=("arbitrary", "arbitrary"), vmem_limit_bytes=VMEM_LIMIT),
        name="inproj",
    )(x, n1, w_p, mu_p, conv_w)


def _prep_body(r_ref, k_ref, v_ref, l_ref, vec_ref, wup_ref, aup_ref, gup_ref, ones_ref,
               out_ref, *, dr):
    r = r_ref[...]
    k = k_ref[...]
    v = v_ref[...]
    vec = vec_ref[...]
    ones_bd = ones_ref[...]

    def vrow(i):
        return vec[i:i + 1, :]

    t128 = l_ref[:, 0:128]
    lw = jnp.dot(jnp.tanh(t128).astype(BF16), wup_ref[...], preferred_element_type=F32)
    aa = jnp.dot(t128.astype(BF16), aup_ref[...], preferred_element_type=F32)
    lw_f = -LOG_DECAY_SCALE * jax.nn.sigmoid(vrow(V_W0F) + lw[:, :dr])
    lw_b = -LOG_DECAY_SCALE * jax.nn.sigmoid(vrow(V_W0B) + lw[:, dr:])
    a_f = jax.nn.sigmoid(vrow(V_A0F) + aa[:, :dr])
    a_b = jax.nn.sigmoid(vrow(V_A0B) + aa[:, dr:])
    g = jnp.dot(jax.nn.sigmoid(l_ref[:, 256:512]).astype(BF16), gup_ref[...],
                preferred_element_type=F32)

    kkr = k * vrow(V_KK)
    ssq = _head_sum(kkr * kkr, ones_bd)
    kk = kkr / jnp.maximum(jnp.sqrt(ssq), NORM_EPS)
    kd_f = k * (1.0 + (a_f - 1.0) * vrow(V_KAF))
    kd_b = k * (1.0 + (a_b - 1.0) * vrow(V_KAB))
    bonus = _head_sum(r * kd_f * vrow(V_RKF) + r * kd_b * vrow(V_RKB), ones_bd) * v

    def put(i, val):
        out_ref[:, i * dr:(i + 1) * dr] = val

    put(P_KK, kk)
    put(P_BF, a_f * kk)
    put(P_BB, a_b * kk)
    put(P_KDF, kd_f)
    put(P_KDB, kd_b)
    put(P_LWF, lw_f)
    put(P_LWB, lw_b)
    put(P_G, g)
    put(P_BONUS, bonus)


def _prep(rkvl, vecs, wup, aup, gup, ones_bd, *, dr, row_tile):
    bsz, seq, _ = rkvl.shape

    def col(c):
        return pl.BlockSpec((None, row_tile, dr), lambda b, i: (b, i, c))

    def full(a):
        return pl.BlockSpec(a.shape, lambda b, i: (0,) * a.ndim)

    return pl.pallas_call(
        functools.partial(_prep_body, dr=dr),
        grid=(bsz, seq // row_tile),
        in_specs=[col(0), col(1), col(2), col(3),
                  full(vecs), full(wup), full(aup), full(gup), full(ones_bd)],
        out_specs=pl.BlockSpec((None, row_tile, P_NUM * dr), lambda b, i: (b, i, 0)),
        out_shape=jax.ShapeDtypeStruct((bsz, seq, P_NUM * dr), F32),
        compiler_params=pltpu.CompilerParams(
            dimension_semantics=("arbitrary", "arbitrary"), vmem_limit_bytes=VMEM_LIMIT),
        name="prep",
    )(rkvl, rkvl, rkvl, rkvl, vecs, wup, aup, gup, ones_bd)


def _block_diag(x, lane_head):
    return jnp.concatenate([jnp.where(lane_head == h, x, 0.0) for h in range(QUAD)],
                           axis=0).astype(BF16)


def _scan_direction(r, v, kk, b, kd, lw, y_ref, st_ref, st_base, *, forward):
    c = CHUNK
    dr = r.shape[-1]
    t_idx = lax.broadcasted_iota(jnp.int32, (c, MXU_DIM), 0)
    lane = lax.broadcasted_iota(jnp.int32, (c, MXU_DIM), 1)
    s_idx = lane & (c - 1)
    lane_head = lane >> 6
    if forward:
        strict, incl = s_idx < t_idx, s_idx <= t_idx
    else:
        strict, incl = s_idx > t_idx, s_idx >= t_idx
    eye = (s_idx == t_idx).astype(F32)
    bd_mask = ((lax.broadcasted_iota(jnp.int32, (MXU_DIM, MXU_DIM), 0) >> 6)
               == (lax.broadcasted_iota(jnp.int32, (MXU_DIM, MXU_DIM), 1) >> 6))

    tri_t = lax.broadcasted_iota(jnp.int32, (c, 3 * c), 0)
    tri_s = lax.broadcasted_iota(jnp.int32, (c, 3 * c), 1) & (c - 1)
    tri = ((tri_s <= tri_t) if forward else (tri_s >= tri_t)).astype(BF16)
    cs = jnp.dot(tri, jnp.concatenate(_split3(lw), axis=0), preferred_element_type=F32)
    cs_end = cs[c - 1:c, :] if forward else cs[0:1, :]
    e_t = jnp.exp(cs)
    e_prev = jnp.exp(cs - lw)
    e_inv = jnp.exp(-cs)
    e_end = jnp.exp(cs_end - cs)
    w_c = jnp.exp(cs_end)

    kap_all = kk * e_prev
    rt_all = r * e_t
    kt_all = kd * e_inv
    bt_all = b * e_inv
    kh_all = kd * e_end
    bh_all = b * e_end

    for q in range(dr // MXU_DIM):
        sl = slice(q * MXU_DIM, (q + 1) * MXU_DIM)
        kr = jnp.concatenate([kap_all[:, sl], rt_all[:, sl]], axis=0).astype(BF16)
        g_b = _dot_nt(kr, _block_diag(bt_all[:, sl], lane_head))
        g_k = _dot_nt(kr, _block_diag(kt_all[:, sl], lane_head))
        akb = jnp.where(strict, g_b[:c], 0.0)
        arb = jnp.where(incl, g_b[c:], 0.0)
        akk = jnp.where(strict, g_k[:c], 0.0)
        ark = jnp.where(incl, g_k[c:], 0.0)

        x = -akb
        t = eye + x
        x = _dot(x, _block_diag(x, lane_head))
        n_sq = CHUNK.bit_length() - 1
        for _ in range(n_sq - 2):
            rr = _dot(jnp.concatenate([x, t], axis=0), _block_diag(x, lane_head))
            x = rr[:c]
            t = t + rr[c:]
        t = t + _dot(t, _block_diag(x, lane_head))

        mt = st_ref[st_base + q]
        krm = _dot_nt(kr, mt)
        av = _dot(jnp.concatenate([akk, ark], axis=0), _block_diag(v[:, sl], lane_head))
        sa = _dot(t, _block_diag((krm[:c] + av[:c]), lane_head))
        y = krm[c:] + av[c:] - _dot(arb, _block_diag(sa, lane_head))
        y_ref[:, sl] = y

        lhs_t = jnp.concatenate([v[:, sl], -sa], axis=0).T
        rhs = jnp.concatenate([kh_all[:, sl], bh_all[:, sl]], axis=0)
        upd = _dot(lhs_t, rhs)
        st_ref[st_base + q] = w_c[:, sl] * mt + jnp.where(bd_mask, upd, 0.0)


def _scan_body(rf, vf, kkf, bf, kdf, lwf, rb, vb, kkb, bb, kdb, lwb, yf_ref, yb_ref, st_ref,
               *, n_quad):
    @pl.when(pl.program_id(1) == 0)
    def _():
        st_ref[...] = jnp.zeros_like(st_ref)

    _scan_direction(rf[...], vf[...], kkf[...], bf[...], kdf[...], lwf[...], yf_ref, st_ref, 0,
                    forward=True)
    _scan_direction(rb[...], vb[...], kkb[...], bb[...], kdb[...], lwb[...], yb_ref, st_ref, n_quad,
                    forward=False)


def _scan(rkvl, prep, *, dr):
    bsz, seq, _ = rkvl.shape
    nc = seq // CHUNK
    n_quad = dr // MXU_DIM

    def fwd(col):
        return pl.BlockSpec((None, CHUNK, dr), lambda b, c: (b, c, col))

    def bwd(col):
        return pl.BlockSpec((None, CHUNK, dr), lambda b, c: (b, nc - 1 - c, col))

    return pl.pallas_call(
        functools.partial(_scan_body, n_quad=n_quad),
        grid=(bsz, nc),
        in_specs=[fwd(0), fwd(2), fwd(P_KK), fwd(P_BF), fwd(P_KDF), fwd(P_LWF),
                  bwd(0), bwd(2), bwd(P_KK), bwd(P_BB), bwd(P_KDB), bwd(P_LWB)],
        out_specs=[pl.BlockSpec((None, CHUNK, dr), lambda b, c: (b, c, 0)),
                   pl.BlockSpec((None, CHUNK, dr), lambda b, c: (b, nc - 1 - c, 0))],
        out_shape=[jax.ShapeDtypeStruct((bsz, seq, dr), F32)] * 2,
        scratch_shapes=[pltpu.VMEM((2 * n_quad, MXU_DIM, MXU_DIM), F32)],
        compiler_params=pltpu.CompilerParams(
            dimension_semantics=("arbitrary", "arbitrary"), vmem_limit_bytes=VMEM_LIMIT),
        name="scan",
    )(rkvl, rkvl, prep, prep, prep, prep, rkvl, rkvl, prep, prep, prep, prep)


def _post_body(yf_ref, yb_ref, g_ref, bonus_ref, oconv_ref, x_ref, wout_ref, gn_ref, ones_ref,
               out_ref, *, dr):
    ones_bd = ones_ref[...]
    y = yf_ref[...] + yb_ref[...]
    mean = _head_sum(y, ones_bd) * (1.0 / HEAD)
    d = y - mean
    var = _head_sum(d * d, ones_bd) * (1.0 / HEAD)
    yn = d * lax.rsqrt(var + GN_EPS) * gn_ref[0:1, :] + gn_ref[1:2, :] + bonus_ref[...]
    o = (yn * g_ref[...]).astype(BF16)
    out_ref[...] = (x_ref[...]
                    + jnp.dot(o, wout_ref[0:dr, :], preferred_element_type=F32)
                    + jnp.dot(oconv_ref[...], wout_ref[dr:, :], preferred_element_type=F32))


def _post(yf, yb, prep, oconv, x, wout, gn, ones_bd, *, dr, row_tile):
    bsz, seq, d = x.shape

    def rows(width, col=0):
        return pl.BlockSpec((None, row_tile, width), lambda b, i: (b, i, col))

    def full(a):
        return pl.BlockSpec(a.shape, lambda b, i: (0,) * a.ndim)

    return pl.pallas_call(
        functools.partial(_post_body, dr=dr),
        grid=(bsz, seq // row_tile),
        in_specs=[rows(dr), rows(dr), rows(dr, P_G), rows(dr, P_BONUS), rows(oconv.shape[-1]),
                  rows(d), full(wout), full(gn), full(ones_bd)],
        out_specs=rows(d),
        out_shape=jax.ShapeDtypeStruct((bsz, seq, d), F32),
        compiler_params=pltpu.CompilerParams(
            dimension_semantics=("arbitrary", "arbitrary"), vmem_limit_bytes=VMEM_LIMIT),
        name="post",
    )(yf, yb, prep, prep, oconv, x, wout, gn, ones_bd)


def _ffn_body(x_ref, n2_ref, wg_ref, wu_ref, wd_ref, nf_ref, out_ref, *, final_norm):
    x = x_ref[...]
    h = _rmsnorm(x, n2_ref[...]).astype(BF16)
    a = jnp.dot(h, wg_ref[...], preferred_element_type=F32)
    u = jnp.dot(h, wu_ref[...], preferred_element_type=F32)
    z = (a * jax.nn.sigmoid(a) * u).astype(BF16)
    x2 = x + jnp.dot(z, wd_ref[...], preferred_element_type=F32)
    out_ref[...] = _rmsnorm(x2, nf_ref[...]) if final_norm else x2


def _ffn(x, n2, wg, wu, wd, nf, *, final_norm, row_tile):
    n, d = x.shape

    def full(a):
        return pl.BlockSpec(a.shape, lambda i: (0,) * a.ndim)

    return pl.pallas_call(
        functools.partial(_ffn_body, final_norm=final_norm),
        grid=(n // row_tile,),
        in_specs=[pl.BlockSpec((row_tile, d), lambda i: (i, 0)),
                  full(n2), full(wg), full(wu), full(wd), full(nf)],
        out_specs=pl.BlockSpec((row_tile, d), lambda i: (i, 0)),
        out_shape=jax.ShapeDtypeStruct((n, d), F32),
        compiler_params=pltpu.CompilerParams(
            dimension_semantics=("arbitrary",), vmem_limit_bytes=VMEM_LIMIT),
        name="ffn",
    )(x, n2, wg, wu, wd, nf)


def _pad_cols(a, width):
    return jnp.pad(a, ((0, 0), (0, width - a.shape[1])))


def _pad_rows(a, height):
    return jnp.pad(a, ((0, height - a.shape[0]), (0, 0)))


def kernel(x, norm1_w, w_in, mu_shift, w_up_f, w0_f, w_up_b, w0_b, a_up_f, a0_f, a_up_b, a0_b,
           g_up, k_k, k_a_f, k_a_b, r_k_f, r_k_b, gn_w, gn_b, conv_w, w_out, norm2_w, w_gate,
           w_up, w_down, norm_f_w):
    bsz, seq, d = x.shape
    depth = w_in.shape[0]
    dr = w0_f.shape[-1]
    n_dec, n_aaa, n_gate = w_up_f.shape[1], a_up_f.shape[1], g_up.shape[1]
    d_conv = conv_w.shape[-1]
    assert dr % MXU_DIM == 0 and d_conv % COL_TILE == 0 and seq % CHUNK == 0
    assert n_dec + n_aaa <= 128 and n_gate <= 256
    o_xw = 3 * dr
    o_xa = o_xw + n_dec
    o_xg = o_xa + n_aaa
    o_conv = o_xg + n_gate
    n_shift = (3 * dr + 2 * COL_TILE) // COL_TILE

    head_id = jnp.arange(dr) // HEAD
    ones_bd = (head_id[:, None] == head_id[None, :]).astype(BF16)

    for l in range(depth):
        w = w_in[l]
        gb, gc, hh = (w[:, o_conv + i * d_conv:o_conv + (i + 1) * d_conv] for i in range(3))
        conv_cols = []
        for t in range(d_conv // COL_TILE):
            cs = slice(t * COL_TILE, (t + 1) * COL_TILE)
            conv_cols += [gb[:, cs], gc[:, cs], hh[:, cs]]
        w_p = jnp.concatenate(
            [w[:, :o_xw], _pad_cols(w[:, o_xw:o_xg], COL_TILE), _pad_cols(w[:, o_xg:o_conv], COL_TILE)]
            + conv_cols, axis=1).astype(BF16)
        mu = mu_shift[l][None, :]
        mu_p = jnp.concatenate(
            [mu[:, :o_xw], _pad_cols(mu[:, o_xw:o_xg], COL_TILE), _pad_cols(mu[:, o_xg:o_conv], COL_TILE)],
            axis=1)

        rkvl, oconv = _inproj(x, norm1_w[l][None, :], w_p, mu_p, conv_w[l], n_shift=n_shift)

        vecs = jnp.stack([w0_f[l], w0_b[l], a0_f[l], a0_b[l], k_k[l], k_a_f[l], k_a_b[l],
                          r_k_f[l].reshape(dr), r_k_b[l].reshape(dr)], axis=0)
        vecs = _pad_rows(vecs, 16)
        wup = _pad_rows(jnp.concatenate([w_up_f[l], w_up_b[l]], axis=1), 128).astype(BF16)
        aup = jnp.concatenate([a_up_f[l], a_up_b[l]], axis=1)
        aup = jnp.pad(aup, ((n_dec, 128 - n_dec - n_aaa), (0, 0))).astype(BF16)
        gup = _pad_rows(g_up[l], 256).astype(BF16)
        prep = _prep(rkvl, vecs, wup, aup, gup, ones_bd, dr=dr, row_tile=512)

        yf, yb = _scan(rkvl, prep, dr=dr)

        gn = jnp.stack([gn_w[l], gn_b[l]], axis=0)
        x1 = _post(yf, yb, prep, oconv, x, w_out[l].astype(BF16), gn, ones_bd, dr=dr, row_tile=512)

        x = _ffn(x1.reshape(bsz * seq, d), norm2_w[l][None, :], w_gate[l].astype(BF16),
                 w_up[l].astype(BF16), w_down[l].astype(BF16), norm_f_w[None, :],
                 final_norm=(l == depth - 1), row_tile=256).reshape(bsz, seq, d)
    return x
```

```python
import functools

import jax
import jax.numpy as jnp
from jax import lax
from jax.experimental import pallas as pl
from jax.experimental.pallas import tpu as pltpu

F32 = jnp.float32
BF16 = jnp.bfloat16

HEAD = 64
CHUNK = 64
MXU_DIM = 256
QUAD = MXU_DIM // CHUNK
COL_TILE = 256
LOG_DECAY_SCALE = 0.606531
RMS_EPS = 1e-6
GN_EPS = 64e-5
NORM_EPS = 1e-12
VMEM_LIMIT = 56 * 1024 * 1024

P_KK, P_BF, P_BB, P_KDF, P_KDB, P_LWF, P_LWB, P_G, P_BONUS, P_NUM = range(10)
V_W0F, V_W0B, V_A0F, V_A0B, V_KK, V_KAF, V_KAB, V_RKF, V_RKB, V_NUM = range(10)


def _dot(a, b):
    return jnp.dot(a.astype(BF16), b.astype(BF16), preferred_element_type=F32)


def _dot_nt(a, b):
    return lax.dot_general(a.astype(BF16), b.astype(BF16), (((1,), (1,)), ((), ())),
                           preferred_element_type=F32)


def _split2(x):
    hi = x.astype(BF16)
    lo = (x - hi.astype(F32)).astype(BF16)
    return hi, lo


def _split3(x):
    hi = x.astype(BF16)
    r1 = x - hi.astype(F32)
    mid = r1.astype(BF16)
    lo = (r1 - mid.astype(F32)).astype(BF16)
    return hi, mid, lo


def _head_sum(x, ones_bd):
    hi, lo = _split2(x)
    return (jnp.dot(hi, ones_bd, preferred_element_type=F32)
            + jnp.dot(lo, ones_bd, preferred_element_type=F32))


def _rmsnorm(x, w):
    ms = jnp.mean(x * x, axis=-1, keepdims=True)
    return x * lax.rsqrt(ms + RMS_EPS) * w


def _inproj_body(x_ref, n1_ref, w_ref, mu_ref, cw_ref, rkvl_ref, oconv_ref,
                 h_s, gb_s, gc_s, *, n_shift, seq, row_blk):
    j = pl.program_id(1)

    @pl.when(j == 0)
    def _():
        def body(i, c):
            rows = pl.ds(pl.multiple_of(i * row_blk, row_blk), row_blk)
            h_s[rows, :] = _rmsnorm(x_ref[rows, :], n1_ref[...]).astype(BF16)
            return c
        lax.fori_loop(0, seq // row_blk, body, 0)

    p = jnp.dot(h_s[...], w_ref[...], preferred_element_type=F32)
    row = lax.broadcasted_iota(jnp.int32, p.shape, 0)

    def prev(u):
        return jnp.where(row == 0, 0.0, pltpu.roll(u, 1, 0))

    def nxt(u):
        return jnp.where(row == seq - 1, 0.0, pltpu.roll(u, seq - 1, 0))

    @pl.when(j < n_shift)
    def _():
        rkvl_ref[...] = p + mu_ref[...] * (0.5 * (prev(p) + nxt(p)) - p)

    phase = lax.rem(jnp.maximum(j - n_shift, 0), 3)
    is_conv = j >= n_shift

    @pl.when(is_conv & (phase == 0))
    def _():
        gb_s[...] = p

    @pl.when(is_conv & (phase == 1))
    def _():
        gc_s[...] = p

    @pl.when(is_conv & (phase == 2))
    def _():
        u = gc_s[...] * p
        cw = cw_ref[...]
        y = cw[0:1, :] * prev(u) + cw[1:2, :] * u + cw[2:3, :] * nxt(u)
        oconv_ref[...] = (gb_s[...] * y).astype(oconv_ref.dtype)


def _inproj(x, n1, w_p, mu_p, conv_w, *, n_shift):
    bsz, seq, d = x.shape
    n_tiles = w_p.shape[1] // COL_TILE
    d_conv = conv_w.shape[-1]

    def conv_blk(j):
        return jnp.maximum(j - n_shift, 0) // 3

    return pl.pallas_call(
        functools.partial(_inproj_body, n_shift=n_shift, seq=seq, row_blk=256),
        grid=(bsz, n_tiles),
        in_specs=[
            pl.BlockSpec((None, seq, d), lambda b, j: (b, 0, 0)),
            pl.BlockSpec((1, d), lambda b, j: (0, 0)),
            pl.BlockSpec((d, COL_TILE), lambda b, j: (0, j)),
            pl.BlockSpec((1, COL_TILE), lambda b, j: (0, jnp.minimum(j, n_shift - 1))),
            pl.BlockSpec((3, COL_TILE), lambda b, j: (0, conv_blk(j))),
        ],
        out_specs=[
            pl.BlockSpec((None, seq, COL_TILE), lambda b, j: (b, 0, jnp.minimum(j, n_shift - 1))),
            pl.BlockSpec((None, seq, COL_TILE), lambda b, j: (b, 0, conv_blk(j))),
        ],
        out_shape=[
            jax.ShapeDtypeStruct((bsz, seq, n_shift * COL_TILE), F32),
            jax.ShapeDtypeStruct((bsz, seq, d_conv), BF16),
        ],
        scratch_shapes=[
            pltpu.VMEM((seq, d), BF16),
            pltpu.VMEM((seq, COL_TILE), F32),
            pltpu.VMEM((seq, COL_TILE), F32),
        ],
        compiler_params=pltpu.CompilerParams(
            dimension_semantics=("arbitrary", "arbitrary"), vmem_limit_bytes=VMEM_LIMIT),
        name="inproj",
    )(x, n1, w_p, mu_p, conv_w)


def _prep_body(r_ref, k_ref, v_ref, l_ref, vec_ref, wup_ref, aup_ref, gup_ref, ones_ref,
               out_ref, *, dr):
    r = r_ref[...]
    k = k_ref[...]
    v = v_ref[...]
    vec = vec_ref[...]
    ones_bd = ones_ref[...]

    def vrow(i):
        return vec[i:i + 1, :]

    t128 = l_ref[:, 0:128]
    lw = jnp.dot(jnp.tanh(t128).astype(BF16), wup_ref[...], preferred_element_type=F32)
    aa = jnp.dot(t128.astype(BF16), aup_ref[...], preferred_element_type=F32)
    lw_f = -LOG_DECAY_SCALE * jax.nn.sigmoid(vrow(V_W0F) + lw[:, :dr])
    lw_b = -LOG_DECAY_SCALE * jax.nn.sigmoid(vrow(V_W0B) + lw[:, dr:])
    a_f = jax.nn.sigmoid(vrow(V_A0F) + aa[:, :dr])
    a_b = jax.nn.sigmoid(vrow(V_A0B) + aa[:, dr:])
    g = jnp.dot(jax.nn.sigmoid(l_ref[:, 256:512]).astype(BF16), gup_ref[...],
                preferred_element_type=F32)

    kkr = k * vrow(V_KK)
    ssq = _head_sum(kkr * kkr, ones_bd)
    kk = kkr / jnp.maximum(jnp.sqrt(ssq), NORM_EPS)
    kd_f = k * (1.0 + (a_f - 1.0) * vrow(V_KAF))
    kd_b = k * (1.0 + (a_b - 1.0) * vrow(V_KAB))
    bonus = _head_sum(r * kd_f * vrow(V_RKF) + r * kd_b * vrow(V_RKB), ones_bd) * v

    def put(i, val):
        out_ref[:, i * dr:(i + 1) * dr] = val

    put(P_KK, kk)
    put(P_BF, a_f * kk)
    put(P_BB, a_b * kk)
    put(P_KDF, kd_f)
    put(P_KDB, kd_b)
    put(P_LWF, lw_f)
    put(P_LWB, lw_b)
    put(P_G, g)
    put(P_BONUS, bonus)


def _prep(rkvl, vecs, wup, aup, gup, ones_bd, *, dr, row_tile):
    bsz, seq, _ = rkvl.shape

    def col(c):
        return pl.BlockSpec((None, row_tile, dr), lambda b, i: (b, i, c))

    def full(a):
        return pl.BlockSpec(a.shape, lambda b, i: (0,) * a.ndim)

    return pl.pallas_call(
        functools.partial(_prep_body, dr=dr),
        grid=(bsz, seq // row_tile),
        in_specs=[col(0), col(1), col(2), col(3),
                  full(vecs), full(wup), full(aup), full(gup), full(ones_bd)],
        out_specs=pl.BlockSpec((None, row_tile, P_NUM * dr), lambda b, i: (b, i, 0)),
        out_shape=jax.ShapeDtypeStruct((bsz, seq, P_NUM * dr), F32),
        compiler_params=pltpu.CompilerParams(
            dimension_semantics=("arbitrary", "arbitrary"), vmem_limit_bytes=VMEM_LIMIT),
        name="prep",
    )(rkvl, rkvl, rkvl, rkvl, vecs, wup, aup, gup, ones_bd)


def _block_diag(x, lane_head):
    return jnp.concatenate([jnp.where(lane_head == h, x, 0.0) for h in range(QUAD)],
                           axis=0).astype(BF16)


def _decay_factors(lw, *, forward):
    c = CHUNK
    tri_t = lax.broadcasted_iota(jnp.int32, (c, 3 * c), 0)
    tri_s = lax.broadcasted_iota(jnp.int32, (c, 3 * c), 1) & (c - 1)
    tri = ((tri_s <= tri_t) if forward else (tri_s >= tri_t)).astype(BF16)
    cs = jnp.dot(tri, jnp.concatenate(_split3(lw), axis=0), preferred_element_type=F32)
    cs_end = cs[c - 1:c, :] if forward else cs[0:1, :]
    return dict(e_t=jnp.exp(cs), e_prev=jnp.exp(cs - lw), e_inv=jnp.exp(-cs),
                e_end=jnp.exp(cs_end - cs), w_c=jnp.exp(cs_end))


def _scan_chunk(dirs, st_ref):
    c = CHUNK
    t_idx = lax.broadcasted_iota(jnp.int32, (c, MXU_DIM), 0)
    lane = lax.broadcasted_iota(jnp.int32, (c, MXU_DIM), 1)
    s_idx = lane & (c - 1)
    lane_head = lane >> 6
    eye = (s_idx == t_idx).astype(F32)
    bd_mask = ((lax.broadcasted_iota(jnp.int32, (MXU_DIM, MXU_DIM), 0) >> 6)
               == (lax.broadcasted_iota(jnp.int32, (MXU_DIM, MXU_DIM), 1) >> 6))

    def bd(x):
        return _block_diag(x, lane_head)

    chains = []
    for forward, a, y_ref in dirs:
        f = _decay_factors(a["lw"], forward=forward)
        if forward:
            strict, incl = s_idx < t_idx, s_idx <= t_idx
        else:
            strict, incl = s_idx > t_idx, s_idx >= t_idx
        kap, rt = a["kk"] * f["e_prev"], a["r"] * f["e_t"]
        kt, bt = a["kd"] * f["e_inv"], a["b"] * f["e_inv"]
        kh, bh = a["kd"] * f["e_end"], a["b"] * f["e_end"]
        for q in range(a["r"].shape[-1] // MXU_DIM):
            sl = slice(q * MXU_DIM, (q + 1) * MXU_DIM)
            chains.append(dict(
                strict=strict, incl=incl, y_ref=y_ref, sl=sl, idx=len(chains),
                kr=jnp.concatenate([kap[:, sl], rt[:, sl]], axis=0).astype(BF16),
                kt=kt[:, sl], bt=bt[:, sl], v=a["v"][:, sl], w_c=f["w_c"][:, sl],
                rhs=jnp.concatenate([kh[:, sl], bh[:, sl]], axis=0).astype(BF16)))

    for ch in chains:
        g_b = _dot_nt(ch["kr"], bd(ch["bt"]))
        g_k = _dot_nt(ch["kr"], bd(ch["kt"]))
        ch["arb"] = jnp.where(ch["incl"], g_b[c:], 0.0)
        ch["aa"] = jnp.concatenate([jnp.where(ch["strict"], g_k[:c], 0.0),
                                    jnp.where(ch["incl"], g_k[c:], 0.0)], axis=0)
        ch["x"] = -jnp.where(ch["strict"], g_b[:c], 0.0)
        ch["t"] = eye + ch["x"]

    for ch in chains:
        ch["mt"] = st_ref[ch["idx"]]
        ch["krm"] = _dot_nt(ch["kr"], ch["mt"])
        ch["av"] = _dot(ch["aa"], bd(ch["v"]))

    n_factors = CHUNK.bit_length() - 1
    for ch in chains:
        ch["x"] = _dot(ch["x"], bd(ch["x"]))
    for _ in range(n_factors - 2):
        for ch in chains:
            rr = _dot(jnp.concatenate([ch["x"], ch["t"]], axis=0), bd(ch["x"]))
            ch["x"] = rr[:c]
            ch["t"] = ch["t"] + rr[c:]
    for ch in chains:
        ch["t"] = ch["t"] + _dot(ch["t"], bd(ch["x"]))

    for ch in chains:
        ch["sa"] = _dot(ch["t"], bd(ch["krm"][:c] + ch["av"][:c]))
    for ch in chains:
        sa = ch["sa"]
        ch["y_ref"][:, ch["sl"]] = ch["krm"][c:] + ch["av"][c:] - _dot(ch["arb"], bd(sa))
        lhs_t = jnp.concatenate([ch["v"], -sa], axis=0).T
        upd = _dot(lhs_t, ch["rhs"])
        st_ref[ch["idx"]] = ch["w_c"] * ch["mt"] + jnp.where(bd_mask, upd, 0.0)


def _scan_body(rf, vf, kkf, bf, kdf, lwf, rb, vb, kkb, bb, kdb, lwb, yf_ref, yb_ref, st_ref):
    @pl.when(pl.program_id(1) == 0)
    def _():
        st_ref[...] = jnp.zeros_like(st_ref)

    def operands(r, v, kk, b, kd, lw):
        return dict(r=r[...], v=v[...], kk=kk[...], b=b[...], kd=kd[...], lw=lw[...])

    _scan_chunk([(True, operands(rf, vf, kkf, bf, kdf, lwf), yf_ref),
                 (False, operands(rb, vb, kkb, bb, kdb, lwb), yb_ref)], st_ref)


def _scan(rkvl, prep, *, dr):
    bsz, seq, _ = rkvl.shape
    nc = seq // CHUNK
    n_quad = dr // MXU_DIM

    def fwd(col):
        return pl.BlockSpec((None, CHUNK, dr), lambda b, c: (b, c, col))

    def bwd(col):
        return pl.BlockSpec((None, CHUNK, dr), lambda b, c: (b, nc - 1 - c, col))

    return pl.pallas_call(
        _scan_body,
        grid=(bsz, nc),
        in_specs=[fwd(0), fwd(2), fwd(P_KK), fwd(P_BF), fwd(P_KDF), fwd(P_LWF),
                  bwd(0), bwd(2), bwd(P_KK), bwd(P_BB), bwd(P_KDB), bwd(P_LWB)],
        out_specs=[pl.BlockSpec((None, CHUNK, dr), lambda b, c: (b, c, 0)),
                   pl.BlockSpec((None, CHUNK, dr), lambda b, c: (b, nc - 1 - c, 0))],
        out_shape=[jax.ShapeDtypeStruct((bsz, seq, dr), F32)] * 2,
        scratch_shapes=[pltpu.VMEM((2 * n_quad, MXU_DIM, MXU_DIM), F32)],
        compiler_params=pltpu.CompilerParams(
            dimension_semantics=("arbitrary", "arbitrary"), vmem_limit_bytes=VMEM_LIMIT),
        name="scan",
    )(rkvl, rkvl, prep, prep, prep, prep, rkvl, rkvl, prep, prep, prep, prep)


def _post_body(yf_ref, yb_ref, g_ref, bonus_ref, oconv_ref, x_ref, wout_ref, gn_ref, ones_ref,
               out_ref, *, dr):
    ones_bd = ones_ref[...]
    y = yf_ref[...] + yb_ref[...]
    mean = _head_sum(y, ones_bd) * (1.0 / HEAD)
    d = y - mean
    var = _head_sum(d * d, ones_bd) * (1.0 / HEAD)
    yn = d * lax.rsqrt(var + GN_EPS) * gn_ref[0:1, :] + gn_ref[1:2, :] + bonus_ref[...]
    o = (yn * g_ref[...]).astype(BF16)
    out_ref[...] = (x_ref[...]
                    + jnp.dot(o, wout_ref[0:dr, :], preferred_element_type=F32)
                    + jnp.dot(oconv_ref[...], wout_ref[dr:, :], preferred_element_type=F32))


def _post(yf, yb, prep, oconv, x, wout, gn, ones_bd, *, dr, row_tile):
    bsz, seq, d = x.shape

    def rows(width, col=0):
        return pl.BlockSpec((None, row_tile, width), lambda b, i: (b, i, col))

    def full(a):
        return pl.BlockSpec(a.shape, lambda b, i: (0,) * a.ndim)

    return pl.pallas_call(
        functools.partial(_post_body, dr=dr),
        grid=(bsz, seq // row_tile),
        in_specs=[rows(dr), rows(dr), rows(dr, P_G), rows(dr, P_BONUS), rows(oconv.shape[-1]),
                  rows(d), full(wout), full(gn), full(ones_bd)],
        out_specs=rows(d),
        out_shape=jax.ShapeDtypeStruct((bsz, seq, d), F32),
        compiler_params=pltpu.CompilerParams(
            dimension_semantics=("arbitrary", "arbitrary"), vmem_limit_bytes=VMEM_LIMIT),
        name="post",
    )(yf, yb, prep, prep, oconv, x, wout, gn, ones_bd)


def _ffn_body(x_ref, n2_ref, wg_ref, wu_ref, wd_ref, nf_ref, out_ref, *, final_norm):
    x = x_ref[...]
    h = _rmsnorm(x, n2_ref[...]).astype(BF16)
    a = jnp.dot(h, wg_ref[...], preferred_element_type=F32)
    u = jnp.dot(h, wu_ref[...], preferred_element_type=F32)
    z = (a * jax.nn.sigmoid(a) * u).astype(BF16)
    x2 = x + jnp.dot(z, wd_ref[...], preferred_element_type=F32)
    out_ref[...] = _rmsnorm(x2, nf_ref[...]) if final_norm else x2


def _ffn(x, n2, wg, wu, wd, nf, *, final_norm, row_tile):
    n, d = x.shape

    def full(a):
        return pl.BlockSpec(a.shape, lambda i: (0,) * a.ndim)

    return pl.pallas_call(
        functools.partial(_ffn_body, final_norm=final_norm),
        grid=(n // row_tile,),
        in_specs=[pl.BlockSpec((row_tile, d), lambda i: (i, 0)),
                  full(n2), full(wg), full(wu), full(wd), full(nf)],
        out_specs=pl.BlockSpec((row_tile, d), lambda i: (i, 0)),
        out_shape=jax.ShapeDtypeStruct((n, d), F32),
        compiler_params=pltpu.CompilerParams(
            dimension_semantics=("arbitrary",), vmem_limit_bytes=VMEM_LIMIT),
        name="ffn",
    )(x, n2, wg, wu, wd, nf)


def _pad_cols(a, width):
    return jnp.pad(a, ((0, 0), (0, width - a.shape[1])))


def _pad_rows(a, height):
    return jnp.pad(a, ((0, height - a.shape[0]), (0, 0)))


def kernel(x, norm1_w, w_in, mu_shift, w_up_f, w0_f, w_up_b, w0_b, a_up_f, a0_f, a_up_b, a0_b,
           g_up, k_k, k_a_f, k_a_b, r_k_f, r_k_b, gn_w, gn_b, conv_w, w_out, norm2_w, w_gate,
           w_up, w_down, norm_f_w):
    bsz, seq, d = x.shape
    depth = w_in.shape[0]
    dr = w0_f.shape[-1]
    n_dec, n_aaa, n_gate = w_up_f.shape[1], a_up_f.shape[1], g_up.shape[1]
    d_conv = conv_w.shape[-1]
    assert dr % MXU_DIM == 0 and d_conv % COL_TILE == 0 and seq % CHUNK == 0
    assert n_dec + n_aaa <= 128 and n_gate <= 256
    o_xw = 3 * dr
    o_xa = o_xw + n_dec
    o_xg = o_xa + n_aaa
    o_conv = o_xg + n_gate
    n_shift = (3 * dr + 2 * COL_TILE) // COL_TILE

    head_id = jnp.arange(dr) // HEAD
    ones_bd = (head_id[:, None] == head_id[None, :]).astype(BF16)

    for l in range(depth):
        w = w_in[l]
        gb, gc, hh = (w[:, o_conv + i * d_conv:o_conv + (i + 1) * d_conv] for i in range(3))
        conv_cols = []
        for t in range(d_conv // COL_TILE):
            cs = slice(t * COL_TILE, (t + 1) * COL_TILE)
            conv_cols += [gb[:, cs], gc[:, cs], hh[:, cs]]
        w_p = jnp.concatenate(
            [w[:, :o_xw], _pad_cols(w[:, o_xw:o_xg], COL_TILE), _pad_cols(w[:, o_xg:o_conv], COL_TILE)]
            + conv_cols, axis=1).astype(BF16)
        mu = mu_shift[l][None, :]
        mu_p = jnp.concatenate(
            [mu[:, :o_xw], _pad_cols(mu[:, o_xw:o_xg], COL_TILE), _pad_cols(mu[:, o_xg:o_conv], COL_TILE)],
            axis=1)

        rkvl, oconv = _inproj(x, norm1_w[l][None, :], w_p, mu_p, conv_w[l], n_shift=n_shift)

        vecs = jnp.stack([w0_f[l], w0_b[l], a0_f[l], a0_b[l], k_k[l], k_a_f[l], k_a_b[l],
                          r_k_f[l].reshape(dr), r_k_b[l].reshape(dr)], axis=0)
        vecs = _pad_rows(vecs, 16)
        wup = _pad_rows(jnp.concatenate([w_up_f[l], w_up_b[l]], axis=1), 128).astype(BF16)
        aup = jnp.concatenate([a_up_f[l], a_up_b[l]], axis=1)
        aup = jnp.pad(aup, ((n_dec, 128 - n_dec - n_aaa), (0, 0))).astype(BF16)
        gup = _pad_rows(g_up[l], 256).astype(BF16)
        prep = _prep(rkvl, vecs, wup, aup, gup, ones_bd, dr=dr, row_tile=512)

        yf, yb = _scan(rkvl, prep, dr=dr)

        gn = jnp.stack([gn_w[l], gn_b[l]], axis=0)
        x1 = _post(yf, yb, prep, oconv, x, w_out[l].astype(BF16), gn, ones_bd, dr=dr, row_tile=512)

        x = _ffn(x1.reshape(bsz * seq, d), norm2_w[l][None, :], w_gate[l].astype(BF16),
                 w_up[l].astype(BF16), w_down[l].astype(BF16), norm_f_w[None, :],
                 final_norm=(l == depth - 1), row_tile=256).reshape(bsz, seq, d)
    return x
```

```python
import functools

import jax
import jax.numpy as jnp
from jax import lax
from jax.experimental import pallas as pl
from jax.experimental.pallas import tpu as pltpu

F32 = jnp.float32
BF16 = jnp.bfloat16

HEAD = 64
CHUNK = 64
MXU_DIM = 256
QUAD = MXU_DIM // CHUNK
COL_TILE = 256
LOG_DECAY_SCALE = 0.606531
RMS_EPS = 1e-6
GN_EPS = 64e-5
NORM_EPS = 1e-12
VMEM_LIMIT = 56 * 1024 * 1024

P_KK, P_BF, P_BB, P_KDF, P_KDB, P_G, P_BONUS, P_NUM = range(8)
L_F, L_B, L_NUM = range(3)
V_W0F, V_W0B, V_A0F, V_A0B, V_KK, V_KAF, V_KAB, V_RKF, V_RKB, V_NUM = range(10)


def _dot(a, b):
    return jnp.dot(a.astype(BF16), b.astype(BF16), preferred_element_type=F32)


def _dot_nt(a, b):
    return lax.dot_general(a.astype(BF16), b.astype(BF16), (((1,), (1,)), ((), ())),
                           preferred_element_type=F32)


def _split2(x):
    hi = x.astype(BF16)
    lo = (x - hi.astype(F32)).astype(BF16)
    return hi, lo


def _split3(x):
    hi = x.astype(BF16)
    r1 = x - hi.astype(F32)
    mid = r1.astype(BF16)
    lo = (r1 - mid.astype(F32)).astype(BF16)
    return hi, mid, lo


def _head_sum(x, ones_bd):
    hi, lo = _split2(x)
    return (jnp.dot(hi, ones_bd, preferred_element_type=F32)
            + jnp.dot(lo, ones_bd, preferred_element_type=F32))


def _rmsnorm(x, w):
    ms = jnp.mean(x * x, axis=-1, keepdims=True)
    return x * lax.rsqrt(ms + RMS_EPS) * w


def _inproj_body(x_ref, n1_ref, w_ref, mu_ref, cw_ref, rkvl_ref, oconv_ref,
                 h_s, gb_s, gc_s, *, n_shift, seq, row_blk):
    j = pl.program_id(1)

    @pl.when(j == 0)
    def _():
        def body(i, c):
            rows = pl.ds(pl.multiple_of(i * row_blk, row_blk), row_blk)
            h_s[rows, :] = _rmsnorm(x_ref[rows, :], n1_ref[...]).astype(BF16)
            return c
        lax.fori_loop(0, seq // row_blk, body, 0)

    p = jnp.dot(h_s[...], w_ref[...], preferred_element_type=F32)
    row = lax.broadcasted_iota(jnp.int32, p.shape, 0)

    def prev(u):
        return jnp.where(row == 0, 0.0, pltpu.roll(u, 1, 0))

    def nxt(u):
        return jnp.where(row == seq - 1, 0.0, pltpu.roll(u, seq - 1, 0))

    @pl.when(j < n_shift)
    def _():
        rkvl_ref[...] = (p + mu_ref[...] * (0.5 * (prev(p) + nxt(p)) - p)).astype(rkvl_ref.dtype)

    phase = lax.rem(jnp.maximum(j - n_shift, 0), 3)
    is_conv = j >= n_shift

    @pl.when(is_conv & (phase == 0))
    def _():
        gb_s[...] = p

    @pl.when(is_conv & (phase == 1))
    def _():
        gc_s[...] = p

    @pl.when(is_conv & (phase == 2))
    def _():
        u = gc_s[...] * p
        cw = cw_ref[...]
        y = cw[0:1, :] * prev(u) + cw[1:2, :] * u + cw[2:3, :] * nxt(u)
        oconv_ref[...] = (gb_s[...] * y).astype(oconv_ref.dtype)


def _inproj(x, n1, w_p, mu_p, conv_w, *, n_shift):
    bsz, seq, d = x.shape
    n_tiles = w_p.shape[1] // COL_TILE
    d_conv = conv_w.shape[-1]

    def conv_blk(j):
        return jnp.maximum(j - n_shift, 0) // 3

    return pl.pallas_call(
        functools.partial(_inproj_body, n_shift=n_shift, seq=seq, row_blk=256),
        grid=(bsz, n_tiles),
        in_specs=[
            pl.BlockSpec((None, seq, d), lambda b, j: (b, 0, 0)),
            pl.BlockSpec((1, d), lambda b, j: (0, 0)),
            pl.BlockSpec((d, COL_TILE), lambda b, j: (0, j)),
            pl.BlockSpec((1, COL_TILE), lambda b, j: (0, jnp.minimum(j, n_shift - 1))),
            pl.BlockSpec((3, COL_TILE), lambda b, j: (0, conv_blk(j))),
        ],
        out_specs=[
            pl.BlockSpec((None, seq, COL_TILE), lambda b, j: (b, 0, jnp.minimum(j, n_shift - 1))),
            pl.BlockSpec((None, seq, COL_TILE), lambda b, j: (b, 0, conv_blk(j))),
        ],
        out_shape=[
            jax.ShapeDtypeStruct((bsz, seq, n_shift * COL_TILE), BF16),
            jax.ShapeDtypeStruct((bsz, seq, d_conv), BF16),
        ],
        scratch_shapes=[
            pltpu.VMEM((seq, d), BF16),
            pltpu.VMEM((seq, COL_TILE), F32),
            pltpu.VMEM((seq, COL_TILE), F32),
        ],
        compiler_params=pltpu.CompilerParams(
            dimension_semantics=("arbitrary", "arbitrary"), vmem_limit_bytes=VMEM_LIMIT),
        name="inproj",
    )(x, n1, w_p, mu_p, conv_w)


def _prep_body(r_ref, k_ref, v_ref, l_ref, vec_ref, wup_ref, aup_ref, gup_ref, ones_ref,
               out_ref, lw_ref, *, dr):
    r = r_ref[...].astype(F32)
    k = k_ref[...].astype(F32)
    v = v_ref[...].astype(F32)
    vec = vec_ref[...]
    ones_bd = ones_ref[...]

    def vrow(i):
        return vec[i:i + 1, :]

    t128 = l_ref[:, 0:128]
    lw = jnp.dot(jnp.tanh(t128.astype(F32)).astype(BF16), wup_ref[...],
                 preferred_element_type=F32)
    aa = jnp.dot(t128, aup_ref[...], preferred_element_type=F32)
    lw_f = -LOG_DECAY_SCALE * jax.nn.sigmoid(vrow(V_W0F) + lw[:, :dr])
    lw_b = -LOG_DECAY_SCALE * jax.nn.sigmoid(vrow(V_W0B) + lw[:, dr:])
    a_f = jax.nn.sigmoid(vrow(V_A0F) + aa[:, :dr])
    a_b = jax.nn.sigmoid(vrow(V_A0B) + aa[:, dr:])
    g = jnp.dot(jax.nn.sigmoid(l_ref[:, 256:512].astype(F32)).astype(BF16), gup_ref[...],
                preferred_element_type=F32)

    kkr = k * vrow(V_KK)
    ssq = _head_sum(kkr * kkr, ones_bd)
    kk = kkr / jnp.maximum(jnp.sqrt(ssq), NORM_EPS)
    kd_f = k * (1.0 + (a_f - 1.0) * vrow(V_KAF))
    kd_b = k * (1.0 + (a_b - 1.0) * vrow(V_KAB))
    bonus = _head_sum(r * kd_f * vrow(V_RKF) + r * kd_b * vrow(V_RKB), ones_bd) * v

    def put(i, val):
        out_ref[:, i * dr:(i + 1) * dr] = val.astype(out_ref.dtype)

    put(P_KK, kk)
    put(P_BF, a_f * kk)
    put(P_BB, a_b * kk)
    put(P_KDF, kd_f)
    put(P_KDB, kd_b)
    put(P_G, g)
    put(P_BONUS, bonus)
    lw_ref[:, L_F * dr:(L_F + 1) * dr] = lw_f
    lw_ref[:, L_B * dr:(L_B + 1) * dr] = lw_b


def _prep(rkvl, vecs, wup, aup, gup, ones_bd, *, dr, row_tile):
    bsz, seq, _ = rkvl.shape

    def col(c):
        return pl.BlockSpec((None, row_tile, dr), lambda b, i: (b, i, c))

    def full(a):
        return pl.BlockSpec(a.shape, lambda b, i: (0,) * a.ndim)

    return pl.pallas_call(
        functools.partial(_prep_body, dr=dr),
        grid=(bsz, seq // row_tile),
        in_specs=[col(0), col(1), col(2), col(3),
                  full(vecs), full(wup), full(aup), full(gup), full(ones_bd)],
        out_specs=[pl.BlockSpec((None, row_tile, P_NUM * dr), lambda b, i: (b, i, 0)),
                   pl.BlockSpec((None, row_tile, L_NUM * dr), lambda b, i: (b, i, 0))],
        out_shape=[jax.ShapeDtypeStruct((bsz, seq, P_NUM * dr), BF16),
                   jax.ShapeDtypeStruct((bsz, seq, L_NUM * dr), F32)],
        compiler_params=pltpu.CompilerParams(
            dimension_semantics=("arbitrary", "arbitrary"), vmem_limit_bytes=VMEM_LIMIT),
        name="prep",
    )(rkvl, rkvl, rkvl, rkvl, vecs, wup, aup, gup, ones_bd)


def _block_diag(x, lane_head):
    return jnp.concatenate([jnp.where(lane_head == h, x, 0.0) for h in range(QUAD)],
                           axis=0).astype(BF16)


def _decay_factors(lw, *, forward):
    c = CHUNK
    tri_t = lax.broadcasted_iota(jnp.int32, (c, 3 * c), 0)
    tri_s = lax.broadcasted_iota(jnp.int32, (c, 3 * c), 1) & (c - 1)
    tri = ((tri_s <= tri_t) if forward else (tri_s >= tri_t)).astype(BF16)
    cs = jnp.dot(tri, jnp.concatenate(_split3(lw), axis=0), preferred_element_type=F32)
    cs_end = cs[c - 1:c, :] if forward else cs[0:1, :]
    return dict(e_t=jnp.exp(cs), e_prev=jnp.exp(cs - lw), e_inv=jnp.exp(-cs),
                e_end=jnp.exp(cs_end - cs), w_c=jnp.exp(cs_end))


def _scan_chunk(dirs, st_ref):
    c = CHUNK
    t_idx = lax.broadcasted_iota(jnp.int32, (c, MXU_DIM), 0)
    lane = lax.broadcasted_iota(jnp.int32, (c, MXU_DIM), 1)
    s_idx = lane & (c - 1)
    lane_head = lane >> 6
    eye = (s_idx == t_idx).astype(F32)
    bd_mask = ((lax.broadcasted_iota(jnp.int32, (MXU_DIM, MXU_DIM), 0) >> 6)
               == (lax.broadcasted_iota(jnp.int32, (MXU_DIM, MXU_DIM), 1) >> 6))

    def bd(x):
        return _block_diag(x, lane_head)

    chains = []
    for forward, a, y_ref in dirs:
        f = _decay_factors(a["lw"], forward=forward)
        if forward:
            strict, incl = s_idx < t_idx, s_idx <= t_idx
        else:
            strict, incl = s_idx > t_idx, s_idx >= t_idx
        kap, rt = a["kk"] * f["e_prev"], a["r"] * f["e_t"]
        kt, bt = a["kd"] * f["e_inv"], a["b"] * f["e_inv"]
        kh, bh = a["kd"] * f["e_end"], a["b"] * f["e_end"]
        for q in range(a["r"].shape[-1] // MXU_DIM):
            sl = slice(q * MXU_DIM, (q + 1) * MXU_DIM)
            chains.append(dict(
                strict=strict, incl=incl, y_ref=y_ref, sl=sl, idx=len(chains),
                kr=jnp.concatenate([kap[:, sl], rt[:, sl]], axis=0).astype(BF16),
                kt=kt[:, sl], bt=bt[:, sl], v=a["v"][:, sl], w_c=f["w_c"][:, sl],
                rhs=jnp.concatenate([kh[:, sl], bh[:, sl]], axis=0).astype(BF16)))

    for ch in chains:
        g_b = _dot_nt(ch["kr"], bd(ch["bt"]))
        g_k = _dot_nt(ch["kr"], bd(ch["kt"]))
        ch["arb"] = jnp.where(ch["incl"], g_b[c:], 0.0)
        ch["aa"] = jnp.concatenate([jnp.where(ch["strict"], g_k[:c], 0.0),
                                    jnp.where(ch["incl"], g_k[c:], 0.0)], axis=0)
        ch["x"] = -jnp.where(ch["strict"], g_b[:c], 0.0)
        ch["t"] = eye + ch["x"]

    for ch in chains:
        ch["mt"] = st_ref[ch["idx"]]
        ch["krm"] = _dot_nt(ch["kr"], ch["mt"])
        ch["av"] = _dot(ch["aa"], bd(ch["v"]))

    n_factors = CHUNK.bit_length() - 1
    for ch in chains:
        ch["x"] = _dot(ch["x"], bd(ch["x"]))
    for _ in range(n_factors - 2):
        for ch in chains:
            rr = _dot(jnp.concatenate([ch["x"], ch["t"]], axis=0), bd(ch["x"]))
            ch["x"] = rr[:c]
            ch["t"] = ch["t"] + rr[c:]
    for ch in chains:
        ch["t"] = ch["t"] + _dot(ch["t"], bd(ch["x"]))

    for ch in chains:
        ch["sa"] = _dot(ch["t"], bd(ch["krm"][:c] + ch["av"][:c]))
    for ch in chains:
        sa = ch["sa"]
        y = ch["krm"][c:] + ch["av"][c:] - _dot(ch["arb"], bd(sa))
        ch["y_ref"][:, ch["sl"]] = y.astype(ch["y_ref"].dtype)
        lhs_t = jnp.concatenate([ch["v"], -sa], axis=0).T
        upd = _dot(lhs_t, ch["rhs"])
        st_ref[ch["idx"]] = ch["w_c"] * ch["mt"] + jnp.where(bd_mask, upd, 0.0)


def _scan_body(rf, vf, kkf, bf, kdf, lwf, rb, vb, kkb, bb, kdb, lwb, yf_ref, yb_ref, st_ref):
    @pl.when(pl.program_id(1) == 0)
    def _():
        st_ref[...] = jnp.zeros_like(st_ref)

    def operands(r, v, kk, b, kd, lw):
        return dict(r=r[...].astype(F32), v=v[...].astype(F32), kk=kk[...].astype(F32),
                    b=b[...].astype(F32), kd=kd[...].astype(F32), lw=lw[...])

    _scan_chunk([(True, operands(rf, vf, kkf, bf, kdf, lwf), yf_ref),
                 (False, operands(rb, vb, kkb, bb, kdb, lwb), yb_ref)], st_ref)


def _scan(rkvl, prep, logw, *, dr):
    bsz, seq, _ = rkvl.shape
    nc = seq // CHUNK
    n_quad = dr // MXU_DIM

    def fwd(col):
        return pl.BlockSpec((None, CHUNK, dr), lambda b, c: (b, c, col))

    def bwd(col):
        return pl.BlockSpec((None, CHUNK, dr), lambda b, c: (b, nc - 1 - c, col))

    return pl.pallas_call(
        _scan_body,
        grid=(bsz, nc),
        in_specs=[fwd(0), fwd(2), fwd(P_KK), fwd(P_BF), fwd(P_KDF), fwd(L_F),
                  bwd(0), bwd(2), bwd(P_KK), bwd(P_BB), bwd(P_KDB), bwd(L_B)],
        out_specs=[pl.BlockSpec((None, CHUNK, dr), lambda b, c: (b, c, 0)),
                   pl.BlockSpec((None, CHUNK, dr), lambda b, c: (b, nc - 1 - c, 0))],
        out_shape=[jax.ShapeDtypeStruct((bsz, seq, dr), BF16)] * 2,
        scratch_shapes=[pltpu.VMEM((2 * n_quad, MXU_DIM, MXU_DIM), F32)],
        compiler_params=pltpu.CompilerParams(
            dimension_semantics=("arbitrary", "arbitrary"), vmem_limit_bytes=VMEM_LIMIT),
        name="scan",
    )(rkvl, rkvl, prep, prep, prep, logw, rkvl, rkvl, prep, prep, prep, logw)


def _post_body(yf_ref, yb_ref, g_ref, bonus_ref, oconv_ref, x_ref, wout_ref, gn_ref, ones_ref,
               out_ref, *, dr):
    ones_bd = ones_ref[...]
    y = yf_ref[...].astype(F32) + yb_ref[...].astype(F32)
    mean = _head_sum(y, ones_bd) * (1.0 / HEAD)
    d = y - mean
    var = _head_sum(d * d, ones_bd) * (1.0 / HEAD)
    yn = (d * lax.rsqrt(var + GN_EPS) * gn_ref[0:1, :] + gn_ref[1:2, :]
          + bonus_ref[...].astype(F32))
    o = (yn * g_ref[...].astype(F32)).astype(BF16)
    out_ref[...] = (x_ref[...]
                    + jnp.dot(o, wout_ref[0:dr, :], preferred_element_type=F32)
                    + jnp.dot(oconv_ref[...], wout_ref[dr:, :], preferred_element_type=F32))


def _post(yf, yb, prep, oconv, x, wout, gn, ones_bd, *, dr, row_tile):
    bsz, seq, d = x.shape

    def rows(width, col=0):
        return pl.BlockSpec((None, row_tile, width), lambda b, i: (b, i, col))

    def full(a):
        return pl.BlockSpec(a.shape, lambda b, i: (0,) * a.ndim)

    return pl.pallas_call(
        functools.partial(_post_body, dr=dr),
        grid=(bsz, seq // row_tile),
        in_specs=[rows(dr), rows(dr), rows(dr, P_G), rows(dr, P_BONUS), rows(oconv.shape[-1]),
                  rows(d), full(wout), full(gn), full(ones_bd)],
        out_specs=rows(d),
        out_shape=jax.ShapeDtypeStruct((bsz, seq, d), F32),
        compiler_params=pltpu.CompilerParams(
            dimension_semantics=("arbitrary", "arbitrary"), vmem_limit_bytes=VMEM_LIMIT),
        name="post",
    )(yf, yb, prep, prep, oconv, x, wout, gn, ones_bd)


def _ffn_body(x_ref, n2_ref, wg_ref, wu_ref, wd_ref, nf_ref, out_ref, *, final_norm):
    x = x_ref[...]
    h = _rmsnorm(x, n2_ref[...]).astype(BF16)
    a = jnp.dot(h, wg_ref[...], preferred_element_type=F32)
    u = jnp.dot(h, wu_ref[...], preferred_element_type=F32)
    z = (a * jax.nn.sigmoid(a) * u).astype(BF16)
    x2 = x + jnp.dot(z, wd_ref[...], preferred_element_type=F32)
    out_ref[...] = _rmsnorm(x2, nf_ref[...]) if final_norm else x2


def _ffn(x, n2, wg, wu, wd, nf, *, final_norm, row_tile):
    n, d = x.shape

    def full(a):
        return pl.BlockSpec(a.shape, lambda i: (0,) * a.ndim, pipeline_mode=pl.Buffered(1))

    return pl.pallas_call(
        functools.partial(_ffn_body, final_norm=final_norm),
        grid=(n // row_tile,),
        in_specs=[pl.BlockSpec((row_tile, d), lambda i: (i, 0)),
                  full(n2), full(wg), full(wu), full(wd), full(nf)],
        out_specs=pl.BlockSpec((row_tile, d), lambda i: (i, 0)),
        out_shape=jax.ShapeDtypeStruct((n, d), F32),
        compiler_params=pltpu.CompilerParams(
            dimension_semantics=("arbitrary",), vmem_limit_bytes=VMEM_LIMIT),
        name="ffn",
    )(x, n2, wg, wu, wd, nf)


def _pad_cols(a, width):
    return jnp.pad(a, ((0, 0), (0, width - a.shape[1])))


def _pad_rows(a, height):
    return jnp.pad(a, ((0, height - a.shape[0]), (0, 0)))


def kernel(x, norm1_w, w_in, mu_shift, w_up_f, w0_f, w_up_b, w0_b, a_up_f, a0_f, a_up_b, a0_b,
           g_up, k_k, k_a_f, k_a_b, r_k_f, r_k_b, gn_w, gn_b, conv_w, w_out, norm2_w, w_gate,
           w_up, w_down, norm_f_w):
    bsz, seq, d = x.shape
    depth = w_in.shape[0]
    dr = w0_f.shape[-1]
    n_dec, n_aaa, n_gate = w_up_f.shape[1], a_up_f.shape[1], g_up.shape[1]
    d_conv = conv_w.shape[-1]
    assert dr % MXU_DIM == 0 and d_conv % COL_TILE == 0 and seq % CHUNK == 0
    assert n_dec + n_aaa <= 128 and n_gate <= 256
    o_xw = 3 * dr
    o_xa = o_xw + n_dec
    o_xg = o_xa + n_aaa
    o_conv = o_xg + n_gate
    n_shift = (3 * dr + 2 * COL_TILE) // COL_TILE

    head_id = jnp.arange(dr) // HEAD
    ones_bd = (head_id[:, None] == head_id[None, :]).astype(BF16)

    for l in range(depth):
        w = w_in[l]
        gb, gc, hh = (w[:, o_conv + i * d_conv:o_conv + (i + 1) * d_conv] for i in range(3))
        conv_cols = []
        for t in range(d_conv // COL_TILE):
            cs = slice(t * COL_TILE, (t + 1) * COL_TILE)
            conv_cols += [gb[:, cs], gc[:, cs], hh[:, cs]]
        w_p = jnp.concatenate(
            [w[:, :o_xw], _pad_cols(w[:, o_xw:o_xg], COL_TILE), _pad_cols(w[:, o_xg:o_conv], COL_TILE)]
            + conv_cols, axis=1).astype(BF16)
        mu = mu_shift[l][None, :]
        mu_p = jnp.concatenate(
            [mu[:, :o_xw], _pad_cols(mu[:, o_xw:o_xg], COL_TILE), _pad_cols(mu[:, o_xg:o_conv], COL_TILE)],
            axis=1)

        rkvl, oconv = _inproj(x, norm1_w[l][None, :], w_p, mu_p, conv_w[l], n_shift=n_shift)

        vecs = jnp.stack([w0_f[l], w0_b[l], a0_f[l], a0_b[l], k_k[l], k_a_f[l], k_a_b[l],
                          r_k_f[l].reshape(dr), r_k_b[l].reshape(dr)], axis=0)
        vecs = _pad_rows(vecs, 16)
        wup = _pad_rows(jnp.concatenate([w_up_f[l], w_up_b[l]], axis=1), 128).astype(BF16)
        aup = jnp.concatenate([a_up_f[l], a_up_b[l]], axis=1)
        aup = jnp.pad(aup, ((n_dec, 128 - n_dec - n_aaa), (0, 0))).astype(BF16)
        gup = _pad_rows(g_up[l], 256).astype(BF16)
        prep, logw = _prep(rkvl, vecs, wup, aup, gup, ones_bd, dr=dr, row_tile=512)

        yf, yb = _scan(rkvl, prep, logw, dr=dr)

        gn = jnp.stack([gn_w[l], gn_b[l]], axis=0)
        x1 = _post(yf, yb, prep, oconv, x, w_out[l].astype(BF16), gn, ones_bd, dr=dr, row_tile=512)

        x = _ffn(x1.reshape(bsz * seq, d), norm2_w[l][None, :], w_gate[l].astype(BF16),
                 w_up[l].astype(BF16), w_down[l].astype(BF16), norm_f_w[None, :],
                 final_norm=(l == depth - 1), row_tile=512).reshape(bsz, seq, d)
    return x
```

```python
import functools

import jax
import jax.numpy as jnp
from jax import lax
from jax.experimental import pallas as pl
from jax.experimental.pallas import tpu as pltpu

F32 = jnp.float32
BF16 = jnp.bfloat16

HEAD = 64
CHUNK = 64
MXU_DIM = 256
QUAD = MXU_DIM // CHUNK
SCAN_SUB = 4
COL_TILE = 256
ROW_CHUNK = 512
HALO = 8
LOG_DECAY_SCALE = 0.606531
RMS_EPS = 1e-6
GN_EPS = 64e-5
NORM_EPS = 1e-12
VMEM_LIMIT = 56 * 1024 * 1024

P_KK, P_BF, P_BB, P_KDF, P_KDB, P_G, P_BONUS, P_NUM = range(8)
L_F, L_B, L_NUM = range(3)
V_W0F, V_W0B, V_A0F, V_A0B, V_KK, V_KAF, V_KAB, V_RKF, V_RKB, V_NUM = range(10)


def _dot(a, b):
    return jnp.dot(a.astype(BF16), b.astype(BF16), preferred_element_type=F32)


def _dot_nt(a, b):
    return lax.dot_general(a.astype(BF16), b.astype(BF16), (((1,), (1,)), ((), ())),
                           preferred_element_type=F32)


def _split2(x):
    hi = x.astype(BF16)
    lo = (x - hi.astype(F32)).astype(BF16)
    return hi, lo


def _split3(x):
    hi = x.astype(BF16)
    r1 = x - hi.astype(F32)
    mid = r1.astype(BF16)
    lo = (r1 - mid.astype(F32)).astype(BF16)
    return hi, mid, lo


def _head_sum(x, ones_bd):
    hi, lo = _split2(x)
    return (jnp.dot(hi, ones_bd, preferred_element_type=F32)
            + jnp.dot(lo, ones_bd, preferred_element_type=F32))


def _rmsnorm(x, w):
    ms = jnp.mean(x * x, axis=-1, keepdims=True)
    return x * lax.rsqrt(ms + RMS_EPS) * w


def _zero_halo(s_ref, seq):
    z = jnp.zeros((HALO, s_ref.shape[1]), s_ref.dtype)
    s_ref[0:HALO, :] = z
    s_ref[HALO + seq:2 * HALO + seq, :] = z


def _windows(s_ref, c0, rows):
    width = s_ref.shape[1]
    win = s_ref[c0:c0 + rows + 2 * HALO, :].reshape(rows // HALO + 2, HALO, width)
    down = pltpu.roll(win, 1, 1)
    up = pltpu.roll(win, HALO - 1, 1)
    sub = lax.broadcasted_iota(jnp.int32, (1, HALO, width), 1)
    prv = jnp.where(sub == 0, down[:-2], down[1:-1])
    nxt = jnp.where(sub == HALO - 1, up[2:], up[1:-1])
    return tuple(a.reshape(rows, width) for a in (win[1:-1], prv, nxt))


def _shift_body(x_ref, n1_ref, w_ref, mu_ref, out_ref, h_ref, p_s, *, seq, row_blk):
    @pl.when(pl.program_id(1) == 0)
    def _():
        _zero_halo(p_s, seq)

        def body(i, c):
            rows = pl.ds(pl.multiple_of(i * row_blk, row_blk), row_blk)
            h_ref[rows, :] = _rmsnorm(x_ref[rows, :], n1_ref[...]).astype(BF16)
            return c
        lax.fori_loop(0, seq // row_blk, body, 0)

    w = w_ref[...]
    mu = mu_ref[...]

    def shift(c0):
        cur, prv, nxt = _windows(p_s, c0, ROW_CHUNK)
        out_ref[c0:c0 + ROW_CHUNK, :] = (cur + mu * (0.5 * (prv + nxt) - cur)).astype(out_ref.dtype)

    for c0 in range(0, seq, ROW_CHUNK):
        p_s[HALO + c0:HALO + c0 + ROW_CHUNK, :] = jnp.dot(
            h_ref[c0:c0 + ROW_CHUNK, :], w, preferred_element_type=F32)
        if c0:
            shift(c0 - ROW_CHUNK)
    shift(seq - ROW_CHUNK)


def _conv_body(h_ref, w_ref, cw_ref, out_ref, u_s, gb_s, *, seq):
    @pl.when(pl.program_id(1) == 0)
    def _():
        _zero_halo(u_s, seq)

    w = w_ref[...]
    cw = cw_ref[...]

    def conv(c0):
        cur, prv, nxt = _windows(u_s, c0, ROW_CHUNK)
        y = cw[0:1, :] * prv + cw[1:2, :] * cur + cw[2:3, :] * nxt
        out_ref[c0:c0 + ROW_CHUNK, :] = (gb_s[c0:c0 + ROW_CHUNK, :] * y).astype(out_ref.dtype)

    for c0 in range(0, seq, ROW_CHUNK):
        p = jnp.dot(h_ref[c0:c0 + ROW_CHUNK, :], w, preferred_element_type=F32)
        gb_s[c0:c0 + ROW_CHUNK, :] = p[:, :COL_TILE]
        u_s[HALO + c0:HALO + c0 + ROW_CHUNK, :] = p[:, COL_TILE:2 * COL_TILE] * p[:, 2 * COL_TILE:]
        if c0:
            conv(c0 - ROW_CHUNK)
    conv(seq - ROW_CHUNK)


def _inproj(x, n1, w_shift, mu_p, w_conv, conv_w):
    bsz, seq, d = x.shape
    assert seq % ROW_CHUNK == 0
    params = pltpu.CompilerParams(
        dimension_semantics=("arbitrary", "arbitrary"), vmem_limit_bytes=VMEM_LIMIT)
    rkvl, h = pl.pallas_call(
        functools.partial(_shift_body, seq=seq, row_blk=256),
        grid=(bsz, w_shift.shape[1] // COL_TILE),
        in_specs=[
            pl.BlockSpec((None, seq, d), lambda b, j: (b, 0, 0)),
            pl.BlockSpec((1, d), lambda b, j: (0, 0)),
            pl.BlockSpec((d, COL_TILE), lambda b, j: (0, j)),
            pl.BlockSpec((1, COL_TILE), lambda b, j: (0, j)),
        ],
        out_specs=[
            pl.BlockSpec((None, seq, COL_TILE), lambda b, j: (b, 0, j)),
            pl.BlockSpec((None, seq, d), lambda b, j: (b, 0, 0)),
        ],
        out_shape=[
            jax.ShapeDtypeStruct((bsz, seq, w_shift.shape[1]), BF16),
            jax.ShapeDtypeStruct((bsz, seq, d), BF16),
        ],
        scratch_shapes=[pltpu.VMEM((seq + 2 * HALO, COL_TILE), F32)],
        compiler_params=params,
        name="inproj_shift",
    )(x, n1, w_shift, mu_p)
    d_conv = conv_w.shape[-1]
    oconv = pl.pallas_call(
        functools.partial(_conv_body, seq=seq),
        grid=(bsz, d_conv // COL_TILE),
        in_specs=[
            pl.BlockSpec((None, seq, d), lambda b, q: (b, 0, 0)),
            pl.BlockSpec((d, 3 * COL_TILE), lambda b, q: (0, q)),
            pl.BlockSpec((3, COL_TILE), lambda b, q: (0, q)),
        ],
        out_specs=pl.BlockSpec((None, seq, COL_TILE), lambda b, q: (b, 0, q)),
        out_shape=jax.ShapeDtypeStruct((bsz, seq, d_conv), BF16),
        scratch_shapes=[pltpu.VMEM((seq + 2 * HALO, COL_TILE), F32),
                        pltpu.VMEM((seq, COL_TILE), F32)],
        compiler_params=params,
        name="inproj_conv",
    )(h, w_conv, conv_w)
    return rkvl, oconv


def _prep_body(r_ref, k_ref, v_ref, l_ref, vec_ref, wup_ref, aup_ref, gup_ref, ones_ref,
               out_ref, lw_ref, *, dr):
    r = r_ref[...].astype(F32)
    k = k_ref[...].astype(F32)
    v = v_ref[...].astype(F32)
    vec = vec_ref[...]
    ones_bd = ones_ref[...]

    def vrow(i):
        return vec[i:i + 1, :]

    t128 = l_ref[:, 0:128]
    lw = jnp.dot(jnp.tanh(t128.astype(F32)).astype(BF16), wup_ref[...],
                 preferred_element_type=F32)
    aa = jnp.dot(t128, aup_ref[...], preferred_element_type=F32)
    lw_f = -LOG_DECAY_SCALE * jax.nn.sigmoid(vrow(V_W0F) + lw[:, :dr])
    lw_b = -LOG_DECAY_SCALE * jax.nn.sigmoid(vrow(V_W0B) + lw[:, dr:])
    a_f = jax.nn.sigmoid(vrow(V_A0F) + aa[:, :dr])
    a_b = jax.nn.sigmoid(vrow(V_A0B) + aa[:, dr:])
    g = jnp.dot(jax.nn.sigmoid(l_ref[:, 256:512].astype(F32)).astype(BF16), gup_ref[...],
                preferred_element_type=F32)

    kkr = k * vrow(V_KK)
    ssq = _head_sum(kkr * kkr, ones_bd)
    kk = kkr / jnp.maximum(jnp.sqrt(ssq), NORM_EPS)
    kd_f = k * (1.0 + (a_f - 1.0) * vrow(V_KAF))
    kd_b = k * (1.0 + (a_b - 1.0) * vrow(V_KAB))
    bonus = _head_sum(r * kd_f * vrow(V_RKF) + r * kd_b * vrow(V_RKB), ones_bd) * v

    def put(i, val):
        out_ref[:, i * dr:(i + 1) * dr] = val.astype(out_ref.dtype)

    put(P_KK, kk)
    put(P_BF, a_f * kk)
    put(P_BB, a_b * kk)
    put(P_KDF, kd_f)
    put(P_KDB, kd_b)
    put(P_G, g)
    put(P_BONUS, bonus)
    lw_ref[:, L_F * dr:(L_F + 1) * dr] = lw_f
    lw_ref[:, L_B * dr:(L_B + 1) * dr] = lw_b


def _prep(rkvl, vecs, wup, aup, gup, ones_bd, *, dr, row_tile):
    bsz, seq, _ = rkvl.shape

    def col(c):
        return pl.BlockSpec((None, row_tile, dr), lambda b, i: (b, i, c))

    def full(a):
        return pl.BlockSpec(a.shape, lambda b, i: (0,) * a.ndim)

    return pl.pallas_call(
        functools.partial(_prep_body, dr=dr),
        grid=(bsz, seq // row_tile),
        in_specs=[col(0), col(1), col(2), col(3),
                  full(vecs), full(wup), full(aup), full(gup), full(ones_bd)],
        out_specs=[pl.BlockSpec((None, row_tile, P_NUM * dr), lambda b, i: (b, i, 0)),
                   pl.BlockSpec((None, row_tile, L_NUM * dr), lambda b, i: (b, i, 0))],
        out_shape=[jax.ShapeDtypeStruct((bsz, seq, P_NUM * dr), BF16),
                   jax.ShapeDtypeStruct((bsz, seq, L_NUM * dr), F32)],
        compiler_params=pltpu.CompilerParams(
            dimension_semantics=("arbitrary", "arbitrary"), vmem_limit_bytes=VMEM_LIMIT),
        name="prep",
    )(rkvl, rkvl, rkvl, rkvl, vecs, wup, aup, gup, ones_bd)


def _block_diag(x, lane_head):
    return jnp.concatenate([jnp.where(lane_head == h, x, 0.0) for h in range(QUAD)],
                           axis=0).astype(BF16)


def _decay_factors(lw, *, forward):
    c = CHUNK
    tri_t = lax.broadcasted_iota(jnp.int32, (c, 3 * c), 0)
    tri_s = lax.broadcasted_iota(jnp.int32, (c, 3 * c), 1) & (c - 1)
    tri = ((tri_s <= tri_t) if forward else (tri_s >= tri_t)).astype(BF16)
    cs = jnp.dot(tri, jnp.concatenate(_split3(lw), axis=0), preferred_element_type=F32)
    cs_end = cs[c - 1:c, :] if forward else cs[0:1, :]
    return dict(e_t=jnp.exp(cs), e_prev=jnp.exp(cs - lw), e_inv=jnp.exp(-cs),
                e_end=jnp.exp(cs_end - cs), w_c=jnp.exp(cs_end))


def _scan_step(dirs, st_ref, n_sub):
    c = CHUNK
    t_idx = lax.broadcasted_iota(jnp.int32, (c, MXU_DIM), 0)
    lane = lax.broadcasted_iota(jnp.int32, (c, MXU_DIM), 1)
    s_idx = lane & (c - 1)
    lane_head = lane >> 6
    eye = (s_idx == t_idx).astype(F32)
    bd_mask = ((lax.broadcasted_iota(jnp.int32, (MXU_DIM, MXU_DIM), 0) >> 6)
               == (lax.broadcasted_iota(jnp.int32, (MXU_DIM, MXU_DIM), 1) >> 6))

    def bd(x):
        return _block_diag(x, lane_head)

    subs = []
    for k in range(n_sub):
        chains = []
        for i, (forward, refs, y_ref) in enumerate(dirs):
            row0 = (k if forward else n_sub - 1 - k) * c
            rows = slice(row0, row0 + c)
            a = {name: ref[rows, :].astype(F32) for name, ref in refs.items()}
            f = _decay_factors(a["lw"], forward=forward)
            if forward:
                strict, incl = s_idx < t_idx, s_idx <= t_idx
            else:
                strict, incl = s_idx > t_idx, s_idx >= t_idx
            kap, rt = a["kk"] * f["e_prev"], a["r"] * f["e_t"]
            kt, bt = a["kd"] * f["e_inv"], a["b"] * f["e_inv"]
            kh, bh = a["kd"] * f["e_end"], a["b"] * f["e_end"]
            n_quad = a["r"].shape[-1] // MXU_DIM
            for q in range(n_quad):
                sl = slice(q * MXU_DIM, (q + 1) * MXU_DIM)
                chains.append(dict(
                    strict=strict, incl=incl, y_ref=y_ref, rows=rows, sl=sl, idx=i * n_quad + q,
                    kr=jnp.concatenate([kap[:, sl], rt[:, sl]], axis=0).astype(BF16),
                    kt=kt[:, sl], bt=bt[:, sl], v=a["v"][:, sl], w_c=f["w_c"][:, sl],
                    rhs=jnp.concatenate([kh[:, sl], bh[:, sl]], axis=0).astype(BF16)))
        subs.append(chains)
    every = [ch for chains in subs for ch in chains]

    for ch in every:
        g_b = _dot_nt(ch["kr"], bd(ch["bt"]))
        g_k = _dot_nt(ch["kr"], bd(ch["kt"]))
        ch["arb"] = jnp.where(ch["incl"], g_b[c:], 0.0)
        ch["aa"] = jnp.concatenate([jnp.where(ch["strict"], g_k[:c], 0.0),
                                    jnp.where(ch["incl"], g_k[c:], 0.0)], axis=0)
        ch["x"] = -jnp.where(ch["strict"], g_b[:c], 0.0)
        ch["t"] = eye + ch["x"]
    for ch in every:
        ch["av"] = _dot(ch["aa"], bd(ch["v"]))

    n_factors = CHUNK.bit_length() - 1
    for ch in every:
        ch["x"] = _dot(ch["x"], bd(ch["x"]))
    for _ in range(n_factors - 2):
        for ch in every:
            rr = _dot(jnp.concatenate([ch["x"], ch["t"]], axis=0), bd(ch["x"]))
            ch["x"] = rr[:c]
            ch["t"] = ch["t"] + rr[c:]
    for ch in every:
        ch["t"] = ch["t"] + _dot(ch["t"], bd(ch["x"]))

    for chains in subs:
        for ch in chains:
            ch["mt"] = st_ref[ch["idx"]]
            ch["krm"] = _dot_nt(ch["kr"], ch["mt"])
        for ch in chains:
            ch["sa"] = _dot(ch["t"], bd(ch["krm"][:c] + ch["av"][:c]))
        for ch in chains:
            sa = ch["sa"]
            y = ch["krm"][c:] + ch["av"][c:] - _dot(ch["arb"], bd(sa))
            ch["y_ref"][ch["rows"], ch["sl"]] = y.astype(ch["y_ref"].dtype)
            lhs_t = jnp.concatenate([ch["v"], -sa], axis=0).T
            upd = _dot(lhs_t, ch["rhs"])
            st_ref[ch["idx"]] = ch["w_c"] * ch["mt"] + jnp.where(bd_mask, upd, 0.0)


def _scan_body(rf, vf, kkf, bf, kdf, lwf, rb, vb, kkb, bb, kdb, lwb, yf_ref, yb_ref, st_ref,
               *, n_sub):
    @pl.when(pl.program_id(1) == 0)
    def _():
        st_ref[...] = jnp.zeros_like(st_ref)

    def operands(r, v, kk, b, kd, lw):
        return dict(r=r, v=v, kk=kk, b=b, kd=kd, lw=lw)

    _scan_step([(True, operands(rf, vf, kkf, bf, kdf, lwf), yf_ref),
                (False, operands(rb, vb, kkb, bb, kdb, lwb), yb_ref)], st_ref, n_sub)


def _scan(rkvl, prep, logw, *, dr):
    bsz, seq, _ = rkvl.shape
    rows = SCAN_SUB * CHUNK
    assert seq % rows == 0
    ns = seq // rows
    n_quad = dr // MXU_DIM

    def fwd(col):
        return pl.BlockSpec((None, rows, dr), lambda b, c: (b, c, col))

    def bwd(col):
        return pl.BlockSpec((None, rows, dr), lambda b, c: (b, ns - 1 - c, col))

    return pl.pallas_call(
        functools.partial(_scan_body, n_sub=SCAN_SUB),
        grid=(bsz, ns),
        in_specs=[fwd(0), fwd(2), fwd(P_KK), fwd(P_BF), fwd(P_KDF), fwd(L_F),
                  bwd(0), bwd(2), bwd(P_KK), bwd(P_BB), bwd(P_KDB), bwd(L_B)],
        out_specs=[pl.BlockSpec((None, rows, dr), lambda b, c: (b, c, 0)),
                   pl.BlockSpec((None, rows, dr), lambda b, c: (b, ns - 1 - c, 0))],
        out_shape=[jax.ShapeDtypeStruct((bsz, seq, dr), BF16)] * 2,
        scratch_shapes=[pltpu.VMEM((2 * n_quad, MXU_DIM, MXU_DIM), F32)],
        compiler_params=pltpu.CompilerParams(
            dimension_semantics=("arbitrary", "arbitrary"), vmem_limit_bytes=VMEM_LIMIT),
        name="scan",
    )(rkvl, rkvl, prep, prep, prep, logw, rkvl, rkvl, prep, prep, prep, logw)


def _post_body(yf_ref, yb_ref, g_ref, bonus_ref, oconv_ref, x_ref, wout_ref, gn_ref, ones_ref,
               out_ref, *, dr):
    ones_bd = ones_ref[...]
    y = yf_ref[...].astype(F32) + yb_ref[...].astype(F32)
    mean = _head_sum(y, ones_bd) * (1.0 / HEAD)
    d = y - mean
    var = _head_sum(d * d, ones_bd) * (1.0 / HEAD)
    yn = (d * lax.rsqrt(var + GN_EPS) * gn_ref[0:1, :] + gn_ref[1:2, :]
          + bonus_ref[...].astype(F32))
    o = (yn * g_ref[...].astype(F32)).astype(BF16)
    out_ref[...] = (x_ref[...]
                    + jnp.dot(o, wout_ref[0:dr, :], preferred_element_type=F32)
                    + jnp.dot(oconv_ref[...], wout_ref[dr:, :], preferred_element_type=F32))


def _post(yf, yb, prep, oconv, x, wout, gn, ones_bd, *, dr, row_tile):
    bsz, seq, d = x.shape

    def rows(width, col=0):
        return pl.BlockSpec((None, row_tile, width), lambda b, i: (b, i, col))

    def full(a):
        return pl.BlockSpec(a.shape, lambda b, i: (0,) * a.ndim)

    return pl.pallas_call(
        functools.partial(_post_body, dr=dr),
        grid=(bsz, seq // row_tile),
        in_specs=[rows(dr), rows(dr), rows(dr, P_G), rows(dr, P_BONUS), rows(oconv.shape[-1]),
                  rows(d), full(wout), full(gn), full(ones_bd)],
        out_specs=rows(d),
        out_shape=jax.ShapeDtypeStruct((bsz, seq, d), F32),
        compiler_params=pltpu.CompilerParams(
            dimension_semantics=("arbitrary", "arbitrary"), vmem_limit_bytes=VMEM_LIMIT),
        name="post",
    )(yf, yb, prep, prep, oconv, x, wout, gn, ones_bd)


def _ffn_body(x_ref, n2_ref, wg_ref, wu_ref, wd_ref, nf_ref, out_ref, *, final_norm):
    x = x_ref[...]
    h = _rmsnorm(x, n2_ref[...]).astype(BF16)
    a = jnp.dot(h, wg_ref[...], preferred_element_type=F32)
    u = jnp.dot(h, wu_ref[...], preferred_element_type=F32)
    z = (a * jax.nn.sigmoid(a) * u).astype(BF16)
    x2 = x + jnp.dot(z, wd_ref[...], preferred_element_type=F32)
    out_ref[...] = _rmsnorm(x2, nf_ref[...]) if final_norm else x2


def _ffn(x, n2, wg, wu, wd, nf, *, final_norm, row_tile):
    n, d = x.shape

    def full(a):
        return pl.BlockSpec(a.shape, lambda i: (0,) * a.ndim, pipeline_mode=pl.Buffered(1))

    return pl.pallas_call(
        functools.partial(_ffn_body, final_norm=final_norm),
        grid=(n // row_tile,),
        in_specs=[pl.BlockSpec((row_tile, d), lambda i: (i, 0)),
                  full(n2), full(wg), full(wu), full(wd), full(nf)],
        out_specs=pl.BlockSpec((row_tile, d), lambda i: (i, 0)),
        out_shape=jax.ShapeDtypeStruct((n, d), F32),
        compiler_params=pltpu.CompilerParams(
            dimension_semantics=("arbitrary",), vmem_limit_bytes=VMEM_LIMIT),
        name="ffn",
    )(x, n2, wg, wu, wd, nf)


def _pad_cols(a, width):
    return jnp.pad(a, ((0, 0), (0, width - a.shape[1])))


def _pad_rows(a, height):
    return jnp.pad(a, ((0, height - a.shape[0]), (0, 0)))


def kernel(x, norm1_w, w_in, mu_shift, w_up_f, w0_f, w_up_b, w0_b, a_up_f, a0_f, a_up_b, a0_b,
           g_up, k_k, k_a_f, k_a_b, r_k_f, r_k_b, gn_w, gn_b, conv_w, w_out, norm2_w, w_gate,
           w_up, w_down, norm_f_w):
    bsz, seq, d = x.shape
    depth = w_in.shape[0]
    dr = w0_f.shape[-1]
    n_dec, n_aaa, n_gate = w_up_f.shape[1], a_up_f.shape[1], g_up.shape[1]
    d_conv = conv_w.shape[-1]
    assert dr % MXU_DIM == 0 and d_conv % COL_TILE == 0 and seq % CHUNK == 0
    assert n_dec + n_aaa <= 128 and n_gate <= 256
    o_xw = 3 * dr
    o_xa = o_xw + n_dec
    o_xg = o_xa + n_aaa
    o_conv = o_xg + n_gate

    head_id = jnp.arange(dr) // HEAD
    ones_bd = (head_id[:, None] == head_id[None, :]).astype(BF16)

    for l in range(depth):
        w = w_in[l]
        gb, gc, hh = (w[:, o_conv + i * d_conv:o_conv + (i + 1) * d_conv] for i in range(3))
        conv_cols = []
        for t in range(d_conv // COL_TILE):
            cs = slice(t * COL_TILE, (t + 1) * COL_TILE)
            conv_cols += [gb[:, cs], gc[:, cs], hh[:, cs]]
        w_shift = jnp.concatenate(
            [w[:, :o_xw], _pad_cols(w[:, o_xw:o_xg], COL_TILE), _pad_cols(w[:, o_xg:o_conv], COL_TILE)],
            axis=1).astype(BF16)
        w_conv = jnp.concatenate(conv_cols, axis=1).astype(BF16)
        mu = mu_shift[l][None, :]
        mu_p = jnp.concatenate(
            [mu[:, :o_xw], _pad_cols(mu[:, o_xw:o_xg], COL_TILE), _pad_cols(mu[:, o_xg:o_conv], COL_TILE)],
            axis=1)

        rkvl, oconv = _inproj(x, norm1_w[l][None, :], w_shift, mu_p, w_conv, conv_w[l])

        vecs = jnp.stack([w0_f[l], w0_b[l], a0_f[l], a0_b[l], k_k[l], k_a_f[l], k_a_b[l],
                          r_k_f[l].reshape(dr), r_k_b[l].reshape(dr)], axis=0)
        vecs = _pad_rows(vecs, 16)
        wup = _pad_rows(jnp.concatenate([w_up_f[l], w_up_b[l]], axis=1), 128).astype(BF16)
        aup = jnp.concatenate([a_up_f[l], a_up_b[l]], axis=1)
        aup = jnp.pad(aup, ((n_dec, 128 - n_dec - n_aaa), (0, 0))).astype(BF16)
        gup = _pad_rows(g_up[l], 256).astype(BF16)
        prep, logw = _prep(rkvl, vecs, wup, aup, gup, ones_bd, dr=dr, row_tile=512)

        yf, yb = _scan(rkvl, prep, logw, dr=dr)

        gn = jnp.stack([gn_w[l], gn_b[l]], axis=0)
        x1 = _post(yf, yb, prep, oconv, x, w_out[l].astype(BF16), gn, ones_bd, dr=dr, row_tile=512)

        x = _ffn(x1.reshape(bsz * seq, d), norm2_w[l][None, :], w_gate[l].astype(BF16),
                 w_up[l].astype(BF16), w_down[l].astype(BF16), norm_f_w[None, :],
                 final_norm=(l == depth - 1), row_tile=512).reshape(bsz, seq, d)
    return x
```

```python
import functools

import jax
import jax.numpy as jnp
from jax import lax
from jax.experimental import pallas as pl
from jax.experimental.pallas import tpu as pltpu

F32 = jnp.float32
BF16 = jnp.bfloat16

HEAD = 64
CHUNK = 64
MXU_DIM = 256
QUAD = MXU_DIM // CHUNK
SCAN_SUB = 4
COL_TILE = 256
ROW_CHUNK = 512
HALO = 8
LOG_DECAY_SCALE = 0.606531
RMS_EPS = 1e-6
GN_EPS = 64e-5
NORM_EPS = 1e-12
VMEM_LIMIT = 56 * 1024 * 1024

P_KK, P_BF, P_BB, P_KDF, P_KDB, P_G, P_BONUS, P_NUM = range(8)
L_F, L_B, L_NUM = range(3)
V_W0F, V_W0B, V_A0F, V_A0B, V_KK, V_KAF, V_KAB, V_RKF, V_RKB, V_NUM = range(10)


def _dot(a, b):
    return jnp.dot(a.astype(BF16), b.astype(BF16), preferred_element_type=F32)


def _dot_nt(a, b):
    return lax.dot_general(a.astype(BF16), b.astype(BF16), (((1,), (1,)), ((), ())),
                           preferred_element_type=F32)


def _split2(x):
    hi = x.astype(BF16)
    lo = (x - hi.astype(F32)).astype(BF16)
    return hi, lo


def _split3(x):
    hi = x.astype(BF16)
    r1 = x - hi.astype(F32)
    mid = r1.astype(BF16)
    lo = (r1 - mid.astype(F32)).astype(BF16)
    return hi, mid, lo


def _head_sum(x, ones_bd):
    hi, lo = _split2(x)
    return (jnp.dot(hi, ones_bd, preferred_element_type=F32)
            + jnp.dot(lo, ones_bd, preferred_element_type=F32))


def _rmsnorm(x, w):
    ms = jnp.mean(x * x, axis=-1, keepdims=True)
    return x * lax.rsqrt(ms + RMS_EPS) * w


def _zero_halo(s_ref, seq):
    z = jnp.zeros((HALO, s_ref.shape[1]), s_ref.dtype)
    s_ref[0:HALO, :] = z
    s_ref[HALO + seq:2 * HALO + seq, :] = z


def _windows(s_ref, c0, rows):
    width = s_ref.shape[1]
    win = s_ref[c0:c0 + rows + 2 * HALO, :].reshape(rows // HALO + 2, HALO, width)
    down = pltpu.roll(win, 1, 1)
    up = pltpu.roll(win, HALO - 1, 1)
    sub = lax.broadcasted_iota(jnp.int32, (1, HALO, width), 1)
    prv = jnp.where(sub == 0, down[:-2], down[1:-1])
    nxt = jnp.where(sub == HALO - 1, up[2:], up[1:-1])
    return tuple(a.reshape(rows, width) for a in (win[1:-1], prv, nxt))


def _shift_body(x_ref, n1_ref, w_ref, mu_ref, out_ref, h_ref, p_s, *, seq, row_blk):
    @pl.when(pl.program_id(1) == 0)
    def _():
        _zero_halo(p_s, seq)

        def body(i, c):
            rows = pl.ds(pl.multiple_of(i * row_blk, row_blk), row_blk)
            h_ref[rows, :] = _rmsnorm(x_ref[rows, :], n1_ref[...]).astype(BF16)
            return c
        lax.fori_loop(0, seq // row_blk, body, 0)

    w = w_ref[...]
    mu = mu_ref[...]

    def shift(c0):
        cur, prv, nxt = _windows(p_s, c0, ROW_CHUNK)
        out_ref[c0:c0 + ROW_CHUNK, :] = (cur + mu * (0.5 * (prv + nxt) - cur)).astype(out_ref.dtype)

    for c0 in range(0, seq, ROW_CHUNK):
        p_s[HALO + c0:HALO + c0 + ROW_CHUNK, :] = jnp.dot(
            h_ref[c0:c0 + ROW_CHUNK, :], w, preferred_element_type=F32)
        if c0:
            shift(c0 - ROW_CHUNK)
    shift(seq - ROW_CHUNK)


def _conv_body(h_ref, w_ref, cw_ref, out_ref, u_s, gb_s, *, seq):
    @pl.when(pl.program_id(1) == 0)
    def _():
        _zero_halo(u_s, seq)

    w = w_ref[...]
    cw = cw_ref[...]

    def conv(c0):
        cur, prv, nxt = _windows(u_s, c0, ROW_CHUNK)
        y = cw[0:1, :] * prv + cw[1:2, :] * cur + cw[2:3, :] * nxt
        out_ref[c0:c0 + ROW_CHUNK, :] = (gb_s[c0:c0 + ROW_CHUNK, :] * y).astype(out_ref.dtype)

    for c0 in range(0, seq, ROW_CHUNK):
        p = jnp.dot(h_ref[c0:c0 + ROW_CHUNK, :], w, preferred_element_type=F32)
        gb_s[c0:c0 + ROW_CHUNK, :] = p[:, :COL_TILE]
        u_s[HALO + c0:HALO + c0 + ROW_CHUNK, :] = p[:, COL_TILE:2 * COL_TILE] * p[:, 2 * COL_TILE:]
        if c0:
            conv(c0 - ROW_CHUNK)
    conv(seq - ROW_CHUNK)


def _inproj(x, n1, w_shift, mu_p, w_conv, conv_w):
    bsz, seq, d = x.shape
    assert seq % ROW_CHUNK == 0
    params = pltpu.CompilerParams(
        dimension_semantics=("arbitrary", "arbitrary"), vmem_limit_bytes=VMEM_LIMIT)
    rkvl, h = pl.pallas_call(
        functools.partial(_shift_body, seq=seq, row_blk=256),
        grid=(bsz, w_shift.shape[1] // COL_TILE),
        in_specs=[
            pl.BlockSpec((None, seq, d), lambda b, j: (b, 0, 0)),
            pl.BlockSpec((1, d), lambda b, j: (0, 0)),
            pl.BlockSpec((d, COL_TILE), lambda b, j: (0, j)),
            pl.BlockSpec((1, COL_TILE), lambda b, j: (0, j)),
        ],
        out_specs=[
            pl.BlockSpec((None, seq, COL_TILE), lambda b, j: (b, 0, j)),
            pl.BlockSpec((None, seq, d), lambda b, j: (b, 0, 0)),
        ],
        out_shape=[
            jax.ShapeDtypeStruct((bsz, seq, w_shift.shape[1]), BF16),
            jax.ShapeDtypeStruct((bsz, seq, d), BF16),
        ],
        scratch_shapes=[pltpu.VMEM((seq + 2 * HALO, COL_TILE), F32)],
        compiler_params=params,
        name="inproj_shift",
    )(x, n1, w_shift, mu_p)
    d_conv = conv_w.shape[-1]
    oconv = pl.pallas_call(
        functools.partial(_conv_body, seq=seq),
        grid=(bsz, d_conv // COL_TILE),
        in_specs=[
            pl.BlockSpec((None, seq, d), lambda b, q: (b, 0, 0)),
            pl.BlockSpec((d, 3 * COL_TILE), lambda b, q: (0, q)),
            pl.BlockSpec((3, COL_TILE), lambda b, q: (0, q)),
        ],
        out_specs=pl.BlockSpec((None, seq, COL_TILE), lambda b, q: (b, 0, q)),
        out_shape=jax.ShapeDtypeStruct((bsz, seq, d_conv), BF16),
        scratch_shapes=[pltpu.VMEM((seq + 2 * HALO, COL_TILE), F32),
                        pltpu.VMEM((seq, COL_TILE), F32)],
        compiler_params=params,
        name="inproj_conv",
    )(h, w_conv, conv_w)
    return rkvl, oconv


def _prep_body(r_ref, k_ref, v_ref, l_ref, vec_ref, wup_ref, aup_ref, gup_ref, ones_ref,
               out_ref, lw_ref, *, dr):
    r = r_ref[...].astype(F32)
    k = k_ref[...].astype(F32)
    v = v_ref[...].astype(F32)
    vec = vec_ref[...]
    ones_bd = ones_ref[...]

    def vrow(i):
        return vec[i:i + 1, :]

    t128 = l_ref[:, 0:128]
    lw = jnp.dot(jnp.tanh(t128.astype(F32)).astype(BF16), wup_ref[...],
                 preferred_element_type=F32)
    aa = jnp.dot(t128, aup_ref[...], preferred_element_type=F32)
    lw_f = -LOG_DECAY_SCALE * jax.nn.sigmoid(vrow(V_W0F) + lw[:, :dr])
    lw_b = -LOG_DECAY_SCALE * jax.nn.sigmoid(vrow(V_W0B) + lw[:, dr:])
    a_f = jax.nn.sigmoid(vrow(V_A0F) + aa[:, :dr])
    a_b = jax.nn.sigmoid(vrow(V_A0B) + aa[:, dr:])
    g = jnp.dot(jax.nn.sigmoid(l_ref[:, 256:512].astype(F32)).astype(BF16), gup_ref[...],
                preferred_element_type=F32)

    kkr = k * vrow(V_KK)
    ssq = _head_sum(kkr * kkr, ones_bd)
    kk = kkr / jnp.maximum(jnp.sqrt(ssq), NORM_EPS)
    kd_f = k * (1.0 + (a_f - 1.0) * vrow(V_KAF))
    kd_b = k * (1.0 + (a_b - 1.0) * vrow(V_KAB))
    bonus = _head_sum(r * kd_f * vrow(V_RKF) + r * kd_b * vrow(V_RKB), ones_bd) * v

    def put(i, val):
        out_ref[:, i * dr:(i + 1) * dr] = val.astype(out_ref.dtype)

    put(P_KK, kk)
    put(P_BF, a_f * kk)
    put(P_BB, a_b * kk)
    put(P_KDF, kd_f)
    put(P_KDB, kd_b)
    put(P_G, g)
    put(P_BONUS, bonus)
    lw_ref[:, L_F * dr:(L_F + 1) * dr] = lw_f
    lw_ref[:, L_B * dr:(L_B + 1) * dr] = lw_b


def _prep(rkvl, vecs, wup, aup, gup, ones_bd, *, dr, row_tile):
    bsz, seq, _ = rkvl.shape

    def col(c):
        return pl.BlockSpec((None, row_tile, dr), lambda b, i: (b, i, c))

    def full(a):
        return pl.BlockSpec(a.shape, lambda b, i: (0,) * a.ndim)

    return pl.pallas_call(
        functools.partial(_prep_body, dr=dr),
        grid=(bsz, seq // row_tile),
        in_specs=[col(0), col(1), col(2), col(3),
                  full(vecs), full(wup), full(aup), full(gup), full(ones_bd)],
        out_specs=[pl.BlockSpec((None, row_tile, P_NUM * dr), lambda b, i: (b, i, 0)),
                   pl.BlockSpec((None, row_tile, L_NUM * dr), lambda b, i: (b, i, 0))],
        out_shape=[jax.ShapeDtypeStruct((bsz, seq, P_NUM * dr), BF16),
                   jax.ShapeDtypeStruct((bsz, seq, L_NUM * dr), F32)],
        compiler_params=pltpu.CompilerParams(
            dimension_semantics=("arbitrary", "arbitrary"), vmem_limit_bytes=VMEM_LIMIT),
        name="prep",
    )(rkvl, rkvl, rkvl, rkvl, vecs, wup, aup, gup, ones_bd)


def _block_diag(x, lane_head):
    return jnp.concatenate([jnp.where(lane_head == h, x, 0.0) for h in range(QUAD)],
                           axis=0).astype(BF16)


def _decay_factors(lw, *, forward):
    c = CHUNK
    row = lax.broadcasted_iota(jnp.int32, lw.shape, 0)
    cs = lw
    step = 1
    while step < c:
        if forward:
            cs = cs + jnp.where(row >= step, pltpu.roll(cs, step, 0), 0.0)
        else:
            cs = cs + jnp.where(row < c - step, pltpu.roll(cs, c - step, 0), 0.0)
        step *= 2
    cs_end = cs[c - 1:c, :] if forward else cs[0:1, :]
    return dict(e_t=jnp.exp(cs), e_prev=jnp.exp(cs - lw), e_inv=jnp.exp(-cs),
                e_end=jnp.exp(cs_end - cs), w_c=jnp.exp(cs_end))


def _scan_step(dirs, st_ref, n_sub):
    c = CHUNK
    t_idx = lax.broadcasted_iota(jnp.int32, (c, MXU_DIM), 0)
    lane = lax.broadcasted_iota(jnp.int32, (c, MXU_DIM), 1)
    s_idx = lane & (c - 1)
    lane_head = lane >> 6
    eye = (s_idx == t_idx).astype(F32)
    bd_mask = ((lax.broadcasted_iota(jnp.int32, (MXU_DIM, MXU_DIM), 0) >> 6)
               == (lax.broadcasted_iota(jnp.int32, (MXU_DIM, MXU_DIM), 1) >> 6))

    def bd(x):
        return _block_diag(x, lane_head)

    subs = []
    for k in range(n_sub):
        chains = []
        for i, (forward, refs, y_ref) in enumerate(dirs):
            row0 = (k if forward else n_sub - 1 - k) * c
            rows = slice(row0, row0 + c)
            a = {name: ref[rows, :].astype(F32) for name, ref in refs.items()}
            f = _decay_factors(a["lw"], forward=forward)
            if forward:
                strict, incl = s_idx < t_idx, s_idx <= t_idx
            else:
                strict, incl = s_idx > t_idx, s_idx >= t_idx
            kap, rt = a["kk"] * f["e_prev"], a["r"] * f["e_t"]
            kt, bt = a["kd"] * f["e_inv"], a["b"] * f["e_inv"]
            kh, bh = a["kd"] * f["e_end"], a["b"] * f["e_end"]
            n_quad = a["r"].shape[-1] // MXU_DIM
            for q in range(n_quad):
                sl = slice(q * MXU_DIM, (q + 1) * MXU_DIM)
                chains.append(dict(
                    strict=strict, incl=incl, y_ref=y_ref, rows=rows, sl=sl, idx=i * n_quad + q,
                    kr=jnp.concatenate([kap[:, sl], rt[:, sl]], axis=0).astype(BF16),
                    kt=kt[:, sl], bt=bt[:, sl], v=a["v"][:, sl],
                    w_col=jnp.broadcast_to(f["w_c"][:, sl], (MXU_DIM // 2, MXU_DIM)).T,
                    lhs_t=jnp.concatenate([kh[:, sl], bh[:, sl]], axis=0).T.astype(BF16)))
        subs.append(chains)
    every = [ch for chains in subs for ch in chains]

    for ch in every:
        g_b = _dot_nt(ch["kr"], bd(ch["bt"]))
        g_k = _dot_nt(ch["kr"], bd(ch["kt"]))
        ch["arb"] = jnp.where(ch["incl"], g_b[c:], 0.0)
        ch["aa"] = jnp.concatenate([jnp.where(ch["strict"], g_k[:c], 0.0),
                                    jnp.where(ch["incl"], g_k[c:], 0.0)], axis=0)
        ch["x"] = -jnp.where(ch["strict"], g_b[:c], 0.0)
        ch["t"] = eye + ch["x"]
    for ch in every:
        ch["av"] = _dot(ch["aa"], bd(ch["v"]))

    n_factors = CHUNK.bit_length() - 1
    for ch in every:
        ch["x"] = _dot(ch["x"], bd(ch["x"]))
    for _ in range(n_factors - 2):
        for ch in every:
            rr = _dot(jnp.concatenate([ch["x"], ch["t"]], axis=0), bd(ch["x"]))
            ch["x"] = rr[:c]
            ch["t"] = ch["t"] + rr[c:]
    for ch in every:
        ch["t"] = ch["t"] + _dot(ch["t"], bd(ch["x"]))

    for chains in subs:
        for ch in chains:
            ch["m"] = st_ref[ch["idx"]]
            ch["krm"] = _dot(ch["kr"], ch["m"])
        for ch in chains:
            ch["sa"] = _dot(ch["t"], bd(ch["krm"][:c] + ch["av"][:c]))
        for ch in chains:
            sa = ch["sa"]
            y = ch["krm"][c:] + ch["av"][c:] - _dot(ch["arb"], bd(sa))
            ch["y_ref"][ch["rows"], ch["sl"]] = y.astype(ch["y_ref"].dtype)
            upd = _dot(ch["lhs_t"], jnp.concatenate([ch["v"], -sa], axis=0))
            decay = jnp.concatenate([ch["w_col"], ch["w_col"]], axis=1)
            st_ref[ch["idx"]] = decay * ch["m"] + jnp.where(bd_mask, upd, 0.0)


def _scan_body(rf, vf, kkf, bf, kdf, lwf, rb, vb, kkb, bb, kdb, lwb, yf_ref, yb_ref, st_ref,
               *, n_sub):
    @pl.when(pl.program_id(1) == 0)
    def _():
        st_ref[...] = jnp.zeros_like(st_ref)

    def operands(r, v, kk, b, kd, lw):
        return dict(r=r, v=v, kk=kk, b=b, kd=kd, lw=lw)

    _scan_step([(True, operands(rf, vf, kkf, bf, kdf, lwf), yf_ref),
                (False, operands(rb, vb, kkb, bb, kdb, lwb), yb_ref)], st_ref, n_sub)


def _scan(rkvl, prep, logw, *, dr):
    bsz, seq, _ = rkvl.shape
    rows = SCAN_SUB * CHUNK
    assert seq % rows == 0
    ns = seq // rows
    n_quad = dr // MXU_DIM

    def fwd(col):
        return pl.BlockSpec((None, rows, dr), lambda b, c: (b, c, col))

    def bwd(col):
        return pl.BlockSpec((None, rows, dr), lambda b, c: (b, ns - 1 - c, col))

    return pl.pallas_call(
        functools.partial(_scan_body, n_sub=SCAN_SUB),
        grid=(bsz, ns),
        in_specs=[fwd(0), fwd(2), fwd(P_KK), fwd(P_BF), fwd(P_KDF), fwd(L_F),
                  bwd(0), bwd(2), bwd(P_KK), bwd(P_BB), bwd(P_KDB), bwd(L_B)],
        out_specs=[pl.BlockSpec((None, rows, dr), lambda b, c: (b, c, 0)),
                   pl.BlockSpec((None, rows, dr), lambda b, c: (b, ns - 1 - c, 0))],
        out_shape=[jax.ShapeDtypeStruct((bsz, seq, dr), BF16)] * 2,
        scratch_shapes=[pltpu.VMEM((2 * n_quad, MXU_DIM, MXU_DIM), F32)],
        compiler_params=pltpu.CompilerParams(
            dimension_semantics=("arbitrary", "arbitrary"), vmem_limit_bytes=VMEM_LIMIT),
        name="scan",
    )(rkvl, rkvl, prep, prep, prep, logw, rkvl, rkvl, prep, prep, prep, logw)


def _post_body(yf_ref, yb_ref, g_ref, bonus_ref, oconv_ref, x_ref, wout_ref, gn_ref, ones_ref,
               out_ref, *, dr):
    ones_bd = ones_ref[...]
    y = yf_ref[...].astype(F32) + yb_ref[...].astype(F32)
    mean = _head_sum(y, ones_bd) * (1.0 / HEAD)
    d = y - mean
    var = _head_sum(d * d, ones_bd) * (1.0 / HEAD)
    yn = (d * lax.rsqrt(var + GN_EPS) * gn_ref[0:1, :] + gn_ref[1:2, :]
          + bonus_ref[...].astype(F32))
    o = (yn * g_ref[...].astype(F32)).astype(BF16)
    out_ref[...] = (x_ref[...]
                    + jnp.dot(o, wout_ref[0:dr, :], preferred_element_type=F32)
                    + jnp.dot(oconv_ref[...], wout_ref[dr:, :], preferred_element_type=F32))


def _post(yf, yb, prep, oconv, x, wout, gn, ones_bd, *, dr, row_tile):
    bsz, seq, d = x.shape

    def rows(width, col=0):
        return pl.BlockSpec((None, row_tile, width), lambda b, i: (b, i, col))

    def full(a):
        return pl.BlockSpec(a.shape, lambda b, i: (0,) * a.ndim)

    return pl.pallas_call(
        functools.partial(_post_body, dr=dr),
        grid=(bsz, seq // row_tile),
        in_specs=[rows(dr), rows(dr), rows(dr, P_G), rows(dr, P_BONUS), rows(oconv.shape[-1]),
                  rows(d), full(wout), full(gn), full(ones_bd)],
        out_specs=rows(d),
        out_shape=jax.ShapeDtypeStruct((bsz, seq, d), F32),
        compiler_params=pltpu.CompilerParams(
            dimension_semantics=("arbitrary", "arbitrary"), vmem_limit_bytes=VMEM_LIMIT),
        name="post",
    )(yf, yb, prep, prep, oconv, x, wout, gn, ones_bd)


def _ffn_body(x_ref, n2_ref, wg_ref, wu_ref, wd_ref, nf_ref, out_ref, *, final_norm):
    x = x_ref[...]
    h = _rmsnorm(x, n2_ref[...]).astype(BF16)
    a = jnp.dot(h, wg_ref[...], preferred_element_type=F32)
    u = jnp.dot(h, wu_ref[...], preferred_element_type=F32)
    z = (a * jax.nn.sigmoid(a) * u).astype(BF16)
    x2 = x + jnp.dot(z, wd_ref[...], preferred_element_type=F32)
    out_ref[...] = _rmsnorm(x2, nf_ref[...]) if final_norm else x2


def _ffn(x, n2, wg, wu, wd, nf, *, final_norm, row_tile):
    n, d = x.shape

    def full(a):
        return pl.BlockSpec(a.shape, lambda i: (0,) * a.ndim, pipeline_mode=pl.Buffered(1))

    return pl.pallas_call(
        functools.partial(_ffn_body, final_norm=final_norm),
        grid=(n // row_tile,),
        in_specs=[pl.BlockSpec((row_tile, d), lambda i: (i, 0)),
                  full(n2), full(wg), full(wu), full(wd), full(nf)],
        out_specs=pl.BlockSpec((row_tile, d), lambda i: (i, 0)),
        out_shape=jax.ShapeDtypeStruct((n, d), F32),
        compiler_params=pltpu.CompilerParams(
            dimension_semantics=("arbitrary",), vmem_limit_bytes=VMEM_LIMIT),
        name="ffn",
    )(x, n2, wg, wu, wd, nf)


def _pad_cols(a, width):
    return jnp.pad(a, ((0, 0), (0, width - a.shape[1])))


def _pad_rows(a, height):
    return jnp.pad(a, ((0, height - a.shape[0]), (0, 0)))


def kernel(x, norm1_w, w_in, mu_shift, w_up_f, w0_f, w_up_b, w0_b, a_up_f, a0_f, a_up_b, a0_b,
           g_up, k_k, k_a_f, k_a_b, r_k_f, r_k_b, gn_w, gn_b, conv_w, w_out, norm2_w, w_gate,
           w_up, w_down, norm_f_w):
    bsz, seq, d = x.shape
    depth = w_in.shape[0]
    dr = w0_f.shape[-1]
    n_dec, n_aaa, n_gate = w_up_f.shape[1], a_up_f.shape[1], g_up.shape[1]
    d_conv = conv_w.shape[-1]
    assert dr % MXU_DIM == 0 and d_conv % COL_TILE == 0 and seq % CHUNK == 0
    assert n_dec + n_aaa <= 128 and n_gate <= 256
    o_xw = 3 * dr
    o_xa = o_xw + n_dec
    o_xg = o_xa + n_aaa
    o_conv = o_xg + n_gate

    head_id = jnp.arange(dr) // HEAD
    ones_bd = (head_id[:, None] == head_id[None, :]).astype(BF16)

    for l in range(depth):
        w = w_in[l]
        gb, gc, hh = (w[:, o_conv + i * d_conv:o_conv + (i + 1) * d_conv] for i in range(3))
        conv_cols = []
        for t in range(d_conv // COL_TILE):
            cs = slice(t * COL_TILE, (t + 1) * COL_TILE)
            conv_cols += [gb[:, cs], gc[:, cs], hh[:, cs]]
        w_shift = jnp.concatenate(
            [w[:, :o_xw], _pad_cols(w[:, o_xw:o_xg], COL_TILE), _pad_cols(w[:, o_xg:o_conv], COL_TILE)],
            axis=1).astype(BF16)
        w_conv = jnp.concatenate(conv_cols, axis=1).astype(BF16)
        mu = mu_shift[l][None, :]
        mu_p = jnp.concatenate(
            [mu[:, :o_xw], _pad_cols(mu[:, o_xw:o_xg], COL_TILE), _pad_cols(mu[:, o_xg:o_conv], COL_TILE)],
            axis=1)

        rkvl, oconv = _inproj(x, norm1_w[l][None, :], w_shift, mu_p, w_conv, conv_w[l])

        vecs = jnp.stack([w0_f[l], w0_b[l], a0_f[l], a0_b[l], k_k[l], k_a_f[l], k_a_b[l],
                          r_k_f[l].reshape(dr), r_k_b[l].reshape(dr)], axis=0)
        vecs = _pad_rows(vecs, 16)
        wup = _pad_rows(jnp.concatenate([w_up_f[l], w_up_b[l]], axis=1), 128).astype(BF16)
        aup = jnp.concatenate([a_up_f[l], a_up_b[l]], axis=1)
        aup = jnp.pad(aup, ((n_dec, 128 - n_dec - n_aaa), (0, 0))).astype(BF16)
        gup = _pad_rows(g_up[l], 256).astype(BF16)
        prep, logw = _prep(rkvl, vecs, wup, aup, gup, ones_bd, dr=dr, row_tile=512)

        yf, yb = _scan(rkvl, prep, logw, dr=dr)

        gn = jnp.stack([gn_w[l], gn_b[l]], axis=0)
        x1 = _post(yf, yb, prep, oconv, x, w_out[l].astype(BF16), gn, ones_bd, dr=dr, row_tile=512)

        x = _ffn(x1.reshape(bsz * seq, d), norm2_w[l][None, :], w_gate[l].astype(BF16),
                 w_up[l].astype(BF16), w_down[l].astype(BF16), norm_f_w[None, :],
                 final_norm=(l == depth - 1), row_tile=512).reshape(bsz, seq, d)
    return x
```

```python
import functools

import jax
import jax.numpy as jnp
from jax import lax
from jax.experimental import pallas as pl
from jax.experimental.pallas import tpu as pltpu

F32 = jnp.float32
BF16 = jnp.bfloat16

HEAD = 64
CHUNK = 64
MXU_DIM = 256
QUAD = MXU_DIM // CHUNK
SCAN_SUB = 4
A_SKEW = 2
COL_TILE = 256
ROW_CHUNK = 512
HALO = 8
LOG_DECAY_SCALE = 0.606531
RMS_EPS = 1e-6
GN_EPS = 64e-5
NORM_EPS = 1e-12
VMEM_LIMIT = 56 * 1024 * 1024

P_KK, P_BF, P_BB, P_KDF, P_KDB, P_G, P_BONUS, P_NUM = range(8)
L_F, L_B, L_NUM = range(3)
V_W0F, V_W0B, V_A0F, V_A0B, V_KK, V_KAF, V_KAB, V_RKF, V_RKB, V_NUM = range(10)


def _dot(a, b):
    return jnp.dot(a.astype(BF16), b.astype(BF16), preferred_element_type=F32)


def _dot_nt(a, b):
    return lax.dot_general(a.astype(BF16), b.astype(BF16), (((1,), (1,)), ((), ())),
                           preferred_element_type=F32)


def _split2(x):
    hi = x.astype(BF16)
    lo = (x - hi.astype(F32)).astype(BF16)
    return hi, lo


def _split3(x):
    hi = x.astype(BF16)
    r1 = x - hi.astype(F32)
    mid = r1.astype(BF16)
    lo = (r1 - mid.astype(F32)).astype(BF16)
    return hi, mid, lo


def _head_sum(x, ones_bd):
    hi, lo = _split2(x)
    return (jnp.dot(hi, ones_bd, preferred_element_type=F32)
            + jnp.dot(lo, ones_bd, preferred_element_type=F32))


def _rmsnorm(x, w):
    ms = jnp.mean(x * x, axis=-1, keepdims=True)
    return x * lax.rsqrt(ms + RMS_EPS) * w


def _zero_halo(s_ref, seq):
    z = jnp.zeros((HALO, s_ref.shape[1]), s_ref.dtype)
    s_ref[0:HALO, :] = z
    s_ref[HALO + seq:2 * HALO + seq, :] = z


def _windows(s_ref, c0, rows):
    width = s_ref.shape[1]
    win = s_ref[c0:c0 + rows + 2 * HALO, :].reshape(rows // HALO + 2, HALO, width)
    down = pltpu.roll(win, 1, 1)
    up = pltpu.roll(win, HALO - 1, 1)
    sub = lax.broadcasted_iota(jnp.int32, (1, HALO, width), 1)
    prv = jnp.where(sub == 0, down[:-2], down[1:-1])
    nxt = jnp.where(sub == HALO - 1, up[2:], up[1:-1])
    return tuple(a.reshape(rows, width) for a in (win[1:-1], prv, nxt))


def _shift_body(x_ref, n1_ref, w_ref, mu_ref, out_ref, hout_ref, h_s, p_s, *, seq):
    bb = pl.program_id(0)
    j = pl.program_id(1)
    new = lax.rem(bb, 2)
    old = 1 - new

    @pl.when((bb == 0) & (j == 0))
    def _():
        _zero_halo(p_s, seq)
        h_s[1] = jnp.zeros(h_s.shape[1:], h_s.dtype)

    n_rows = x_ref.shape[0]
    h = _rmsnorm(x_ref[...], n1_ref[...]).astype(BF16)
    h_s[new, pl.ds(pl.multiple_of(j * n_rows, n_rows), n_rows), :] = h
    hout_ref[...] = h

    w = w_ref[...]
    mu = mu_ref[...]

    def shift(c0):
        cur, prv, nxt = _windows(p_s, c0, ROW_CHUNK)
        out_ref[c0:c0 + ROW_CHUNK, :] = (cur + mu * (0.5 * (prv + nxt) - cur)).astype(out_ref.dtype)

    for c0 in range(0, seq, ROW_CHUNK):
        p_s[HALO + c0:HALO + c0 + ROW_CHUNK, :] = jnp.dot(
            h_s[old, c0:c0 + ROW_CHUNK, :], w, preferred_element_type=F32)
        if c0:
            shift(c0 - ROW_CHUNK)
    shift(seq - ROW_CHUNK)


def _conv_body(h_ref, w_ref, cw_ref, out_ref, u_s, gb_s, *, seq):
    @pl.when(pl.program_id(1) == 0)
    def _():
        _zero_halo(u_s, seq)

    w = w_ref[...]
    cw = cw_ref[...]

    def conv(c0):
        cur, prv, nxt = _windows(u_s, c0, ROW_CHUNK)
        y = cw[0:1, :] * prv + cw[1:2, :] * cur + cw[2:3, :] * nxt
        out_ref[c0:c0 + ROW_CHUNK, :] = (gb_s[c0:c0 + ROW_CHUNK, :] * y).astype(out_ref.dtype)

    for c0 in range(0, seq, ROW_CHUNK):
        p = jnp.dot(h_ref[c0:c0 + ROW_CHUNK, :], w, preferred_element_type=F32)
        gb_s[c0:c0 + ROW_CHUNK, :] = p[:, :COL_TILE]
        u_s[HALO + c0:HALO + c0 + ROW_CHUNK, :] = p[:, COL_TILE:2 * COL_TILE] * p[:, 2 * COL_TILE:]
        if c0:
            conv(c0 - ROW_CHUNK)
    conv(seq - ROW_CHUNK)


def _inproj(x, n1, w_shift, mu_p, w_conv, conv_w):
    bsz, seq, d = x.shape
    assert seq % ROW_CHUNK == 0
    params = pltpu.CompilerParams(
        dimension_semantics=("arbitrary", "arbitrary"), vmem_limit_bytes=VMEM_LIMIT)
    n_tiles = w_shift.shape[1] // COL_TILE
    assert seq % n_tiles == 0
    norm_rows = seq // n_tiles

    def nxt(b):
        return jnp.minimum(b, bsz - 1)

    rkvl, h = pl.pallas_call(
        functools.partial(_shift_body, seq=seq),
        grid=(bsz + 1, n_tiles),
        in_specs=[
            pl.BlockSpec((None, norm_rows, d), lambda b, j: (nxt(b), j, 0)),
            pl.BlockSpec((1, d), lambda b, j: (0, 0)),
            pl.BlockSpec((d, COL_TILE), lambda b, j: (0, j)),
            pl.BlockSpec((1, COL_TILE), lambda b, j: (0, j)),
        ],
        out_specs=[
            pl.BlockSpec((None, seq, COL_TILE), lambda b, j: (b, 0, j)),
            pl.BlockSpec((None, norm_rows, d), lambda b, j: (b, j, 0)),
        ],
        out_shape=[
            jax.ShapeDtypeStruct((bsz + 1, seq, w_shift.shape[1]), BF16),
            jax.ShapeDtypeStruct((bsz + 1, seq, d), BF16),
        ],
        scratch_shapes=[pltpu.VMEM((2, seq, d), BF16),
                        pltpu.VMEM((seq + 2 * HALO, COL_TILE), F32)],
        compiler_params=params,
        name="inproj_shift",
    )(x, n1, w_shift, mu_p)
    d_conv = conv_w.shape[-1]
    oconv = pl.pallas_call(
        functools.partial(_conv_body, seq=seq),
        grid=(bsz, d_conv // COL_TILE),
        in_specs=[
            pl.BlockSpec((None, seq, d), lambda b, q: (b, 0, 0)),
            pl.BlockSpec((d, 3 * COL_TILE), lambda b, q: (0, q)),
            pl.BlockSpec((3, COL_TILE), lambda b, q: (0, q)),
        ],
        out_specs=pl.BlockSpec((None, seq, COL_TILE), lambda b, q: (b, 0, q)),
        out_shape=jax.ShapeDtypeStruct((bsz, seq, d_conv), BF16),
        scratch_shapes=[pltpu.VMEM((seq + 2 * HALO, COL_TILE), F32),
                        pltpu.VMEM((seq, COL_TILE), F32)],
        compiler_params=params,
        name="inproj_conv",
    )(h, w_conv, conv_w)
    return rkvl, oconv


def _prep_body(r_ref, k_ref, v_ref, l_ref, vec_ref, wup_ref, aup_ref, gup_ref, ones_ref,
               out_ref, lw_ref, *, dr):
    r = r_ref[...].astype(F32)
    k = k_ref[...].astype(F32)
    v = v_ref[...].astype(F32)
    vec = vec_ref[...]
    ones_bd = ones_ref[...]

    def vrow(i):
        return vec[i:i + 1, :]

    t128 = l_ref[:, 0:128]
    lw = jnp.dot(jnp.tanh(t128.astype(F32)).astype(BF16), wup_ref[...],
                 preferred_element_type=F32)
    aa = jnp.dot(t128, aup_ref[...], preferred_element_type=F32)
    lw_f = -LOG_DECAY_SCALE * jax.nn.sigmoid(vrow(V_W0F) + lw[:, :dr])
    lw_b = -LOG_DECAY_SCALE * jax.nn.sigmoid(vrow(V_W0B) + lw[:, dr:])
    a_f = jax.nn.sigmoid(vrow(V_A0F) + aa[:, :dr])
    a_b = jax.nn.sigmoid(vrow(V_A0B) + aa[:, dr:])
    g = jnp.dot(jax.nn.sigmoid(l_ref[:, 256:512].astype(F32)).astype(BF16), gup_ref[...],
                preferred_element_type=F32)

    kkr = k * vrow(V_KK)
    ssq = _head_sum(kkr * kkr, ones_bd)
    kk = kkr / jnp.maximum(jnp.sqrt(ssq), NORM_EPS)
    kd_f = k * (1.0 + (a_f - 1.0) * vrow(V_KAF))
    kd_b = k * (1.0 + (a_b - 1.0) * vrow(V_KAB))
    bonus = _head_sum(r * kd_f * vrow(V_RKF) + r * kd_b * vrow(V_RKB), ones_bd) * v

    def put(i, val):
        out_ref[:, i * dr:(i + 1) * dr] = val.astype(out_ref.dtype)

    put(P_KK, kk)
    put(P_BF, a_f * kk)
    put(P_BB, a_b * kk)
    put(P_KDF, kd_f)
    put(P_KDB, kd_b)
    put(P_G, g)
    put(P_BONUS, bonus)
    lw_ref[:, L_F * dr:(L_F + 1) * dr] = lw_f
    lw_ref[:, L_B * dr:(L_B + 1) * dr] = lw_b


def _prep(rkvl, vecs, wup, aup, gup, ones_bd, *, dr, row_tile):
    bsz, seq = rkvl.shape[0] - 1, rkvl.shape[1]

    def col(c):
        return pl.BlockSpec((None, row_tile, dr), lambda b, i: (b + 1, i, c))

    def full(a):
        return pl.BlockSpec(a.shape, lambda b, i: (0,) * a.ndim)

    return pl.pallas_call(
        functools.partial(_prep_body, dr=dr),
        grid=(bsz, seq // row_tile),
        in_specs=[col(0), col(1), col(2), col(3),
                  full(vecs), full(wup), full(aup), full(gup), full(ones_bd)],
        out_specs=[pl.BlockSpec((None, row_tile, P_NUM * dr), lambda b, i: (b, i, 0)),
                   pl.BlockSpec((None, row_tile, L_NUM * dr), lambda b, i: (b, i, 0))],
        out_shape=[jax.ShapeDtypeStruct((bsz, seq, P_NUM * dr), BF16),
                   jax.ShapeDtypeStruct((bsz, seq, L_NUM * dr), F32)],
        compiler_params=pltpu.CompilerParams(
            dimension_semantics=("arbitrary", "arbitrary"), vmem_limit_bytes=VMEM_LIMIT),
        name="prep",
    )(rkvl, rkvl, rkvl, rkvl, vecs, wup, aup, gup, ones_bd)


def _block_diag(x, lane_head):
    return jnp.concatenate([jnp.where(lane_head == h, x, 0.0) for h in range(QUAD)],
                           axis=0).astype(BF16)


def _decay_factors(lw, *, forward):
    c = CHUNK
    row = lax.broadcasted_iota(jnp.int32, lw.shape, 0)
    cs = lw
    step = 1
    while step < c:
        if forward:
            cs = cs + jnp.where(row >= step, pltpu.roll(cs, step, 0), 0.0)
        else:
            cs = cs + jnp.where(row < c - step, pltpu.roll(cs, c - step, 0), 0.0)
        step *= 2
    cs_end = cs[c - 1:c, :] if forward else cs[0:1, :]
    return dict(e_t=jnp.exp(cs), e_prev=jnp.exp(cs - lw), e_inv=jnp.exp(-cs),
                e_end=jnp.exp(cs_end - cs), w_c=jnp.exp(cs_end))


def _scan_body(rf, vf, kkf, bf, kdf, lwf, rb, vb, kkb, bb, kdb, lwb, yf_ref, yb_ref,
               st_ref, t_s, av_s, arb_s, kr_s, lhs_s, wcol_s, v_s, *, n_sub, ns):
    c = CHUNK
    g = pl.program_id(0)
    new = lax.rem(g, 2)
    old = 1 - new
    carry = (t_s, av_s, arb_s, kr_s, lhs_s, wcol_s, v_s)
    dr = rf.shape[-1]
    n_quad = dr // MXU_DIM
    n_chain = 2 * n_quad

    @pl.when(g == 0)
    def _():
        for ref in carry:
            ref[1] = jnp.zeros(ref.shape[1:], ref.dtype)

    @pl.when(lax.rem(jnp.maximum(g - 1, 0), ns) == 0)
    def _():
        st_ref[...] = jnp.zeros_like(st_ref)

    t_idx = lax.broadcasted_iota(jnp.int32, (c, MXU_DIM), 0)
    lane = lax.broadcasted_iota(jnp.int32, (c, MXU_DIM), 1)
    s_idx = lane & (c - 1)
    lane_head = lane >> 6
    eye = (s_idx == t_idx).astype(F32)
    bd_mask = ((lax.broadcasted_iota(jnp.int32, (MXU_DIM, MXU_DIM), 0) >> 6)
               == (lax.broadcasted_iota(jnp.int32, (MXU_DIM, MXU_DIM), 1) >> 6))
    tri = {True: (s_idx < t_idx, s_idx <= t_idx), False: (s_idx > t_idx, s_idx >= t_idx)}

    def bd(x):
        return _block_diag(x, lane_head)

    def mm(a, b):
        return jnp.dot(a, b, preferred_element_type=F32)

    def chunk_rows(k, forward):
        row0 = (k if forward else n_sub - 1 - k) * c
        return slice(row0, row0 + c)

    dirs = ((True, dict(r=rf, v=vf, kk=kkf, b=bf, kd=kdf, lw=lwf), yf_ref),
            (False, dict(r=rb, v=vb, kk=kkb, b=bb, kd=kdb, lw=lwb), yb_ref))

    def stage_a_levels(k):
        chains = []

        def level_operands():
            for i, (forward, refs, _) in enumerate(dirs):
                rows = chunk_rows(k, forward)
                a = {name: ref[rows, :].astype(F32) for name, ref in refs.items()}
                f = _decay_factors(a["lw"], forward=forward)
                kap, rt = a["kk"] * f["e_prev"], a["r"] * f["e_t"]
                kt, bt = a["kd"] * f["e_inv"], a["b"] * f["e_inv"]
                kh, bh = a["kd"] * f["e_end"], a["b"] * f["e_end"]
                for q in range(n_quad):
                    sl = slice(q * MXU_DIM, (q + 1) * MXU_DIM)
                    slot = k * n_chain + i * n_quad + q
                    kr = jnp.concatenate([kap[:, sl], rt[:, sl]], axis=0).astype(BF16)
                    kr_s[new, slot] = kr
                    lhs_s[new, slot] = jnp.concatenate([kh[:, sl], bh[:, sl]], axis=0).T.astype(BF16)
                    wcol_s[new, slot] = jnp.broadcast_to(f["w_c"][:, sl], (MXU_DIM // 2, MXU_DIM)).T
                    v_s[new, slot] = a["v"][:, sl].astype(BF16)
                    chains.append(dict(slot=slot, forward=forward, kr=kr, kt=kt[:, sl], bt=bt[:, sl],
                                       v=a["v"][:, sl]))

        def level_gram():
            for ch in chains:
                strict, incl = tri[ch["forward"]]
                g_b = _dot_nt(ch["kr"], bd(ch["bt"]))
                g_k = _dot_nt(ch["kr"], bd(ch["kt"]))
                arb_s[new, ch["slot"]] = jnp.where(incl, g_b[c:], 0.0).astype(BF16)
                ch["aa"] = jnp.concatenate([jnp.where(strict, g_k[:c], 0.0),
                                            jnp.where(incl, g_k[c:], 0.0)], axis=0)
                ch["x"] = -jnp.where(strict, g_b[:c], 0.0)
                ch["t"] = eye + ch["x"]

        def level_av():
            for ch in chains:
                av_s[new, ch["slot"]] = _dot(ch["aa"], bd(ch["v"]))

        def level_square():
            for ch in chains:
                ch["x"] = _dot(ch["x"], bd(ch["x"]))

        def level_double():
            for ch in chains:
                rr = _dot(jnp.concatenate([ch["x"], ch["t"]], axis=0), bd(ch["x"]))
                ch["x"] = rr[:c]
                ch["t"] = ch["t"] + rr[c:]

        def level_last():
            for ch in chains:
                t_s[new, ch["slot"]] = (ch["t"] + _dot(ch["t"], bd(ch["x"]))).astype(BF16)

        n_factors = CHUNK.bit_length() - 1
        return ([level_operands, level_gram, level_av, level_square]
                + [level_double] * (n_factors - 2) + [level_last])

    stage_a = [stage_a_levels(k) for k in range(n_sub)]
    n_slots = len(stage_a[0]) + A_SKEW * (n_sub - 1)
    slots_a = [[stage_a[k][s - A_SKEW * k] for k in range(n_sub)
                if 0 <= s - A_SKEW * k < len(stage_a[k])] for s in range(n_slots)]

    levels_b = []
    for k in range(n_sub):
        chains = []
        for i, (forward, _, y_ref) in enumerate(dirs):
            for q in range(n_quad):
                chains.append(dict(slot=k * n_chain + i * n_quad + q, idx=i * n_quad + q, y_ref=y_ref,
                                   rows=chunk_rows(k, forward),
                                   sl=slice(q * MXU_DIM, (q + 1) * MXU_DIM)))

        def level_state_read(chains=chains):
            for ch in chains:
                ch["m"] = st_ref[ch["idx"]]
                ch["krm"] = mm(kr_s[old, ch["slot"]], ch["m"].astype(BF16))

        def level_sa(chains=chains):
            for ch in chains:
                av = av_s[old, ch["slot"]]
                ch["sa"] = mm(t_s[old, ch["slot"]], bd(ch["krm"][:c] + av[:c]))
                ch["y0"] = ch["krm"][c:] + av[c:]

        def level_out(chains=chains):
            for ch in chains:
                sa = ch["sa"]
                y = ch["y0"] - mm(arb_s[old, ch["slot"]], bd(sa))
                ch["y_ref"][ch["rows"], ch["sl"]] = y.astype(ch["y_ref"].dtype)
                rhs = jnp.concatenate([v_s[old, ch["slot"]], (-sa).astype(BF16)], axis=0)
                upd = mm(lhs_s[old, ch["slot"]], rhs)
                w_col = wcol_s[old, ch["slot"]]
                decay = jnp.concatenate([w_col, w_col], axis=1)
                st_ref[ch["idx"]] = decay * ch["m"] + jnp.where(bd_mask, upd, 0.0)

        levels_b += [level_state_read, level_sa, level_out]

    done_b = 0
    for done_a, slot_levels in enumerate(slots_a, start=1):
        for level in slot_levels:
            level()
        while done_b < len(levels_b) and done_b * len(slots_a) < done_a * len(levels_b):
            levels_b[done_b]()
            done_b += 1


def _scan(rkvl, prep, logw, *, dr):
    bsz, seq, _ = prep.shape
    c = CHUNK
    rows = SCAN_SUB * c
    assert seq % rows == 0
    ns = seq // rows
    total = bsz * ns
    n_chain = 2 * (dr // MXU_DIM)
    ncc = SCAN_SUB * n_chain

    def cur(g):
        gi = jnp.minimum(g, total - 1)
        return gi // ns, gi % ns

    def prev(g):
        gd = jnp.maximum(g - 1, 0)
        return gd // ns, gd % ns

    def fwd(col, slot0=0):
        return pl.BlockSpec((None, rows, dr), lambda g: (cur(g)[0] + slot0, cur(g)[1], col))

    def bwd(col, slot0=0):
        return pl.BlockSpec((None, rows, dr), lambda g: (cur(g)[0] + slot0, ns - 1 - cur(g)[1], col))

    return pl.pallas_call(
        functools.partial(_scan_body, n_sub=SCAN_SUB, ns=ns),
        grid=(total + 1,),
        in_specs=[fwd(0, 1), fwd(2, 1), fwd(P_KK), fwd(P_BF), fwd(P_KDF), fwd(L_F),
                  bwd(0, 1), bwd(2, 1), bwd(P_KK), bwd(P_BB), bwd(P_KDB), bwd(L_B)],
        out_specs=[pl.BlockSpec((None, rows, dr), lambda g: (prev(g)[0], prev(g)[1], 0)),
                   pl.BlockSpec((None, rows, dr), lambda g: (prev(g)[0], ns - 1 - prev(g)[1], 0))],
        out_shape=[jax.ShapeDtypeStruct((bsz, seq, dr), BF16)] * 2,
        scratch_shapes=[pltpu.VMEM((n_chain, MXU_DIM, MXU_DIM), F32),
                        pltpu.VMEM((2, ncc, c, MXU_DIM), BF16),
                        pltpu.VMEM((2, ncc, 2 * c, MXU_DIM), F32),
                        pltpu.VMEM((2, ncc, c, MXU_DIM), BF16),
                        pltpu.VMEM((2, ncc, 2 * c, MXU_DIM), BF16),
                        pltpu.VMEM((2, ncc, MXU_DIM, 2 * c), BF16),
                        pltpu.VMEM((2, ncc, MXU_DIM, MXU_DIM // 2), F32),
                        pltpu.VMEM((2, ncc, c, MXU_DIM), BF16)],
        compiler_params=pltpu.CompilerParams(
            dimension_semantics=("arbitrary",), vmem_limit_bytes=VMEM_LIMIT),
        name="scan",
    )(rkvl, rkvl, prep, prep, prep, logw, rkvl, rkvl, prep, prep, prep, logw)


def _post_body(yf_ref, yb_ref, g_ref, bonus_ref, oconv_ref, x_ref, wout_ref, gn_ref, ones_ref,
               out_ref, *, dr):
    ones_bd = ones_ref[...]
    y = yf_ref[...].astype(F32) + yb_ref[...].astype(F32)
    mean = _head_sum(y, ones_bd) * (1.0 / HEAD)
    d = y - mean
    var = _head_sum(d * d, ones_bd) * (1.0 / HEAD)
    yn = (d * lax.rsqrt(var + GN_EPS) * gn_ref[0:1, :] + gn_ref[1:2, :]
          + bonus_ref[...].astype(F32))
    o = (yn * g_ref[...].astype(F32)).astype(BF16)
    out_ref[...] = (x_ref[...]
                    + jnp.dot(o, wout_ref[0:dr, :], preferred_element_type=F32)
                    + jnp.dot(oconv_ref[...], wout_ref[dr:, :], preferred_element_type=F32))


def _post(yf, yb, prep, oconv, x, wout, gn, ones_bd, *, dr, row_tile):
    bsz, seq, d = x.shape

    def rows(width, col=0):
        return pl.BlockSpec((None, row_tile, width), lambda b, i: (b, i, col))

    def full(a):
        return pl.BlockSpec(a.shape, lambda b, i: (0,) * a.ndim)

    return pl.pallas_call(
        functools.partial(_post_body, dr=dr),
        grid=(bsz, seq // row_tile),
        in_specs=[rows(dr), rows(dr), rows(dr, P_G), rows(dr, P_BONUS), rows(oconv.shape[-1]),
                  rows(d), full(wout), full(gn), full(ones_bd)],
        out_specs=rows(d),
        out_shape=jax.ShapeDtypeStruct((bsz, seq, d), F32),
        compiler_params=pltpu.CompilerParams(
            dimension_semantics=("arbitrary", "arbitrary"), vmem_limit_bytes=VMEM_LIMIT),
        name="post",
    )(yf, yb, prep, prep, oconv, x, wout, gn, ones_bd)


def _ffn_body(x_ref, n2_ref, wg_ref, wu_ref, wd_ref, nf_ref, out_ref, *, final_norm):
    x = x_ref[...]
    h = _rmsnorm(x, n2_ref[...]).astype(BF16)
    a = jnp.dot(h, wg_ref[...], preferred_element_type=F32)
    u = jnp.dot(h, wu_ref[...], preferred_element_type=F32)
    z = (a * jax.nn.sigmoid(a) * u).astype(BF16)
    x2 = x + jnp.dot(z, wd_ref[...], preferred_element_type=F32)
    out_ref[...] = _rmsnorm(x2, nf_ref[...]) if final_norm else x2


def _ffn(x, n2, wg, wu, wd, nf, *, final_norm, row_tile):
    n, d = x.shape

    def full(a):
        return pl.BlockSpec(a.shape, lambda i: (0,) * a.ndim, pipeline_mode=pl.Buffered(1))

    return pl.pallas_call(
        functools.partial(_ffn_body, final_norm=final_norm),
        grid=(n // row_tile,),
        in_specs=[pl.BlockSpec((row_tile, d), lambda i: (i, 0)),
                  full(n2), full(wg), full(wu), full(wd), full(nf)],
        out_specs=pl.BlockSpec((row_tile, d), lambda i: (i, 0)),
        out_shape=jax.ShapeDtypeStruct((n, d), F32),
        compiler_params=pltpu.CompilerParams(
            dimension_semantics=("arbitrary",), vmem_limit_bytes=VMEM_LIMIT),
        name="ffn",
    )(x, n2, wg, wu, wd, nf)


def _pad_cols(a, width):
    return jnp.pad(a, ((0, 0), (0, width - a.shape[1])))


def _pad_rows(a, height):
    return jnp.pad(a, ((0, height - a.shape[0]), (0, 0)))


def kernel(x, norm1_w, w_in, mu_shift, w_up_f, w0_f, w_up_b, w0_b, a_up_f, a0_f, a_up_b, a0_b,
           g_up, k_k, k_a_f, k_a_b, r_k_f, r_k_b, gn_w, gn_b, conv_w, w_out, norm2_w, w_gate,
           w_up, w_down, norm_f_w):
    bsz, seq, d = x.shape
    depth = w_in.shape[0]
    dr = w0_f.shape[-1]
    n_dec, n_aaa, n_gate = w_up_f.shape[1], a_up_f.shape[1], g_up.shape[1]
    d_conv = conv_w.shape[-1]
    assert dr % MXU_DIM == 0 and d_conv % COL_TILE == 0 and seq % CHUNK == 0
    assert n_dec + n_aaa <= 128 and n_gate <= 256
    o_xw = 3 * dr
    o_xa = o_xw + n_dec
    o_xg = o_xa + n_aaa
    o_conv = o_xg + n_gate

    head_id = jnp.arange(dr) // HEAD
    ones_bd = (head_id[:, None] == head_id[None, :]).astype(BF16)

    for l in range(depth):
        w = w_in[l]
        gb, gc, hh = (w[:, o_conv + i * d_conv:o_conv + (i + 1) * d_conv] for i in range(3))
        conv_cols = []
        for t in range(d_conv // COL_TILE):
            cs = slice(t * COL_TILE, (t + 1) * COL_TILE)
            conv_cols += [gb[:, cs], gc[:, cs], hh[:, cs]]
        w_shift = jnp.concatenate(
            [w[:, :o_xw], _pad_cols(w[:, o_xw:o_xg], COL_TILE), _pad_cols(w[:, o_xg:o_conv], COL_TILE)],
            axis=1).astype(BF16)
        w_conv = jnp.concatenate(conv_cols, axis=1).astype(BF16)
        mu = mu_shift[l][None, :]
        mu_p = jnp.concatenate(
            [mu[:, :o_xw], _pad_cols(mu[:, o_xw:o_xg], COL_TILE), _pad_cols(mu[:, o_xg:o_conv], COL_TILE)],
            axis=1)

        rkvl, oconv = _inproj(x, norm1_w[l][None, :], w_shift, mu_p, w_conv, conv_w[l])

        vecs = jnp.stack([w0_f[l], w0_b[l], a0_f[l], a0_b[l], k_k[l], k_a_f[l], k_a_b[l],
                          r_k_f[l].reshape(dr), r_k_b[l].reshape(dr)], axis=0)
        vecs = _pad_rows(vecs, 16)
        wup = _pad_rows(jnp.concatenate([w_up_f[l], w_up_b[l]], axis=1), 128).astype(BF16)
        aup = jnp.concatenate([a_up_f[l], a_up_b[l]], axis=1)
        aup = jnp.pad(aup, ((n_dec, 128 - n_dec - n_aaa), (0, 0))).astype(BF16)
        gup = _pad_rows(g_up[l], 256).astype(BF16)
        prep, logw = _prep(rkvl, vecs, wup, aup, gup, ones_bd, dr=dr, row_tile=512)

        yf, yb = _scan(rkvl, prep, logw, dr=dr)

        gn = jnp.stack([gn_w[l], gn_b[l]], axis=0)
        x1 = _post(yf, yb, prep, oconv, x, w_out[l].astype(BF16), gn, ones_bd, dr=dr, row_tile=512)

        x = _ffn(x1.reshape(bsz * seq, d), norm2_w[l][None, :], w_gate[l].astype(BF16),
                 w_up[l].astype(BF16), w_down[l].astype(BF16), norm_f_w[None, :],
                 final_norm=(l == depth - 1), row_tile=512).reshape(bsz, seq, d)
    return x
```

```python
import functools

import jax
import jax.numpy as jnp
from jax import lax
from jax.experimental import pallas as pl
from jax.experimental.pallas import tpu as pltpu

F32 = jnp.float32
BF16 = jnp.bfloat16

HEAD = 64
CHUNK = 64
MXU_DIM = 256
QUAD = MXU_DIM // CHUNK
SCAN_SUB = 8
COL_TILE = 256
ROW_CHUNK = 512
HALO = 8
LOG_DECAY_SCALE = 0.606531
RMS_EPS = 1e-6
GN_EPS = 64e-5
NORM_EPS = 1e-12
VMEM_LIMIT = 56 * 1024 * 1024

P_KK, P_BF, P_BB, P_KDF, P_KDB, P_G, P_BONUS, P_NUM = range(8)
L_F, L_B, L_NUM = range(3)
V_W0F, V_W0B, V_A0F, V_A0B, V_KK, V_KAF, V_KAB, V_RKF, V_RKB, V_NUM = range(10)


def _dot(a, b):
    return jnp.dot(a.astype(BF16), b.astype(BF16), preferred_element_type=F32)


def _dot_nt(a, b):
    return lax.dot_general(a.astype(BF16), b.astype(BF16), (((1,), (1,)), ((), ())),
                           preferred_element_type=F32)


def _split2(x):
    hi = x.astype(BF16)
    lo = (x - hi.astype(F32)).astype(BF16)
    return hi, lo


def _split3(x):
    hi = x.astype(BF16)
    r1 = x - hi.astype(F32)
    mid = r1.astype(BF16)
    lo = (r1 - mid.astype(F32)).astype(BF16)
    return hi, mid, lo


def _head_sum(x, ones_bd):
    hi, lo = _split2(x)
    return (jnp.dot(hi, ones_bd, preferred_element_type=F32)
            + jnp.dot(lo, ones_bd, preferred_element_type=F32))


def _rmsnorm(x, w):
    ms = jnp.mean(x * x, axis=-1, keepdims=True)
    return x * lax.rsqrt(ms + RMS_EPS) * w


def _zero_halo(s_ref, seq):
    z = jnp.zeros((HALO, s_ref.shape[1]), s_ref.dtype)
    s_ref[0:HALO, :] = z
    s_ref[HALO + seq:2 * HALO + seq, :] = z


def _windows(s_ref, c0, rows):
    width = s_ref.shape[1]
    win = s_ref[c0:c0 + rows + 2 * HALO, :].reshape(rows // HALO + 2, HALO, width)
    down = pltpu.roll(win, 1, 1)
    up = pltpu.roll(win, HALO - 1, 1)
    sub = lax.broadcasted_iota(jnp.int32, (1, HALO, width), 1)
    prv = jnp.where(sub == 0, down[:-2], down[1:-1])
    nxt = jnp.where(sub == HALO - 1, up[2:], up[1:-1])
    return tuple(a.reshape(rows, width) for a in (win[1:-1], prv, nxt))


def _shift_body(x_ref, n1_ref, w_ref, mu_ref, out_ref, hout_ref, h_s, p_s, *, seq):
    bb = pl.program_id(0)
    j = pl.program_id(1)
    new = lax.rem(bb, 2)
    old = 1 - new

    @pl.when((bb == 0) & (j == 0))
    def _():
        _zero_halo(p_s, seq)
        h_s[1] = jnp.zeros(h_s.shape[1:], h_s.dtype)

    n_rows = x_ref.shape[0]
    h = _rmsnorm(x_ref[...], n1_ref[...]).astype(BF16)
    h_s[new, pl.ds(pl.multiple_of(j * n_rows, n_rows), n_rows), :] = h
    hout_ref[...] = h

    w = w_ref[...]
    mu = mu_ref[...]

    def shift(c0):
        cur, prv, nxt = _windows(p_s, c0, ROW_CHUNK)
        out_ref[c0:c0 + ROW_CHUNK, :] = (cur + mu * (0.5 * (prv + nxt) - cur)).astype(out_ref.dtype)

    for c0 in range(0, seq, ROW_CHUNK):
        p_s[HALO + c0:HALO + c0 + ROW_CHUNK, :] = jnp.dot(
            h_s[old, c0:c0 + ROW_CHUNK, :], w, preferred_element_type=F32)
        if c0:
            shift(c0 - ROW_CHUNK)
    shift(seq - ROW_CHUNK)


def _conv_body(h_ref, w_ref, cw_ref, out_ref, u_s, gb_s, *, seq):
    @pl.when(pl.program_id(1) == 0)
    def _():
        _zero_halo(u_s, seq)

    w = w_ref[...]
    cw = cw_ref[...]

    def conv(c0):
        cur, prv, nxt = _windows(u_s, c0, ROW_CHUNK)
        y = cw[0:1, :] * prv + cw[1:2, :] * cur + cw[2:3, :] * nxt
        out_ref[c0:c0 + ROW_CHUNK, :] = (gb_s[c0:c0 + ROW_CHUNK, :] * y).astype(out_ref.dtype)

    for c0 in range(0, seq, ROW_CHUNK):
        p = jnp.dot(h_ref[c0:c0 + ROW_CHUNK, :], w, preferred_element_type=F32)
        gb_s[c0:c0 + ROW_CHUNK, :] = p[:, :COL_TILE]
        u_s[HALO + c0:HALO + c0 + ROW_CHUNK, :] = p[:, COL_TILE:2 * COL_TILE] * p[:, 2 * COL_TILE:]
        if c0:
            conv(c0 - ROW_CHUNK)
    conv(seq - ROW_CHUNK)


def _inproj(x, n1, w_shift, mu_p, w_conv, conv_w):
    bsz, seq, d = x.shape
    assert seq % ROW_CHUNK == 0
    params = pltpu.CompilerParams(
        dimension_semantics=("arbitrary", "arbitrary"), vmem_limit_bytes=VMEM_LIMIT)
    n_tiles = w_shift.shape[1] // COL_TILE
    assert seq % n_tiles == 0
    norm_rows = seq // n_tiles

    def nxt(b):
        return jnp.minimum(b, bsz - 1)

    rkvl, h = pl.pallas_call(
        functools.partial(_shift_body, seq=seq),
        grid=(bsz + 1, n_tiles),
        in_specs=[
            pl.BlockSpec((None, norm_rows, d), lambda b, j: (nxt(b), j, 0)),
            pl.BlockSpec((1, d), lambda b, j: (0, 0)),
            pl.BlockSpec((d, COL_TILE), lambda b, j: (0, j)),
            pl.BlockSpec((1, COL_TILE), lambda b, j: (0, j)),
        ],
        out_specs=[
            pl.BlockSpec((None, seq, COL_TILE), lambda b, j: (b, 0, j)),
            pl.BlockSpec((None, norm_rows, d), lambda b, j: (b, j, 0)),
        ],
        out_shape=[
            jax.ShapeDtypeStruct((bsz + 1, seq, w_shift.shape[1]), BF16),
            jax.ShapeDtypeStruct((bsz + 1, seq, d), BF16),
        ],
        scratch_shapes=[pltpu.VMEM((2, seq, d), BF16),
                        pltpu.VMEM((seq + 2 * HALO, COL_TILE), F32)],
        compiler_params=params,
        name="inproj_shift",
    )(x, n1, w_shift, mu_p)
    d_conv = conv_w.shape[-1]
    oconv = pl.pallas_call(
        functools.partial(_conv_body, seq=seq),
        grid=(bsz, d_conv // COL_TILE),
        in_specs=[
            pl.BlockSpec((None, seq, d), lambda b, q: (b, 0, 0)),
            pl.BlockSpec((d, 3 * COL_TILE), lambda b, q: (0, q)),
            pl.BlockSpec((3, COL_TILE), lambda b, q: (0, q)),
        ],
        out_specs=pl.BlockSpec((None, seq, COL_TILE), lambda b, q: (b, 0, q)),
        out_shape=jax.ShapeDtypeStruct((bsz, seq, d_conv), BF16),
        scratch_shapes=[pltpu.VMEM((seq + 2 * HALO, COL_TILE), F32),
                        pltpu.VMEM((seq, COL_TILE), F32)],
        compiler_params=params,
        name="inproj_conv",
    )(h, w_conv, conv_w)
    return rkvl, oconv


def _prep_body(r_ref, k_ref, v_ref, l_ref, vec_ref, wup_ref, aup_ref, gup_ref, ones_ref,
               out_ref, lw_ref, *, dr):
    r = r_ref[...].astype(F32)
    k = k_ref[...].astype(F32)
    v = v_ref[...].astype(F32)
    vec = vec_ref[...]
    ones_bd = ones_ref[...]

    def vrow(i):
        return vec[i:i + 1, :]

    t128 = l_ref[:, 0:128]
    lw = jnp.dot(jnp.tanh(t128.astype(F32)).astype(BF16), wup_ref[...],
                 preferred_element_type=F32)
    aa = jnp.dot(t128, aup_ref[...], preferred_element_type=F32)
    lw_f = -LOG_DECAY_SCALE * jax.nn.sigmoid(vrow(V_W0F) + lw[:, :dr])
    lw_b = -LOG_DECAY_SCALE * jax.nn.sigmoid(vrow(V_W0B) + lw[:, dr:])
    a_f = jax.nn.sigmoid(vrow(V_A0F) + aa[:, :dr])
    a_b = jax.nn.sigmoid(vrow(V_A0B) + aa[:, dr:])
    g = jnp.dot(jax.nn.sigmoid(l_ref[:, 256:512].astype(F32)).astype(BF16), gup_ref[...],
                preferred_element_type=F32)

    kkr = k * vrow(V_KK)
    ssq = _head_sum(kkr * kkr, ones_bd)
    kk = kkr / jnp.maximum(jnp.sqrt(ssq), NORM_EPS)
    kd_f = k * (1.0 + (a_f - 1.0) * vrow(V_KAF))
    kd_b = k * (1.0 + (a_b - 1.0) * vrow(V_KAB))
    bonus = _head_sum(r * kd_f * vrow(V_RKF) + r * kd_b * vrow(V_RKB), ones_bd) * v

    def put(i, val):
        out_ref[:, i * dr:(i + 1) * dr] = val.astype(out_ref.dtype)

    put(P_KK, kk)
    put(P_BF, a_f * kk)
    put(P_BB, a_b * kk)
    put(P_KDF, kd_f)
    put(P_KDB, kd_b)
    put(P_G, g)
    put(P_BONUS, bonus)
    lw_ref[:, L_F * dr:(L_F + 1) * dr] = lw_f
    lw_ref[:, L_B * dr:(L_B + 1) * dr] = lw_b


def _prep(rkvl, vecs, wup, aup, gup, ones_bd, *, dr, row_tile):
    bsz, seq = rkvl.shape[0] - 1, rkvl.shape[1]

    def col(c):
        return pl.BlockSpec((None, row_tile, dr), lambda b, i: (b + 1, i, c))

    def full(a):
        return pl.BlockSpec(a.shape, lambda b, i: (0,) * a.ndim)

    return pl.pallas_call(
        functools.partial(_prep_body, dr=dr),
        grid=(bsz, seq // row_tile),
        in_specs=[col(0), col(1), col(2), col(3),
                  full(vecs), full(wup), full(aup), full(gup), full(ones_bd)],
        out_specs=[pl.BlockSpec((None, row_tile, P_NUM * dr), lambda b, i: (b, i, 0)),
                   pl.BlockSpec((None, row_tile, L_NUM * dr), lambda b, i: (b, i, 0))],
        out_shape=[jax.ShapeDtypeStruct((bsz, seq, P_NUM * dr), BF16),
                   jax.ShapeDtypeStruct((bsz, seq, L_NUM * dr), F32)],
        compiler_params=pltpu.CompilerParams(
            dimension_semantics=("arbitrary", "arbitrary"), vmem_limit_bytes=VMEM_LIMIT),
        name="prep",
    )(rkvl, rkvl, rkvl, rkvl, vecs, wup, aup, gup, ones_bd)


def _block_diag(x, lane_head):
    return jnp.concatenate([jnp.where(lane_head == h, x, 0.0) for h in range(QUAD)],
                           axis=0).astype(BF16)


def _decay_factors(lw, *, forward):
    c = CHUNK
    row = lax.broadcasted_iota(jnp.int32, lw.shape, 0)
    cs = lw
    step = 1
    while step < c:
        if forward:
            cs = cs + jnp.where(row >= step, pltpu.roll(cs, step, 0), 0.0)
        else:
            cs = cs + jnp.where(row < c - step, pltpu.roll(cs, c - step, 0), 0.0)
        step *= 2
    cs_end = cs[c - 1:c, :] if forward else cs[0:1, :]
    return dict(e_t=jnp.exp(cs), e_prev=jnp.exp(cs - lw), e_inv=jnp.exp(-cs),
                e_end=jnp.exp(cs_end - cs), w_c=jnp.exp(cs_end))


def _scan_step(dirs, st_ref, n_sub):
    c = CHUNK
    t_idx = lax.broadcasted_iota(jnp.int32, (c, MXU_DIM), 0)
    lane = lax.broadcasted_iota(jnp.int32, (c, MXU_DIM), 1)
    s_idx = lane & (c - 1)
    lane_head = lane >> 6
    eye = (s_idx == t_idx).astype(F32)
    bd_mask = ((lax.broadcasted_iota(jnp.int32, (MXU_DIM, MXU_DIM), 0) >> 6)
               == (lax.broadcasted_iota(jnp.int32, (MXU_DIM, MXU_DIM), 1) >> 6))

    def bd(x):
        return _block_diag(x, lane_head)

    subs = []
    for k in range(n_sub):
        chains = []
        for i, (forward, refs, y_ref) in enumerate(dirs):
            row0 = (k if forward else n_sub - 1 - k) * c
            rows = slice(row0, row0 + c)
            a = {name: ref[rows, :].astype(F32) for name, ref in refs.items()}
            f = _decay_factors(a["lw"], forward=forward)
            if forward:
                strict, incl = s_idx < t_idx, s_idx <= t_idx
            else:
                strict, incl = s_idx > t_idx, s_idx >= t_idx
            kap, rt = a["kk"] * f["e_prev"], a["r"] * f["e_t"]
            kt, bt = a["kd"] * f["e_inv"], a["b"] * f["e_inv"]
            kh, bh = a["kd"] * f["e_end"], a["b"] * f["e_end"]
            n_quad = a["r"].shape[-1] // MXU_DIM
            for q in range(n_quad):
                sl = slice(q * MXU_DIM, (q + 1) * MXU_DIM)
                chains.append(dict(
                    strict=strict, incl=incl, y_ref=y_ref, rows=rows, sl=sl, idx=i * n_quad + q,
                    kr=jnp.concatenate([kap[:, sl], rt[:, sl]], axis=0).astype(BF16),
                    kt=kt[:, sl], bt=bt[:, sl], v=a["v"][:, sl],
                    w_col=jnp.broadcast_to(f["w_c"][:, sl], (MXU_DIM // 2, MXU_DIM)).T,
                    lhs_t=jnp.concatenate([kh[:, sl], bh[:, sl]], axis=0).T.astype(BF16)))
        subs.append(chains)
    every = [ch for chains in subs for ch in chains]

    for ch in every:
        g_b = _dot_nt(ch["kr"], bd(ch["bt"]))
        g_k = _dot_nt(ch["kr"], bd(ch["kt"]))
        ch["arb"] = jnp.where(ch["incl"], g_b[c:], 0.0)
        ch["aa"] = jnp.concatenate([jnp.where(ch["strict"], g_k[:c], 0.0),
                                    jnp.where(ch["incl"], g_k[c:], 0.0)], axis=0)
        ch["x"] = -jnp.where(ch["strict"], g_b[:c], 0.0)
        ch["t"] = eye + ch["x"]
    for ch in every:
        ch["av"] = _dot(ch["aa"], bd(ch["v"]))

    n_factors = CHUNK.bit_length() - 1
    for ch in every:
        ch["x"] = _dot(ch["x"], bd(ch["x"]))
    for _ in range(n_factors - 2):
        for ch in every:
            rr = _dot(jnp.concatenate([ch["x"], ch["t"]], axis=0), bd(ch["x"]))
            ch["x"] = rr[:c]
            ch["t"] = ch["t"] + rr[c:]
    for ch in every:
        ch["t"] = ch["t"] + _dot(ch["t"], bd(ch["x"]))

    for chains in subs:
        for ch in chains:
            ch["m"] = st_ref[ch["idx"]]
            ch["krm"] = _dot(ch["kr"], ch["m"])
        for ch in chains:
            ch["sa"] = _dot(ch["t"], bd(ch["krm"][:c] + ch["av"][:c]))
        for ch in chains:
            sa = ch["sa"]
            y = ch["krm"][c:] + ch["av"][c:] - _dot(ch["arb"], bd(sa))
            ch["y_ref"][ch["rows"], ch["sl"]] = y.astype(ch["y_ref"].dtype)
            upd = _dot(ch["lhs_t"], jnp.concatenate([ch["v"], -sa], axis=0))
            decay = jnp.concatenate([ch["w_col"], ch["w_col"]], axis=1)
            st_ref[ch["idx"]] = decay * ch["m"] + jnp.where(bd_mask, upd, 0.0)


def _scan_body(rf, vf, kkf, bf, kdf, lwf, rb, vb, kkb, bb, kdb, lwb, yf_ref, yb_ref, st_ref,
               *, n_sub):
    @pl.when(pl.program_id(1) == 0)
    def _():
        st_ref[...] = jnp.zeros_like(st_ref)

    def operands(r, v, kk, b, kd, lw):
        return dict(r=r, v=v, kk=kk, b=b, kd=kd, lw=lw)

    _scan_step([(True, operands(rf, vf, kkf, bf, kdf, lwf), yf_ref),
                (False, operands(rb, vb, kkb, bb, kdb, lwb), yb_ref)], st_ref, n_sub)


def _scan(rkvl, prep, logw, *, dr):
    bsz, seq, _ = prep.shape
    rows = SCAN_SUB * CHUNK
    assert seq % rows == 0
    ns = seq // rows
    n_quad = dr // MXU_DIM

    def fwd(col, slot0=0):
        return pl.BlockSpec((None, rows, dr), lambda b, c: (b + slot0, c, col))

    def bwd(col, slot0=0):
        return pl.BlockSpec((None, rows, dr), lambda b, c: (b + slot0, ns - 1 - c, col))

    return pl.pallas_call(
        functools.partial(_scan_body, n_sub=SCAN_SUB),
        grid=(bsz, ns),
        in_specs=[fwd(0, 1), fwd(2, 1), fwd(P_KK), fwd(P_BF), fwd(P_KDF), fwd(L_F),
                  bwd(0, 1), bwd(2, 1), bwd(P_KK), bwd(P_BB), bwd(P_KDB), bwd(L_B)],
        out_specs=[pl.BlockSpec((None, rows, dr), lambda b, c: (b, c, 0)),
                   pl.BlockSpec((None, rows, dr), lambda b, c: (b, ns - 1 - c, 0))],
        out_shape=[jax.ShapeDtypeStruct((bsz, seq, dr), BF16)] * 2,
        scratch_shapes=[pltpu.VMEM((2 * n_quad, MXU_DIM, MXU_DIM), F32)],
        compiler_params=pltpu.CompilerParams(
            dimension_semantics=("arbitrary", "arbitrary"), vmem_limit_bytes=VMEM_LIMIT),
        name="scan",
    )(rkvl, rkvl, prep, prep, prep, logw, rkvl, rkvl, prep, prep, prep, logw)


def _post_ffn_body(yf_ref, yb_ref, g_ref, bonus_ref, oconv_ref, x_ref, wout_ref, gn_ref, ones_ref,
                   n2_ref, wg_ref, wu_ref, wd_ref, nf_ref, out_ref, *, dr, final_norm):
    ones_bd = ones_ref[...]
    y = yf_ref[...].astype(F32) + yb_ref[...].astype(F32)
    mean = _head_sum(y, ones_bd) * (1.0 / HEAD)
    d = y - mean
    var = _head_sum(d * d, ones_bd) * (1.0 / HEAD)
    yn = (d * lax.rsqrt(var + GN_EPS) * gn_ref[0:1, :] + gn_ref[1:2, :]
          + bonus_ref[...].astype(F32))
    o = (yn * g_ref[...].astype(F32)).astype(BF16)
    x1 = (x_ref[...]
          + jnp.dot(o, wout_ref[0:dr, :], preferred_element_type=F32)
          + jnp.dot(oconv_ref[...], wout_ref[dr:, :], preferred_element_type=F32))

    h = _rmsnorm(x1, n2_ref[...]).astype(BF16)
    a = jnp.dot(h, wg_ref[...], preferred_element_type=F32)
    u = jnp.dot(h, wu_ref[...], preferred_element_type=F32)
    z = (a * jax.nn.sigmoid(a) * u).astype(BF16)
    x2 = x1 + jnp.dot(z, wd_ref[...], preferred_element_type=F32)
    out_ref[...] = _rmsnorm(x2, nf_ref[...]) if final_norm else x2


def _post_ffn(yf, yb, prep, oconv, x, wout, gn, ones_bd, n2, wg, wu, wd, nf, *, dr, final_norm,
              row_tile):
    n, d = x.shape

    def rows(width, col=0):
        return pl.BlockSpec((row_tile, width), lambda i: (i, col))

    def full(a):
        return pl.BlockSpec(a.shape, lambda i: (0,) * a.ndim, pipeline_mode=pl.Buffered(1))

    return pl.pallas_call(
        functools.partial(_post_ffn_body, dr=dr, final_norm=final_norm),
        grid=(n // row_tile,),
        in_specs=[rows(dr), rows(dr), rows(dr, P_G), rows(dr, P_BONUS), rows(oconv.shape[-1]),
                  rows(d), full(wout), full(gn), full(ones_bd),
                  full(n2), full(wg), full(wu), full(wd), full(nf)],
        out_specs=rows(d),
        out_shape=jax.ShapeDtypeStruct((n, d), F32),
        compiler_params=pltpu.CompilerParams(
            dimension_semantics=("arbitrary",), vmem_limit_bytes=VMEM_LIMIT),
        name="post_ffn",
    )(yf, yb, prep, prep, oconv, x, wout, gn, ones_bd, n2, wg, wu, wd, nf)


def _pad_cols(a, width):
    return jnp.pad(a, ((0, 0), (0, width - a.shape[1])))


def _pad_rows(a, height):
    return jnp.pad(a, ((0, height - a.shape[0]), (0, 0)))


def kernel(x, norm1_w, w_in, mu_shift, w_up_f, w0_f, w_up_b, w0_b, a_up_f, a0_f, a_up_b, a0_b,
           g_up, k_k, k_a_f, k_a_b, r_k_f, r_k_b, gn_w, gn_b, conv_w, w_out, norm2_w, w_gate,
           w_up, w_down, norm_f_w):
    bsz, seq, d = x.shape
    depth = w_in.shape[0]
    dr = w0_f.shape[-1]
    n_dec, n_aaa, n_gate = w_up_f.shape[1], a_up_f.shape[1], g_up.shape[1]
    d_conv = conv_w.shape[-1]
    assert dr % MXU_DIM == 0 and d_conv % COL_TILE == 0 and seq % CHUNK == 0
    assert n_dec + n_aaa <= 128 and n_gate <= 256
    o_xw = 3 * dr
    o_xa = o_xw + n_dec
    o_xg = o_xa + n_aaa
    o_conv = o_xg + n_gate

    head_id = jnp.arange(dr) // HEAD
    ones_bd = (head_id[:, None] == head_id[None, :]).astype(BF16)

    for l in range(depth):
        w = w_in[l]
        gb, gc, hh = (w[:, o_conv + i * d_conv:o_conv + (i + 1) * d_conv] for i in range(3))
        conv_cols = []
        for t in range(d_conv // COL_TILE):
            cs = slice(t * COL_TILE, (t + 1) * COL_TILE)
            conv_cols += [gb[:, cs], gc[:, cs], hh[:, cs]]
        w_shift = jnp.concatenate(
            [w[:, :o_xw], _pad_cols(w[:, o_xw:o_xg], COL_TILE), _pad_cols(w[:, o_xg:o_conv], COL_TILE)],
            axis=1).astype(BF16)
        w_conv = jnp.concatenate(conv_cols, axis=1).astype(BF16)
        mu = mu_shift[l][None, :]
        mu_p = jnp.concatenate(
            [mu[:, :o_xw], _pad_cols(mu[:, o_xw:o_xg], COL_TILE), _pad_cols(mu[:, o_xg:o_conv], COL_TILE)],
            axis=1)

        rkvl, oconv = _inproj(x, norm1_w[l][None, :], w_shift, mu_p, w_conv, conv_w[l])

        vecs = jnp.stack([w0_f[l], w0_b[l], a0_f[l], a0_b[l], k_k[l], k_a_f[l], k_a_b[l],
                          r_k_f[l].reshape(dr), r_k_b[l].reshape(dr)], axis=0)
        vecs = _pad_rows(vecs, 16)
        wup = _pad_rows(jnp.concatenate([w_up_f[l], w_up_b[l]], axis=1), 128).astype(BF16)
        aup = jnp.concatenate([a_up_f[l], a_up_b[l]], axis=1)
        aup = jnp.pad(aup, ((n_dec, 128 - n_dec - n_aaa), (0, 0))).astype(BF16)
        gup = _pad_rows(g_up[l], 256).astype(BF16)
        prep, logw = _prep(rkvl, vecs, wup, aup, gup, ones_bd, dr=dr, row_tile=512)

        yf, yb = _scan(rkvl, prep, logw, dr=dr)

        gn = jnp.stack([gn_w[l], gn_b[l]], axis=0)
        n_tok = bsz * seq
        x = _post_ffn(yf.reshape(n_tok, dr), yb.reshape(n_tok, dr), prep.reshape(n_tok, P_NUM * dr),
                      oconv.reshape(n_tok, d_conv), x.reshape(n_tok, d), w_out[l].astype(BF16), gn,
                      ones_bd, norm2_w[l][None, :], w_gate[l].astype(BF16), w_up[l].astype(BF16),
                      w_down[l].astype(BF16), norm_f_w[None, :], dr=dr,
                      final_norm=(l == depth - 1), row_tile=512).reshape(bsz, seq, d)
    return x
```

```python
import functools

import jax
import jax.numpy as jnp
from jax import lax
from jax.experimental import pallas as pl
from jax.experimental.pallas import tpu as pltpu

F32 = jnp.float32
BF16 = jnp.bfloat16

HEAD = 64
CHUNK = 64
MXU_DIM = 256
QUAD = MXU_DIM // CHUNK
SCAN_SUB = 8
COL_TILE = 256
ROW_CHUNK = 512
HALO = 8
LOG_DECAY_SCALE = 0.606531
RMS_EPS = 1e-6
GN_EPS = 64e-5
NORM_EPS = 1e-12
VMEM_LIMIT = 56 * 1024 * 1024

P_KK, P_BF, P_BB, P_KDF, P_KDB, P_G, P_BONUS, P_NUM = range(8)
L_F, L_B, L_NUM = range(3)
V_W0F, V_W0B, V_A0F, V_A0B, V_KK, V_KAF, V_KAB, V_RKF, V_RKB, V_NUM = range(10)


def _dot(a, b):
    return jnp.dot(a.astype(BF16), b.astype(BF16), preferred_element_type=F32)


def _dot_nt(a, b):
    return lax.dot_general(a.astype(BF16), b.astype(BF16), (((1,), (1,)), ((), ())),
                           preferred_element_type=F32)


def _split2(x):
    hi = x.astype(BF16)
    lo = (x - hi.astype(F32)).astype(BF16)
    return hi, lo


def _split3(x):
    hi = x.astype(BF16)
    r1 = x - hi.astype(F32)
    mid = r1.astype(BF16)
    lo = (r1 - mid.astype(F32)).astype(BF16)
    return hi, mid, lo


def _head_sum(x, ones_bd):
    hi, lo = _split2(x)
    return (jnp.dot(hi, ones_bd, preferred_element_type=F32)
            + jnp.dot(lo, ones_bd, preferred_element_type=F32))


def _rmsnorm(x, w):
    ms = jnp.mean(x * x, axis=-1, keepdims=True)
    return x * lax.rsqrt(ms + RMS_EPS) * w


def _zero_halo(s_ref, seq):
    z = jnp.zeros((HALO, s_ref.shape[1]), s_ref.dtype)
    s_ref[0:HALO, :] = z
    s_ref[HALO + seq:2 * HALO + seq, :] = z


def _windows(s_ref, c0, rows):
    width = s_ref.shape[1]
    win = s_ref[c0:c0 + rows + 2 * HALO, :].reshape(rows // HALO + 2, HALO, width)
    down = pltpu.roll(win, 1, 1)
    up = pltpu.roll(win, HALO - 1, 1)
    sub = lax.broadcasted_iota(jnp.int32, (1, HALO, width), 1)
    prv = jnp.where(sub == 0, down[:-2], down[1:-1])
    nxt = jnp.where(sub == HALO - 1, up[2:], up[1:-1])
    return tuple(a.reshape(rows, width) for a in (win[1:-1], prv, nxt))


def _shift_body(x_ref, n1_ref, w_ref, mu_ref, out_ref, hout_ref, h_s, p_s, *, seq):
    bb = pl.program_id(0)
    j = pl.program_id(1)
    new = lax.rem(bb, 2)
    old = 1 - new

    @pl.when((bb == 0) & (j == 0))
    def _():
        _zero_halo(p_s, seq)
        h_s[1] = jnp.zeros(h_s.shape[1:], h_s.dtype)

    n_rows = x_ref.shape[0]
    h = _rmsnorm(x_ref[...], n1_ref[...]).astype(BF16)
    h_s[new, pl.ds(pl.multiple_of(j * n_rows, n_rows), n_rows), :] = h
    hout_ref[...] = h

    w = w_ref[...]
    mu = mu_ref[...]
    keep, mix = 1.0 - mu, 0.5 * mu

    def shift(c0):
        cur, prv, nxt = _windows(p_s, c0, ROW_CHUNK)
        out_ref[c0:c0 + ROW_CHUNK, :] = (keep * cur + mix * (prv + nxt)).astype(out_ref.dtype)

    for c0 in range(0, seq, ROW_CHUNK):
        p_s[HALO + c0:HALO + c0 + ROW_CHUNK, :] = jnp.dot(
            h_s[old, c0:c0 + ROW_CHUNK, :], w, preferred_element_type=F32)
        if c0:
            shift(c0 - ROW_CHUNK)
    shift(seq - ROW_CHUNK)


def _conv_body(h_ref, w_ref, cw_ref, out_ref, u_s, gb_s, *, seq):
    @pl.when(pl.program_id(1) == 0)
    def _():
        _zero_halo(u_s, seq)

    w = w_ref[...]
    cw = cw_ref[...]

    def conv(c0):
        cur, prv, nxt = _windows(u_s, c0, ROW_CHUNK)
        y = cw[0:1, :] * prv + cw[1:2, :] * cur + cw[2:3, :] * nxt
        out_ref[c0:c0 + ROW_CHUNK, :] = (gb_s[c0:c0 + ROW_CHUNK, :] * y).astype(out_ref.dtype)

    for c0 in range(0, seq, ROW_CHUNK):
        p = jnp.dot(h_ref[c0:c0 + ROW_CHUNK, :], w, preferred_element_type=F32)
        gb_s[c0:c0 + ROW_CHUNK, :] = p[:, :COL_TILE]
        u_s[HALO + c0:HALO + c0 + ROW_CHUNK, :] = p[:, COL_TILE:2 * COL_TILE] * p[:, 2 * COL_TILE:]
        if c0:
            conv(c0 - ROW_CHUNK)
    conv(seq - ROW_CHUNK)


def _inproj(x, n1, w_shift, mu_p, w_conv, conv_w):
    bsz, seq, d = x.shape
    assert seq % ROW_CHUNK == 0
    params = pltpu.CompilerParams(
        dimension_semantics=("arbitrary", "arbitrary"), vmem_limit_bytes=VMEM_LIMIT)
    n_tiles = w_shift.shape[1] // COL_TILE
    assert seq % n_tiles == 0
    norm_rows = seq // n_tiles

    def nxt(b):
        return jnp.minimum(b, bsz - 1)

    rkvl, h = pl.pallas_call(
        functools.partial(_shift_body, seq=seq),
        grid=(bsz + 1, n_tiles),
        in_specs=[
            pl.BlockSpec((None, norm_rows, d), lambda b, j: (nxt(b), j, 0)),
            pl.BlockSpec((1, d), lambda b, j: (0, 0)),
            pl.BlockSpec((d, COL_TILE), lambda b, j: (0, j)),
            pl.BlockSpec((1, COL_TILE), lambda b, j: (0, j)),
        ],
        out_specs=[
            pl.BlockSpec((None, seq, COL_TILE), lambda b, j: (b, 0, j)),
            pl.BlockSpec((None, norm_rows, d), lambda b, j: (b, j, 0)),
        ],
        out_shape=[
            jax.ShapeDtypeStruct((bsz + 1, seq, w_shift.shape[1]), BF16),
            jax.ShapeDtypeStruct((bsz + 1, seq, d), BF16),
        ],
        scratch_shapes=[pltpu.VMEM((2, seq, d), BF16),
                        pltpu.VMEM((seq + 2 * HALO, COL_TILE), F32)],
        compiler_params=params,
        name="inproj_shift",
    )(x, n1, w_shift, mu_p)
    d_conv = conv_w.shape[-1]
    oconv = pl.pallas_call(
        functools.partial(_conv_body, seq=seq),
        grid=(bsz, d_conv // COL_TILE),
        in_specs=[
            pl.BlockSpec((None, seq, d), lambda b, q: (b, 0, 0)),
            pl.BlockSpec((d, 3 * COL_TILE), lambda b, q: (0, q)),
            pl.BlockSpec((3, COL_TILE), lambda b, q: (0, q)),
        ],
        out_specs=pl.BlockSpec((None, seq, COL_TILE), lambda b, q: (b, 0, q)),
        out_shape=jax.ShapeDtypeStruct((bsz, seq, d_conv), BF16),
        scratch_shapes=[pltpu.VMEM((seq + 2 * HALO, COL_TILE), F32),
                        pltpu.VMEM((seq, COL_TILE), F32)],
        compiler_params=params,
        name="inproj_conv",
    )(h, w_conv, conv_w)
    return rkvl, oconv


def _prep_body(r_ref, k_ref, v_ref, l_ref, vec_ref, wup_ref, aup_ref, gup_ref, ones_ref,
               out_ref, lw_ref, *, dr):
    r = r_ref[...].astype(F32)
    k = k_ref[...].astype(F32)
    v = v_ref[...].astype(F32)
    vec = vec_ref[...]
    ones_bd = ones_ref[...]

    def vrow(i):
        return vec[i:i + 1, :]

    t128 = l_ref[:, 0:128]
    lw = jnp.dot(jnp.tanh(t128.astype(F32)).astype(BF16), wup_ref[...],
                 preferred_element_type=F32)
    aa = jnp.dot(t128, aup_ref[...], preferred_element_type=F32)
    lw_f = -LOG_DECAY_SCALE * jax.nn.sigmoid(vrow(V_W0F) + lw[:, :dr])
    lw_b = -LOG_DECAY_SCALE * jax.nn.sigmoid(vrow(V_W0B) + lw[:, dr:])
    a_f = jax.nn.sigmoid(vrow(V_A0F) + aa[:, :dr])
    a_b = jax.nn.sigmoid(vrow(V_A0B) + aa[:, dr:])
    g = jnp.dot(jax.nn.sigmoid(l_ref[:, 256:512].astype(F32)).astype(BF16), gup_ref[...],
                preferred_element_type=F32)

    kkr = k * vrow(V_KK)
    ssq = _head_sum(kkr * kkr, ones_bd)
    kk = kkr / jnp.maximum(jnp.sqrt(ssq), NORM_EPS)
    kd_f = k * (1.0 + (a_f - 1.0) * vrow(V_KAF))
    kd_b = k * (1.0 + (a_b - 1.0) * vrow(V_KAB))
    bonus = _head_sum(r * kd_f * vrow(V_RKF) + r * kd_b * vrow(V_RKB), ones_bd) * v

    def put(i, val):
        out_ref[:, i * dr:(i + 1) * dr] = val.astype(out_ref.dtype)

    put(P_KK, kk)
    put(P_BF, a_f * kk)
    put(P_BB, a_b * kk)
    put(P_KDF, kd_f)
    put(P_KDB, kd_b)
    put(P_G, g)
    put(P_BONUS, bonus)
    lw_ref[:, L_F * dr:(L_F + 1) * dr] = lw_f
    lw_ref[:, L_B * dr:(L_B + 1) * dr] = lw_b


def _prep(rkvl, vecs, wup, aup, gup, ones_bd, *, dr, row_tile):
    bsz, seq = rkvl.shape[0] - 1, rkvl.shape[1]

    def col(c):
        return pl.BlockSpec((None, row_tile, dr), lambda b, i: (b + 1, i, c))

    def full(a):
        return pl.BlockSpec(a.shape, lambda b, i: (0,) * a.ndim)

    return pl.pallas_call(
        functools.partial(_prep_body, dr=dr),
        grid=(bsz, seq // row_tile),
        in_specs=[col(0), col(1), col(2), col(3),
                  full(vecs), full(wup), full(aup), full(gup), full(ones_bd)],
        out_specs=[pl.BlockSpec((None, row_tile, P_NUM * dr), lambda b, i: (b, i, 0)),
                   pl.BlockSpec((None, row_tile, L_NUM * dr), lambda b, i: (b, i, 0))],
        out_shape=[jax.ShapeDtypeStruct((bsz, seq, P_NUM * dr), BF16),
                   jax.ShapeDtypeStruct((bsz, seq, L_NUM * dr), F32)],
        compiler_params=pltpu.CompilerParams(
            dimension_semantics=("arbitrary", "arbitrary"), vmem_limit_bytes=VMEM_LIMIT),
        name="prep",
    )(rkvl, rkvl, rkvl, rkvl, vecs, wup, aup, gup, ones_bd)


def _block_diag(x, lane_head):
    return jnp.concatenate([jnp.where(lane_head == h, x, 0.0) for h in range(QUAD)],
                           axis=0).astype(BF16)


def _decay_factors(lw, *, forward):
    c = CHUNK
    row = lax.broadcasted_iota(jnp.int32, lw.shape, 0)
    cs = lw
    step = 1
    while step < c:
        if forward:
            cs = cs + jnp.where(row >= step, pltpu.roll(cs, step, 0), 0.0)
        else:
            cs = cs + jnp.where(row < c - step, pltpu.roll(cs, c - step, 0), 0.0)
        step *= 2
    cs_end = cs[c - 1:c, :] if forward else cs[0:1, :]
    return dict(e_t=jnp.exp(cs), e_prev=jnp.exp(cs - lw), e_inv=jnp.exp(-cs),
                e_end=jnp.exp(cs_end - cs), w_c=jnp.exp(cs_end))


def _scan_step(dirs, st_ref, n_sub):
    c = CHUNK
    t_idx = lax.broadcasted_iota(jnp.int32, (c, MXU_DIM), 0)
    lane = lax.broadcasted_iota(jnp.int32, (c, MXU_DIM), 1)
    s_idx = lane & (c - 1)
    lane_head = lane >> 6
    eye = (s_idx == t_idx).astype(F32)
    bd_mask = ((lax.broadcasted_iota(jnp.int32, (MXU_DIM, MXU_DIM), 0) >> 6)
               == (lax.broadcasted_iota(jnp.int32, (MXU_DIM, MXU_DIM), 1) >> 6))

    def bd(x):
        return _block_diag(x, lane_head)

    subs = []
    for k in range(n_sub):
        chains = []
        for i, (forward, refs, y_ref) in enumerate(dirs):
            row0 = (k if forward else n_sub - 1 - k) * c
            rows = slice(row0, row0 + c)
            a = {name: ref[rows, :].astype(F32) for name, ref in refs.items()}
            f = _decay_factors(a["lw"], forward=forward)
            if forward:
                strict, incl = s_idx < t_idx, s_idx <= t_idx
            else:
                strict, incl = s_idx > t_idx, s_idx >= t_idx
            kap, rt = a["kk"] * f["e_prev"], a["r"] * f["e_t"]
            kt, bt = a["kd"] * f["e_inv"], a["b"] * f["e_inv"]
            kh, bh = a["kd"] * f["e_end"], a["b"] * f["e_end"]
            n_quad = a["r"].shape[-1] // MXU_DIM
            for q in range(n_quad):
                sl = slice(q * MXU_DIM, (q + 1) * MXU_DIM)
                chains.append(dict(
                    strict=strict, incl=incl, y_ref=y_ref, rows=rows, sl=sl, idx=i * n_quad + q,
                    kr=jnp.concatenate([kap[:, sl], rt[:, sl]], axis=0).astype(BF16),
                    kt=kt[:, sl], bt=bt[:, sl], v=a["v"][:, sl],
                    w_col=jnp.broadcast_to(f["w_c"][:, sl], (MXU_DIM // 2, MXU_DIM)).T,
                    lhs_t=jnp.concatenate([kh[:, sl], bh[:, sl]], axis=0).T.astype(BF16)))
        subs.append(chains)
    every = [ch for chains in subs for ch in chains]

    for ch in every:
        g_b = _dot_nt(ch["kr"], bd(ch["bt"]))
        g_k = _dot_nt(ch["kr"], bd(ch["kt"]))
        ch["arb"] = jnp.where(ch["incl"], g_b[c:], 0.0)
        ch["aa"] = jnp.concatenate([jnp.where(ch["strict"], g_k[:c], 0.0),
                                    jnp.where(ch["incl"], g_k[c:], 0.0)], axis=0)
        ch["x"] = -jnp.where(ch["strict"], g_b[:c], 0.0)
        ch["t"] = eye + ch["x"]
    for ch in every:
        ch["av"] = _dot(ch["aa"], bd(ch["v"]))

    n_factors = CHUNK.bit_length() - 1
    for ch in every:
        ch["x"] = _dot(ch["x"], bd(ch["x"]))
    for _ in range(n_factors - 2):
        for ch in every:
            rr = _dot(jnp.concatenate([ch["x"], ch["t"]], axis=0), bd(ch["x"]))
            ch["x"] = rr[:c]
            ch["t"] = ch["t"] + rr[c:]
    for ch in every:
        ch["t"] = ch["t"] + _dot(ch["t"], bd(ch["x"]))

    for chains in subs:
        for ch in chains:
            ch["m"] = st_ref[ch["idx"]]
            ch["krm"] = _dot(ch["kr"], ch["m"])
        for ch in chains:
            ch["sa"] = _dot(ch["t"], bd(ch["krm"][:c] + ch["av"][:c]))
        for ch in chains:
            sa = ch["sa"]
            y = ch["krm"][c:] + ch["av"][c:] - _dot(ch["arb"], bd(sa))
            ch["y_ref"][ch["rows"], ch["sl"]] = y.astype(ch["y_ref"].dtype)
            upd = _dot(ch["lhs_t"], jnp.concatenate([ch["v"], -sa], axis=0))
            decay = jnp.concatenate([ch["w_col"], ch["w_col"]], axis=1)
            st_ref[ch["idx"]] = decay * ch["m"] + jnp.where(bd_mask, upd, 0.0)


def _scan_body(rf, vf, kkf, bf, kdf, lwf, rb, vb, kkb, bb, kdb, lwb, yf_ref, yb_ref, st_ref,
               *, n_sub):
    @pl.when(pl.program_id(1) == 0)
    def _():
        st_ref[...] = jnp.zeros_like(st_ref)

    def operands(r, v, kk, b, kd, lw):
        return dict(r=r, v=v, kk=kk, b=b, kd=kd, lw=lw)

    _scan_step([(True, operands(rf, vf, kkf, bf, kdf, lwf), yf_ref),
                (False, operands(rb, vb, kkb, bb, kdb, lwb), yb_ref)], st_ref, n_sub)


def _scan(rkvl, prep, logw, *, dr):
    bsz, seq, _ = prep.shape
    rows = SCAN_SUB * CHUNK
    assert seq % rows == 0
    ns = seq // rows
    n_quad = dr // MXU_DIM

    def fwd(col, slot0=0):
        return pl.BlockSpec((None, rows, dr), lambda b, c: (b + slot0, c, col))

    def bwd(col, slot0=0):
        return pl.BlockSpec((None, rows, dr), lambda b, c: (b + slot0, ns - 1 - c, col))

    return pl.pallas_call(
        functools.partial(_scan_body, n_sub=SCAN_SUB),
        grid=(bsz, ns),
        in_specs=[fwd(0, 1), fwd(2, 1), fwd(P_KK), fwd(P_BF), fwd(P_KDF), fwd(L_F),
                  bwd(0, 1), bwd(2, 1), bwd(P_KK), bwd(P_BB), bwd(P_KDB), bwd(L_B)],
        out_specs=[pl.BlockSpec((None, rows, dr), lambda b, c: (b, c, 0)),
                   pl.BlockSpec((None, rows, dr), lambda b, c: (b, ns - 1 - c, 0))],
        out_shape=[jax.ShapeDtypeStruct((bsz, seq, dr), BF16)] * 2,
        scratch_shapes=[pltpu.VMEM((2 * n_quad, MXU_DIM, MXU_DIM), F32)],
        compiler_params=pltpu.CompilerParams(
            dimension_semantics=("arbitrary", "arbitrary"), vmem_limit_bytes=VMEM_LIMIT),
        name="scan",
    )(rkvl, rkvl, prep, prep, prep, logw, rkvl, rkvl, prep, prep, prep, logw)


def _post_ffn_body(yf_ref, yb_ref, g_ref, bonus_ref, oconv_ref, x_ref, wout_ref, gn_ref, ones_ref,
                   n2_ref, wg_ref, wu_ref, wd_ref, nf_ref, out_ref, *, dr, final_norm):
    ones_bd = ones_ref[...]
    y = yf_ref[...].astype(F32) + yb_ref[...].astype(F32)
    mean = jnp.dot(y.astype(BF16), ones_bd, preferred_element_type=F32) * (1.0 / HEAD)
    d = y - mean
    var = jnp.dot((d * d).astype(BF16), ones_bd, preferred_element_type=F32) * (1.0 / HEAD)
    yn = (d * lax.rsqrt(var + GN_EPS) * gn_ref[0:1, :] + gn_ref[1:2, :]
          + bonus_ref[...].astype(F32))
    o = (yn * g_ref[...].astype(F32)).astype(BF16)
    x1 = (x_ref[...]
          + jnp.dot(o, wout_ref[0:dr, :], preferred_element_type=F32)
          + jnp.dot(oconv_ref[...], wout_ref[dr:, :], preferred_element_type=F32))

    h = _rmsnorm(x1, n2_ref[...]).astype(BF16)
    a = jnp.dot(h, wg_ref[...], preferred_element_type=F32)
    u = jnp.dot(h, wu_ref[...], preferred_element_type=F32)
    z = (a * jax.nn.sigmoid(a) * u).astype(BF16)
    x2 = x1 + jnp.dot(z, wd_ref[...], preferred_element_type=F32)
    out_ref[...] = _rmsnorm(x2, nf_ref[...]) if final_norm else x2


def _post_ffn(yf, yb, prep, oconv, x, wout, gn, ones_bd, n2, wg, wu, wd, nf, *, dr, final_norm,
              row_tile):
    n, d = x.shape

    def rows(width, col=0):
        return pl.BlockSpec((row_tile, width), lambda i: (i, col))

    def full(a):
        return pl.BlockSpec(a.shape, lambda i: (0,) * a.ndim, pipeline_mode=pl.Buffered(1))

    return pl.pallas_call(
        functools.partial(_post_ffn_body, dr=dr, final_norm=final_norm),
        grid=(n // row_tile,),
        in_specs=[rows(dr), rows(dr), rows(dr, P_G), rows(dr, P_BONUS), rows(oconv.shape[-1]),
                  rows(d), full(wout), full(gn), full(ones_bd),
                  full(n2), full(wg), full(wu), full(wd), full(nf)],
        out_specs=rows(d),
        out_shape=jax.ShapeDtypeStruct((n, d), F32),
        compiler_params=pltpu.CompilerParams(
            dimension_semantics=("arbitrary",), vmem_limit_bytes=VMEM_LIMIT),
        name="post_ffn",
    )(yf, yb, prep, prep, oconv, x, wout, gn, ones_bd, n2, wg, wu, wd, nf)


def _pad_cols(a, width):
    return jnp.pad(a, ((0, 0), (0, width - a.shape[1])))


def _pad_rows(a, height):
    return jnp.pad(a, ((0, height - a.shape[0]), (0, 0)))


def kernel(x, norm1_w, w_in, mu_shift, w_up_f, w0_f, w_up_b, w0_b, a_up_f, a0_f, a_up_b, a0_b,
           g_up, k_k, k_a_f, k_a_b, r_k_f, r_k_b, gn_w, gn_b, conv_w, w_out, norm2_w, w_gate,
           w_up, w_down, norm_f_w):
    bsz, seq, d = x.shape
    depth = w_in.shape[0]
    dr = w0_f.shape[-1]
    n_dec, n_aaa, n_gate = w_up_f.shape[1], a_up_f.shape[1], g_up.shape[1]
    d_conv = conv_w.shape[-1]
    assert dr % MXU_DIM == 0 and d_conv % COL_TILE == 0 and seq % CHUNK == 0
    assert n_dec + n_aaa <= 128 and n_gate <= 256
    o_xw = 3 * dr
    o_xa = o_xw + n_dec
    o_xg = o_xa + n_aaa
    o_conv = o_xg + n_gate

    head_id = jnp.arange(dr) // HEAD
    ones_bd = (head_id[:, None] == head_id[None, :]).astype(BF16)

    for l in range(depth):
        w = w_in[l]
        gb, gc, hh = (w[:, o_conv + i * d_conv:o_conv + (i + 1) * d_conv] for i in range(3))
        conv_cols = []
        for t in range(d_conv // COL_TILE):
            cs = slice(t * COL_TILE, (t + 1) * COL_TILE)
            conv_cols += [gb[:, cs], gc[:, cs], hh[:, cs]]
        w_shift = jnp.concatenate(
            [w[:, :o_xw], _pad_cols(w[:, o_xw:o_xg], COL_TILE), _pad_cols(w[:, o_xg:o_conv], COL_TILE)],
            axis=1).astype(BF16)
        w_conv = jnp.concatenate(conv_cols, axis=1).astype(BF16)
        mu = mu_shift[l][None, :]
        mu_p = jnp.concatenate(
            [mu[:, :o_xw], _pad_cols(mu[:, o_xw:o_xg], COL_TILE), _pad_cols(mu[:, o_xg:o_conv], COL_TILE)],
            axis=1)

        rkvl, oconv = _inproj(x, norm1_w[l][None, :], w_shift, mu_p, w_conv, conv_w[l])

        vecs = jnp.stack([w0_f[l], w0_b[l], a0_f[l], a0_b[l], k_k[l], k_a_f[l], k_a_b[l],
                          r_k_f[l].reshape(dr), r_k_b[l].reshape(dr)], axis=0)
        vecs = _pad_rows(vecs, 16)
        wup = _pad_rows(jnp.concatenate([w_up_f[l], w_up_b[l]], axis=1), 128).astype(BF16)
        aup = jnp.concatenate([a_up_f[l], a_up_b[l]], axis=1)
        aup = jnp.pad(aup, ((n_dec, 128 - n_dec - n_aaa), (0, 0))).astype(BF16)
        gup = _pad_rows(g_up[l], 256).astype(BF16)
        prep, logw = _prep(rkvl, vecs, wup, aup, gup, ones_bd, dr=dr, row_tile=1024)

        yf, yb = _scan(rkvl, prep, logw, dr=dr)

        gn = jnp.stack([gn_w[l], gn_b[l]], axis=0)
        n_tok = bsz * seq
        x = _post_ffn(yf.reshape(n_tok, dr), yb.reshape(n_tok, dr), prep.reshape(n_tok, P_NUM * dr),
                      oconv.reshape(n_tok, d_conv), x.reshape(n_tok, d), w_out[l].astype(BF16), gn,
                      ones_bd, norm2_w[l][None, :], w_gate[l].astype(BF16), w_up[l].astype(BF16),
                      w_down[l].astype(BF16), norm_f_w[None, :], dr=dr,
                      final_norm=(l == depth - 1), row_tile=512).reshape(bsz, seq, d)
    return x
```

```python
import functools

import jax
import jax.numpy as jnp
from jax import lax
from jax.experimental import pallas as pl
from jax.experimental.pallas import tpu as pltpu

F32 = jnp.float32
BF16 = jnp.bfloat16

HEAD = 64
CHUNK = 64
MXU_DIM = 256
QUAD = MXU_DIM // CHUNK
SCAN_SUB = 8
COL_TILE = 256
ROW_CHUNK = 512
HALO = 8
LOG_DECAY_SCALE = 0.606531
RMS_EPS = 1e-6
GN_EPS = 64e-5
NORM_EPS = 1e-12
VMEM_LIMIT = 56 * 1024 * 1024

P_KK, P_BF, P_BB, P_KDF, P_KDB, P_G, P_BONUS, P_NUM = range(8)
L_F, L_B, L_NUM = range(3)
V_W0F, V_W0B, V_A0F, V_A0B, V_KK, V_KAF, V_KAB, V_RKF, V_RKB, V_NUM = range(10)


def _dot(a, b):
    return jnp.dot(a.astype(BF16), b.astype(BF16), preferred_element_type=F32)


def _dot_nt(a, b):
    return lax.dot_general(a.astype(BF16), b.astype(BF16), (((1,), (1,)), ((), ())),
                           preferred_element_type=F32)


def _rmsnorm(x, w):
    ms = jnp.mean(x * x, axis=-1, keepdims=True)
    return x * lax.rsqrt(ms + RMS_EPS) * w


def _zero_halo(s_ref, seq):
    z = jnp.zeros((HALO, s_ref.shape[1]), s_ref.dtype)
    s_ref[0:HALO, :] = z
    s_ref[HALO + seq:2 * HALO + seq, :] = z


def _windows(s_ref, c0, rows):
    width = s_ref.shape[-1]
    win = s_ref[c0:c0 + rows + 2 * HALO, :].reshape(rows // HALO + 2, HALO, width)
    down = pltpu.roll(win, 1, 1)
    up = pltpu.roll(win, HALO - 1, 1)
    sub = lax.broadcasted_iota(jnp.int32, (1, HALO, width), 1)
    prv = jnp.where(sub == 0, down[:-2], down[1:-1])
    nxt = jnp.where(sub == HALO - 1, up[2:], up[1:-1])
    return tuple(a.reshape(rows, width) for a in (win[1:-1], prv, nxt))


def _shift_body(x_ref, n1_ref, w_ref, mu_ref, out_ref, hout_ref, h_s, p_s, *, seq):
    bb = pl.program_id(0)
    j = pl.program_id(1)
    new = lax.rem(bb, 2)
    old = 1 - new

    @pl.when((bb == 0) & (j == 0))
    def _():
        _zero_halo(p_s, seq)
        h_s[1] = jnp.zeros(h_s.shape[1:], h_s.dtype)

    n_rows = x_ref.shape[0]
    h = _rmsnorm(x_ref[...], n1_ref[...]).astype(BF16)
    h_s[new, pl.ds(pl.multiple_of(j * n_rows, n_rows), n_rows), :] = h
    hout_ref[...] = h

    w = w_ref[...]
    mu = mu_ref[...]
    keep, mix = 1.0 - mu, 0.5 * mu

    def shift(c0):
        cur, prv, nxt = _windows(p_s, c0, ROW_CHUNK)
        out_ref[c0:c0 + ROW_CHUNK, :] = (keep * cur + mix * (prv + nxt)).astype(out_ref.dtype)

    for c0 in range(0, seq, ROW_CHUNK):
        p_s[HALO + c0:HALO + c0 + ROW_CHUNK, :] = jnp.dot(
            h_s[old, c0:c0 + ROW_CHUNK, :], w, preferred_element_type=F32)
        if c0:
            shift(c0 - ROW_CHUNK)
    shift(seq - ROW_CHUNK)


def _conv_body(h_ref, w_ref, cw_ref, out_ref, u_s, gb_s, *, seq):
    @pl.when(pl.program_id(1) == 0)
    def _():
        _zero_halo(u_s, seq)

    w = w_ref[...]
    cw = cw_ref[...]

    def conv(c0):
        cur, prv, nxt = _windows(u_s, c0, ROW_CHUNK)
        y = cw[0:1, :] * prv + cw[1:2, :] * cur + cw[2:3, :] * nxt
        out_ref[c0:c0 + ROW_CHUNK, :] = (gb_s[c0:c0 + ROW_CHUNK, :] * y).astype(out_ref.dtype)

    for c0 in range(0, seq, ROW_CHUNK):
        p = jnp.dot(h_ref[c0:c0 + ROW_CHUNK, :], w, preferred_element_type=F32)
        gb_s[c0:c0 + ROW_CHUNK, :] = p[:, :COL_TILE]
        u_s[HALO + c0:HALO + c0 + ROW_CHUNK, :] = p[:, COL_TILE:2 * COL_TILE] * p[:, 2 * COL_TILE:]
        if c0:
            conv(c0 - ROW_CHUNK)
    conv(seq - ROW_CHUNK)


def _inproj(x, n1, w_shift, mu_p, w_conv, conv_w):
    bsz, seq, d = x.shape
    assert seq % ROW_CHUNK == 0
    params = pltpu.CompilerParams(
        dimension_semantics=("arbitrary", "arbitrary"), vmem_limit_bytes=VMEM_LIMIT)
    n_tiles = w_shift.shape[1] // COL_TILE
    assert seq % n_tiles == 0
    norm_rows = seq // n_tiles

    def nxt(b):
        return jnp.minimum(b, bsz - 1)

    rkvl, h = pl.pallas_call(
        functools.partial(_shift_body, seq=seq),
        grid=(bsz + 1, n_tiles),
        in_specs=[
            pl.BlockSpec((None, norm_rows, d), lambda b, j: (nxt(b), j, 0)),
            pl.BlockSpec((1, d), lambda b, j: (0, 0)),
            pl.BlockSpec((d, COL_TILE), lambda b, j: (0, j)),
            pl.BlockSpec((1, COL_TILE), lambda b, j: (0, j)),
        ],
        out_specs=[
            pl.BlockSpec((None, seq, COL_TILE), lambda b, j: (b, 0, j)),
            pl.BlockSpec((None, norm_rows, d), lambda b, j: (b, j, 0)),
        ],
        out_shape=[
            jax.ShapeDtypeStruct((bsz + 1, seq, w_shift.shape[1]), BF16),
            jax.ShapeDtypeStruct((bsz + 1, seq, d), BF16),
        ],
        scratch_shapes=[pltpu.VMEM((2, seq, d), BF16),
                        pltpu.VMEM((seq + 2 * HALO, COL_TILE), F32)],
        compiler_params=params,
        name="inproj_shift",
    )(x, n1, w_shift, mu_p)
    d_conv = conv_w.shape[-1]
    oconv = pl.pallas_call(
        functools.partial(_conv_body, seq=seq),
        grid=(bsz, d_conv // COL_TILE),
        in_specs=[
            pl.BlockSpec((None, seq, d), lambda b, q: (b, 0, 0)),
            pl.BlockSpec((d, 3 * COL_TILE), lambda b, q: (0, q)),
            pl.BlockSpec((3, COL_TILE), lambda b, q: (0, q)),
        ],
        out_specs=pl.BlockSpec((None, seq, COL_TILE), lambda b, q: (b, 0, q)),
        out_shape=jax.ShapeDtypeStruct((bsz, seq, d_conv), BF16),
        scratch_shapes=[pltpu.VMEM((seq + 2 * HALO, COL_TILE), F32),
                        pltpu.VMEM((seq, COL_TILE), F32)],
        compiler_params=params,
        name="inproj_conv",
    )(h, w_conv, conv_w)
    return rkvl, oconv


def _prep_body(r_ref, k_ref, v_ref, l_ref, vec_ref, wup_ref, aup_ref, gup_ref, ones_ref,
               out_ref, lw_ref, *, dr):
    r = r_ref[...].astype(F32)
    k = k_ref[...].astype(F32)
    v = v_ref[...].astype(F32)
    vec = vec_ref[...]
    ones_bd = ones_ref[...]

    def vrow(i):
        return vec[i:i + 1, :]

    t128 = l_ref[:, 0:128]
    lw = jnp.dot(jnp.tanh(t128.astype(F32)).astype(BF16), wup_ref[...],
                 preferred_element_type=F32)
    aa = jnp.dot(t128, aup_ref[...], preferred_element_type=F32)
    lw_f = -LOG_DECAY_SCALE * jax.nn.sigmoid(vrow(V_W0F) + lw[:, :dr])
    lw_b = -LOG_DECAY_SCALE * jax.nn.sigmoid(vrow(V_W0B) + lw[:, dr:])
    a_f = jax.nn.sigmoid(vrow(V_A0F) + aa[:, :dr])
    a_b = jax.nn.sigmoid(vrow(V_A0B) + aa[:, dr:])
    g = jnp.dot(jax.nn.sigmoid(l_ref[:, 256:512].astype(F32)).astype(BF16), gup_ref[...],
                preferred_element_type=F32)

    def head_sum(t):
        return jnp.dot(t.astype(BF16), ones_bd, preferred_element_type=F32)

    kkr = k * vrow(V_KK)
    kk = kkr * lax.rsqrt(jnp.maximum(head_sum(kkr * kkr), NORM_EPS * NORM_EPS))
    kd_f = k * (1.0 + (a_f - 1.0) * vrow(V_KAF))
    kd_b = k * (1.0 + (a_b - 1.0) * vrow(V_KAB))
    bonus = head_sum(r * kd_f * vrow(V_RKF) + r * kd_b * vrow(V_RKB)) * v

    def put(i, val):
        out_ref[:, i * dr:(i + 1) * dr] = val.astype(out_ref.dtype)

    put(P_KK, kk)
    put(P_BF, a_f * kk)
    put(P_BB, a_b * kk)
    put(P_KDF, kd_f)
    put(P_KDB, kd_b)
    put(P_G, g)
    put(P_BONUS, bonus)
    lw_ref[:, L_F * dr:(L_F + 1) * dr] = lw_f
    lw_ref[:, L_B * dr:(L_B + 1) * dr] = lw_b


def _prep(rkvl, vecs, wup, aup, gup, ones_bd, *, dr, row_tile):
    bsz, seq = rkvl.shape[0] - 1, rkvl.shape[1]

    def col(c):
        return pl.BlockSpec((None, row_tile, dr), lambda b, i: (b + 1, i, c))

    def full(a):
        return pl.BlockSpec(a.shape, lambda b, i: (0,) * a.ndim)

    return pl.pallas_call(
        functools.partial(_prep_body, dr=dr),
        grid=(bsz, seq // row_tile),
        in_specs=[col(0), col(1), col(2), col(3),
                  full(vecs), full(wup), full(aup), full(gup), full(ones_bd)],
        out_specs=[pl.BlockSpec((None, row_tile, P_NUM * dr), lambda b, i: (b, i, 0)),
                   pl.BlockSpec((None, row_tile, L_NUM * dr), lambda b, i: (b, i, 0))],
        out_shape=[jax.ShapeDtypeStruct((bsz, seq, P_NUM * dr), BF16),
                   jax.ShapeDtypeStruct((bsz, seq, L_NUM * dr), F32)],
        compiler_params=pltpu.CompilerParams(
            dimension_semantics=("arbitrary", "arbitrary"), vmem_limit_bytes=VMEM_LIMIT),
        name="prep",
    )(rkvl, rkvl, rkvl, rkvl, vecs, wup, aup, gup, ones_bd)


def _block_diag(x, lane_head):
    return jnp.concatenate([jnp.where(lane_head == h, x, 0.0) for h in range(QUAD)],
                           axis=0).astype(BF16)


def _cumsum_time(x, *, forward):
    rows, n = x.shape
    tiles = rows // HALO
    x = x.reshape(tiles, HALO, n)
    sub = lax.broadcasted_iota(jnp.int32, (1, HALO, n), 1)
    step = 1
    while step < HALO:
        if forward:
            x = x + jnp.where(sub >= step, pltpu.roll(x, step, 1), 0.0)
        else:
            x = x + jnp.where(sub < HALO - step, pltpu.roll(x, HALO - step, 1), 0.0)
        step *= 2
    edge = HALO - 1 if forward else 0
    parts = [None] * tiles
    carry = None
    for i in (range(tiles) if forward else reversed(range(tiles))):
        parts[i] = x[i:i + 1] if carry is None else x[i:i + 1] + carry
        total = x[i:i + 1, edge:edge + 1, :]
        carry = total if carry is None else carry + total
    return jnp.concatenate(parts, axis=0).reshape(rows, n)


def _decay_factors(lw, *, forward):
    c = CHUNK
    cs = _cumsum_time(lw, forward=forward)
    cs_end = cs[c - 1:c, :] if forward else cs[0:1, :]
    return dict(e_t=jnp.exp(cs), e_prev=jnp.exp(cs - lw), e_inv=jnp.exp(-cs),
                e_end=jnp.exp(cs_end - cs), w_c=jnp.exp(cs_end))


def _scan_step(dirs, st_ref, n_sub):
    c = CHUNK
    t_idx = lax.broadcasted_iota(jnp.int32, (c, MXU_DIM), 0)
    lane = lax.broadcasted_iota(jnp.int32, (c, MXU_DIM), 1)
    s_idx = lane & (c - 1)
    lane_head = lane >> 6
    eye = (s_idx == t_idx).astype(F32)
    bd_mask = ((lax.broadcasted_iota(jnp.int32, (MXU_DIM, MXU_DIM), 0) >> 6)
               == (lax.broadcasted_iota(jnp.int32, (MXU_DIM, MXU_DIM), 1) >> 6))

    def bd(x):
        return _block_diag(x, lane_head)

    subs = []
    for k in range(n_sub):
        chains = []
        for i, (forward, refs, y_ref) in enumerate(dirs):
            row0 = (k if forward else n_sub - 1 - k) * c
            rows = slice(row0, row0 + c)
            a = {name: ref[rows, :].astype(F32) for name, ref in refs.items()}
            f = _decay_factors(a["lw"], forward=forward)
            if forward:
                strict, incl = s_idx < t_idx, s_idx <= t_idx
            else:
                strict, incl = s_idx > t_idx, s_idx >= t_idx
            kap, rt = a["kk"] * f["e_prev"], a["r"] * f["e_t"]
            kt, bt = a["kd"] * f["e_inv"], a["b"] * f["e_inv"]
            kh, bh = a["kd"] * f["e_end"], a["b"] * f["e_end"]
            n_quad = a["r"].shape[-1] // MXU_DIM
            for q in range(n_quad):
                sl = slice(q * MXU_DIM, (q + 1) * MXU_DIM)
                chains.append(dict(
                    strict=strict, incl=incl, y_ref=y_ref, rows=rows, sl=sl, idx=i * n_quad + q,
                    kr=jnp.concatenate([kap[:, sl], rt[:, sl]], axis=0).astype(BF16),
                    kt=kt[:, sl], bt=bt[:, sl], v=a["v"][:, sl],
                    w_col=jnp.broadcast_to(f["w_c"][:, sl], (MXU_DIM // 2, MXU_DIM)).T,
                    lhs_t=jnp.concatenate([kh[:, sl], bh[:, sl]], axis=0).T.astype(BF16)))
        subs.append(chains)
    every = [ch for chains in subs for ch in chains]

    for ch in every:
        g_b = _dot_nt(ch["kr"], bd(ch["bt"]))
        g_k = _dot_nt(ch["kr"], bd(ch["kt"]))
        ch["arb"] = jnp.where(ch["incl"], g_b[c:], 0.0)
        ch["aa"] = jnp.concatenate([jnp.where(ch["strict"], g_k[:c], 0.0),
                                    jnp.where(ch["incl"], g_k[c:], 0.0)], axis=0)
        ch["x"] = -jnp.where(ch["strict"], g_b[:c], 0.0)
        ch["t"] = eye + ch["x"]
    for ch in every:
        ch["av"] = _dot(ch["aa"], bd(ch["v"]))

    n_factors = CHUNK.bit_length() - 1
    for ch in every:
        ch["x"] = _dot(ch["x"], bd(ch["x"]))
    for _ in range(n_factors - 2):
        for ch in every:
            rr = _dot(jnp.concatenate([ch["x"], ch["t"]], axis=0), bd(ch["x"]))
            ch["x"] = rr[:c]
            ch["t"] = ch["t"] + rr[c:]
    for ch in every:
        ch["t"] = ch["t"] + _dot(ch["t"], bd(ch["x"]))

    for chains in subs:
        for ch in chains:
            ch["m"] = st_ref[ch["idx"]]
            ch["krm"] = _dot(ch["kr"], ch["m"])
        for ch in chains:
            ch["sa"] = _dot(ch["t"], bd(ch["krm"][:c] + ch["av"][:c]))
        for ch in chains:
            sa = ch["sa"]
            y = ch["krm"][c:] + ch["av"][c:] - _dot(ch["arb"], bd(sa))
            ch["y_ref"][ch["rows"], ch["sl"]] = y.astype(ch["y_ref"].dtype)
            upd = _dot(ch["lhs_t"], jnp.concatenate([ch["v"], -sa], axis=0))
            decay = jnp.concatenate([ch["w_col"], ch["w_col"]], axis=1)
            st_ref[ch["idx"]] = decay * ch["m"] + jnp.where(bd_mask, upd, 0.0)


def _scan_body(rf, vf, kkf, bf, kdf, lwf, rb, vb, kkb, bb, kdb, lwb, yf_ref, yb_ref, st_ref,
               *, n_sub):
    @pl.when(pl.program_id(1) == 0)
    def _():
        st_ref[...] = jnp.zeros_like(st_ref)

    def operands(r, v, kk, b, kd, lw):
        return dict(r=r, v=v, kk=kk, b=b, kd=kd, lw=lw)

    _scan_step([(True, operands(rf, vf, kkf, bf, kdf, lwf), yf_ref),
                (False, operands(rb, vb, kkb, bb, kdb, lwb), yb_ref)], st_ref, n_sub)


def _scan(rkvl, prep, logw, *, dr):
    bsz, seq, _ = prep.shape
    rows = SCAN_SUB * CHUNK
    assert seq % rows == 0
    ns = seq // rows
    n_quad = dr // MXU_DIM

    def fwd(col, slot0=0):
        return pl.BlockSpec((None, rows, dr), lambda b, c: (b + slot0, c, col))

    def bwd(col, slot0=0):
        return pl.BlockSpec((None, rows, dr), lambda b, c: (b + slot0, ns - 1 - c, col))

    return pl.pallas_call(
        functools.partial(_scan_body, n_sub=SCAN_SUB),
        grid=(bsz, ns),
        in_specs=[fwd(0, 1), fwd(2, 1), fwd(P_KK), fwd(P_BF), fwd(P_KDF), fwd(L_F),
                  bwd(0, 1), bwd(2, 1), bwd(P_KK), bwd(P_BB), bwd(P_KDB), bwd(L_B)],
        out_specs=[pl.BlockSpec((None, rows, dr), lambda b, c: (b, c, 0)),
                   pl.BlockSpec((None, rows, dr), lambda b, c: (b, ns - 1 - c, 0))],
        out_shape=[jax.ShapeDtypeStruct((bsz, seq, dr), BF16)] * 2,
        scratch_shapes=[pltpu.VMEM((2 * n_quad, MXU_DIM, MXU_DIM), F32)],
        compiler_params=pltpu.CompilerParams(
            dimension_semantics=("arbitrary", "arbitrary"), vmem_limit_bytes=VMEM_LIMIT),
        name="scan",
    )(rkvl, rkvl, prep, prep, prep, logw, rkvl, rkvl, prep, prep, prep, logw)


def _post_ffn_body(yf_ref, yb_ref, g_ref, bonus_ref, oconv_ref, x_ref, wout_ref, gn_ref, ones_ref,
                   n2_ref, wg_ref, wu_ref, wd_ref, nf_ref, out_ref, *, dr, final_norm):
    ones_bd = ones_ref[...]
    y = yf_ref[...].astype(F32) + yb_ref[...].astype(F32)
    mean = jnp.dot(y.astype(BF16), ones_bd, preferred_element_type=F32) * (1.0 / HEAD)
    d = y - mean
    var = jnp.dot((d * d).astype(BF16), ones_bd, preferred_element_type=F32) * (1.0 / HEAD)
    yn = (d * lax.rsqrt(var + GN_EPS) * gn_ref[0:1, :] + gn_ref[1:2, :]
          + bonus_ref[...].astype(F32))
    o = (yn * g_ref[...].astype(F32)).astype(BF16)
    x1 = (x_ref[...]
          + jnp.dot(o, wout_ref[0:dr, :], preferred_element_type=F32)
          + jnp.dot(oconv_ref[...], wout_ref[dr:, :], preferred_element_type=F32))

    h = _rmsnorm(x1, n2_ref[...]).astype(BF16)
    a = jnp.dot(h, wg_ref[...], preferred_element_type=F32)
    u = jnp.dot(h, wu_ref[...], preferred_element_type=F32)
    z = (a * jax.nn.sigmoid(a) * u).astype(BF16)
    x2 = x1 + jnp.dot(z, wd_ref[...], preferred_element_type=F32)
    out_ref[...] = _rmsnorm(x2, nf_ref[...]) if final_norm else x2


def _post_ffn(yf, yb, prep, oconv, x, wout, gn, ones_bd, n2, wg, wu, wd, nf, *, dr, final_norm,
              row_tile):
    n, d = x.shape

    def rows(width, col=0):
        return pl.BlockSpec((row_tile, width), lambda i: (i, col))

    def full(a):
        return pl.BlockSpec(a.shape, lambda i: (0,) * a.ndim, pipeline_mode=pl.Buffered(1))

    return pl.pallas_call(
        functools.partial(_post_ffn_body, dr=dr, final_norm=final_norm),
        grid=(n // row_tile,),
        in_specs=[rows(dr), rows(dr), rows(dr, P_G), rows(dr, P_BONUS), rows(oconv.shape[-1]),
                  rows(d), full(wout), full(gn), full(ones_bd),
                  full(n2), full(wg), full(wu), full(wd), full(nf)],
        out_specs=rows(d),
        out_shape=jax.ShapeDtypeStruct((n, d), F32),
        compiler_params=pltpu.CompilerParams(
            dimension_semantics=("arbitrary",), vmem_limit_bytes=VMEM_LIMIT),
        name="post_ffn",
    )(yf, yb, prep, prep, oconv, x, wout, gn, ones_bd, n2, wg, wu, wd, nf)


def _pad_cols(a, width):
    return jnp.pad(a, ((0, 0), (0, width - a.shape[1])))


def _pad_rows(a, height):
    return jnp.pad(a, ((0, height - a.shape[0]), (0, 0)))


def kernel(x, norm1_w, w_in, mu_shift, w_up_f, w0_f, w_up_b, w0_b, a_up_f, a0_f, a_up_b, a0_b,
           g_up, k_k, k_a_f, k_a_b, r_k_f, r_k_b, gn_w, gn_b, conv_w, w_out, norm2_w, w_gate,
           w_up, w_down, norm_f_w):
    bsz, seq, d = x.shape
    depth = w_in.shape[0]
    dr = w0_f.shape[-1]
    n_dec, n_aaa, n_gate = w_up_f.shape[1], a_up_f.shape[1], g_up.shape[1]
    d_conv = conv_w.shape[-1]
    assert dr % MXU_DIM == 0 and d_conv % COL_TILE == 0 and seq % CHUNK == 0
    assert n_dec + n_aaa <= 128 and n_gate <= 256
    o_xw = 3 * dr
    o_xa = o_xw + n_dec
    o_xg = o_xa + n_aaa
    o_conv = o_xg + n_gate

    head_id = jnp.arange(dr) // HEAD
    ones_bd = (head_id[:, None] == head_id[None, :]).astype(BF16)

    for l in range(depth):
        w = w_in[l]
        gb, gc, hh = (w[:, o_conv + i * d_conv:o_conv + (i + 1) * d_conv] for i in range(3))
        conv_cols = []
        for t in range(d_conv // COL_TILE):
            cs = slice(t * COL_TILE, (t + 1) * COL_TILE)
            conv_cols += [gb[:, cs], gc[:, cs], hh[:, cs]]
        w_shift = jnp.concatenate(
            [w[:, :o_xw], _pad_cols(w[:, o_xw:o_xg], COL_TILE), _pad_cols(w[:, o_xg:o_conv], COL_TILE)],
            axis=1).astype(BF16)
        w_conv = jnp.concatenate(conv_cols, axis=1).astype(BF16)
        mu = mu_shift[l][None, :]
        mu_p = jnp.concatenate(
            [mu[:, :o_xw], _pad_cols(mu[:, o_xw:o_xg], COL_TILE), _pad_cols(mu[:, o_xg:o_conv], COL_TILE)],
            axis=1)

        rkvl, oconv = _inproj(x, norm1_w[l][None, :], w_shift, mu_p, w_conv, conv_w[l])

        vecs = jnp.stack([w0_f[l], w0_b[l], a0_f[l], a0_b[l], k_k[l], k_a_f[l], k_a_b[l],
                          r_k_f[l].reshape(dr), r_k_b[l].reshape(dr)], axis=0)
        vecs = _pad_rows(vecs, 16)
        wup = _pad_rows(jnp.concatenate([w_up_f[l], w_up_b[l]], axis=1), 128).astype(BF16)
        aup = jnp.concatenate([a_up_f[l], a_up_b[l]], axis=1)
        aup = jnp.pad(aup, ((n_dec, 128 - n_dec - n_aaa), (0, 0))).astype(BF16)
        gup = _pad_rows(g_up[l], 256).astype(BF16)
        prep, logw = _prep(rkvl, vecs, wup, aup, gup, ones_bd, dr=dr, row_tile=min(1024, seq))

        yf, yb = _scan(rkvl, prep, logw, dr=dr)

        gn = jnp.stack([gn_w[l], gn_b[l]], axis=0)
        n_tok = bsz * seq
        x = _post_ffn(yf.reshape(n_tok, dr), yb.reshape(n_tok, dr), prep.reshape(n_tok, P_NUM * dr),
                      oconv.reshape(n_tok, d_conv), x.reshape(n_tok, d), w_out[l].astype(BF16), gn,
                      ones_bd, norm2_w[l][None, :], w_gate[l].astype(BF16), w_up[l].astype(BF16),
                      w_down[l].astype(BF16), norm_f_w[None, :], dr=dr,
                      final_norm=(l == depth - 1), row_tile=512).reshape(bsz, seq, d)
    return x
```

```python
import functools

import jax
import jax.numpy as jnp
from jax import lax
from jax.experimental import pallas as pl
from jax.experimental.pallas import tpu as pltpu

F32 = jnp.float32
BF16 = jnp.bfloat16

HEAD = 64
CHUNK = 64
MXU_DIM = 256
QUAD = MXU_DIM // CHUNK
SCAN_SUB = 8
COL_TILE = 256
SHIFT_TILE = 512
ROW_CHUNK = 512
HALO = 8
LOG_DECAY_SCALE = 0.606531
RMS_EPS = 1e-6
GN_EPS = 64e-5
NORM_EPS = 1e-12
VMEM_LIMIT = 56 * 1024 * 1024

P_KK, P_BF, P_BB, P_KDF, P_KDB, P_G, P_BONUS, P_NUM = range(8)
L_F, L_B, L_NUM = range(3)
V_W0F, V_W0B, V_A0F, V_A0B, V_KK, V_KAF, V_KAB, V_RKF, V_RKB, V_NUM = range(10)


def _dot(a, b):
    return jnp.dot(a.astype(BF16), b.astype(BF16), preferred_element_type=F32)


def _dot_nt(a, b):
    return lax.dot_general(a.astype(BF16), b.astype(BF16), (((1,), (1,)), ((), ())),
                           preferred_element_type=F32)


def _rmsnorm(x, w):
    ms = jnp.mean(x * x, axis=-1, keepdims=True)
    return x * lax.rsqrt(ms + RMS_EPS) * w


def _zero_halo(s_ref, seq):
    z = jnp.zeros((HALO, s_ref.shape[1]), s_ref.dtype)
    s_ref[0:HALO, :] = z
    s_ref[HALO + seq:2 * HALO + seq, :] = z


def _windows(s_ref, c0, rows):
    width = s_ref.shape[-1]
    win = s_ref[c0:c0 + rows + 2 * HALO, :].reshape(rows // HALO + 2, HALO, width)
    down = pltpu.roll(win, 1, 1)
    up = pltpu.roll(win, HALO - 1, 1)
    sub = lax.broadcasted_iota(jnp.int32, (1, HALO, width), 1)
    prv = jnp.where(sub == 0, down[:-2], down[1:-1])
    nxt = jnp.where(sub == HALO - 1, up[2:], up[1:-1])
    return tuple(a.reshape(rows, width) for a in (win[1:-1], prv, nxt))


def _shift_body(x_ref, n1_ref, w_ref, mu_ref, out_ref, hout_ref, h_s, p_s, *, seq):
    bb = pl.program_id(0)
    j = pl.program_id(1)
    new = lax.rem(bb, 2)
    old = 1 - new

    @pl.when((bb == 0) & (j == 0))
    def _():
        _zero_halo(p_s, seq)
        h_s[1] = jnp.zeros(h_s.shape[1:], h_s.dtype)

    n_rows = x_ref.shape[0]
    h = _rmsnorm(x_ref[...], n1_ref[...]).astype(BF16)
    h_s[new, pl.ds(pl.multiple_of(j * n_rows, n_rows), n_rows), :] = h
    hout_ref[...] = h

    w = w_ref[...]
    mu = mu_ref[...]
    keep, mix = 1.0 - mu, 0.5 * mu

    def shift(c0):
        cur, prv, nxt = _windows(p_s, c0, ROW_CHUNK)
        out_ref[c0:c0 + ROW_CHUNK, :] = (keep * cur + mix * (prv + nxt)).astype(out_ref.dtype)

    for c0 in range(0, seq, ROW_CHUNK):
        p_s[HALO + c0:HALO + c0 + ROW_CHUNK, :] = jnp.dot(
            h_s[old, c0:c0 + ROW_CHUNK, :], w, preferred_element_type=F32)
        if c0:
            shift(c0 - ROW_CHUNK)
    shift(seq - ROW_CHUNK)


def _conv_body(h_ref, w_ref, cw_ref, out_ref, u_s, gb_s, *, seq):
    @pl.when(pl.program_id(1) == 0)
    def _():
        _zero_halo(u_s, seq)

    w = w_ref[...]
    cw = cw_ref[...]

    def conv(c0):
        cur, prv, nxt = _windows(u_s, c0, ROW_CHUNK)
        y = cw[0:1, :] * prv + cw[1:2, :] * cur + cw[2:3, :] * nxt
        out_ref[c0:c0 + ROW_CHUNK, :] = (gb_s[c0:c0 + ROW_CHUNK, :] * y).astype(out_ref.dtype)

    for c0 in range(0, seq, ROW_CHUNK):
        p = jnp.dot(h_ref[c0:c0 + ROW_CHUNK, :], w, preferred_element_type=F32)
        gb_s[c0:c0 + ROW_CHUNK, :] = p[:, :COL_TILE]
        u_s[HALO + c0:HALO + c0 + ROW_CHUNK, :] = p[:, COL_TILE:2 * COL_TILE] * p[:, 2 * COL_TILE:]
        if c0:
            conv(c0 - ROW_CHUNK)
    conv(seq - ROW_CHUNK)


def _inproj(x, n1, w_shift, mu_p, w_conv, conv_w):
    bsz, seq, d = x.shape
    assert seq % ROW_CHUNK == 0
    params = pltpu.CompilerParams(
        dimension_semantics=("arbitrary", "arbitrary"), vmem_limit_bytes=VMEM_LIMIT)
    n_tiles = w_shift.shape[1] // SHIFT_TILE
    assert seq % n_tiles == 0
    norm_rows = seq // n_tiles

    def nxt(b):
        return jnp.minimum(b, bsz - 1)

    rkvl, h = pl.pallas_call(
        functools.partial(_shift_body, seq=seq),
        grid=(bsz + 1, n_tiles),
        in_specs=[
            pl.BlockSpec((None, norm_rows, d), lambda b, j: (nxt(b), j, 0)),
            pl.BlockSpec((1, d), lambda b, j: (0, 0)),
            pl.BlockSpec((d, SHIFT_TILE), lambda b, j: (0, j)),
            pl.BlockSpec((1, SHIFT_TILE), lambda b, j: (0, j)),
        ],
        out_specs=[
            pl.BlockSpec((None, seq, SHIFT_TILE), lambda b, j: (b, 0, j)),
            pl.BlockSpec((None, norm_rows, d), lambda b, j: (b, j, 0)),
        ],
        out_shape=[
            jax.ShapeDtypeStruct((bsz + 1, seq, w_shift.shape[1]), BF16),
            jax.ShapeDtypeStruct((bsz + 1, seq, d), BF16),
        ],
        scratch_shapes=[pltpu.VMEM((2, seq, d), BF16),
                        pltpu.VMEM((seq + 2 * HALO, SHIFT_TILE), F32)],
        compiler_params=params,
        name="inproj_shift",
    )(x, n1, w_shift, mu_p)
    d_conv = conv_w.shape[-1]
    oconv = pl.pallas_call(
        functools.partial(_conv_body, seq=seq),
        grid=(bsz, d_conv // COL_TILE),
        in_specs=[
            pl.BlockSpec((None, seq, d), lambda b, q: (b, 0, 0)),
            pl.BlockSpec((d, 3 * COL_TILE), lambda b, q: (0, q)),
            pl.BlockSpec((3, COL_TILE), lambda b, q: (0, q)),
        ],
        out_specs=pl.BlockSpec((None, seq, COL_TILE), lambda b, q: (b, 0, q)),
        out_shape=jax.ShapeDtypeStruct((bsz, seq, d_conv), BF16),
        scratch_shapes=[pltpu.VMEM((seq + 2 * HALO, COL_TILE), F32),
                        pltpu.VMEM((seq, COL_TILE), F32)],
        compiler_params=params,
        name="inproj_conv",
    )(h, w_conv, conv_w)
    return rkvl, oconv


def _prep_body(r_ref, k_ref, v_ref, l_ref, vec_ref, wup_ref, aup_ref, gup_ref, ones_ref,
               out_ref, lw_ref, *, dr):
    r = r_ref[...].astype(F32)
    k = k_ref[...].astype(F32)
    v = v_ref[...].astype(F32)
    vec = vec_ref[...]
    ones_bd = ones_ref[...]

    def vrow(i):
        return vec[i:i + 1, :]

    t128 = l_ref[:, 0:128]
    lw = jnp.dot(jnp.tanh(t128.astype(F32)).astype(BF16), wup_ref[...],
                 preferred_element_type=F32)
    aa = jnp.dot(t128, aup_ref[...], preferred_element_type=F32)
    lw_f = -LOG_DECAY_SCALE * jax.nn.sigmoid(vrow(V_W0F) + lw[:, :dr])
    lw_b = -LOG_DECAY_SCALE * jax.nn.sigmoid(vrow(V_W0B) + lw[:, dr:])
    a_f = jax.nn.sigmoid(vrow(V_A0F) + aa[:, :dr])
    a_b = jax.nn.sigmoid(vrow(V_A0B) + aa[:, dr:])
    g = jnp.dot(jax.nn.sigmoid(l_ref[:, 256:512].astype(F32)).astype(BF16), gup_ref[...],
                preferred_element_type=F32)

    def head_sum(t):
        return jnp.dot(t.astype(BF16), ones_bd, preferred_element_type=F32)

    kkr = k * vrow(V_KK)
    kk = kkr * lax.rsqrt(jnp.maximum(head_sum(kkr * kkr), NORM_EPS * NORM_EPS))
    kd_f = k * (1.0 + (a_f - 1.0) * vrow(V_KAF))
    kd_b = k * (1.0 + (a_b - 1.0) * vrow(V_KAB))
    bonus = head_sum(r * kd_f * vrow(V_RKF) + r * kd_b * vrow(V_RKB)) * v

    def put(i, val):
        out_ref[:, i * dr:(i + 1) * dr] = val.astype(out_ref.dtype)

    put(P_KK, kk)
    put(P_BF, a_f * kk)
    put(P_BB, a_b * kk)
    put(P_KDF, kd_f)
    put(P_KDB, kd_b)
    put(P_G, g)
    put(P_BONUS, bonus)
    lw_ref[:, L_F * dr:(L_F + 1) * dr] = lw_f
    lw_ref[:, L_B * dr:(L_B + 1) * dr] = lw_b


def _prep(rkvl, vecs, wup, aup, gup, ones_bd, *, dr, row_tile):
    bsz, seq = rkvl.shape[0] - 1, rkvl.shape[1]

    def col(c):
        return pl.BlockSpec((None, row_tile, dr), lambda b, i: (b + 1, i, c))

    def full(a):
        return pl.BlockSpec(a.shape, lambda b, i: (0,) * a.ndim)

    return pl.pallas_call(
        functools.partial(_prep_body, dr=dr),
        grid=(bsz, seq // row_tile),
        in_specs=[col(0), col(1), col(2), col(3),
                  full(vecs), full(wup), full(aup), full(gup), full(ones_bd)],
        out_specs=[pl.BlockSpec((None, row_tile, P_NUM * dr), lambda b, i: (b, i, 0)),
                   pl.BlockSpec((None, row_tile, L_NUM * dr), lambda b, i: (b, i, 0))],
        out_shape=[jax.ShapeDtypeStruct((bsz, seq, P_NUM * dr), BF16),
                   jax.ShapeDtypeStruct((bsz, seq, L_NUM * dr), F32)],
        compiler_params=pltpu.CompilerParams(
            dimension_semantics=("arbitrary", "arbitrary"), vmem_limit_bytes=VMEM_LIMIT),
        name="prep",
    )(rkvl, rkvl, rkvl, rkvl, vecs, wup, aup, gup, ones_bd)


def _block_diag(x, lane_head):
    return jnp.concatenate([jnp.where(lane_head == h, x, 0.0) for h in range(QUAD)],
                           axis=0).astype(BF16)


def _cumsum_time(x, *, forward):
    rows, n = x.shape
    tiles = rows // HALO
    x = x.reshape(tiles, HALO, n)
    sub = lax.broadcasted_iota(jnp.int32, (1, HALO, n), 1)
    step = 1
    while step < HALO:
        if forward:
            x = x + jnp.where(sub >= step, pltpu.roll(x, step, 1), 0.0)
        else:
            x = x + jnp.where(sub < HALO - step, pltpu.roll(x, HALO - step, 1), 0.0)
        step *= 2
    edge = HALO - 1 if forward else 0
    parts = [None] * tiles
    carry = None
    for i in (range(tiles) if forward else reversed(range(tiles))):
        parts[i] = x[i:i + 1] if carry is None else x[i:i + 1] + carry
        total = x[i:i + 1, edge:edge + 1, :]
        carry = total if carry is None else carry + total
    return jnp.concatenate(parts, axis=0).reshape(rows, n)


def _decay_factors(lw, *, forward):
    c = CHUNK
    cs = _cumsum_time(lw, forward=forward)
    cs_end = cs[c - 1:c, :] if forward else cs[0:1, :]
    return dict(e_t=jnp.exp(cs), e_prev=jnp.exp(cs - lw), e_inv=jnp.exp(-cs),
                e_end=jnp.exp(cs_end - cs), w_c=jnp.exp(cs_end))


def _scan_step(dirs, st_ref, n_sub):
    c = CHUNK
    t_idx = lax.broadcasted_iota(jnp.int32, (c, MXU_DIM), 0)
    lane = lax.broadcasted_iota(jnp.int32, (c, MXU_DIM), 1)
    s_idx = lane & (c - 1)
    lane_head = lane >> 6
    eye = (s_idx == t_idx).astype(F32)
    bd_mask = ((lax.broadcasted_iota(jnp.int32, (MXU_DIM, MXU_DIM), 0) >> 6)
               == (lax.broadcasted_iota(jnp.int32, (MXU_DIM, MXU_DIM), 1) >> 6))

    def bd(x):
        return _block_diag(x, lane_head)

    subs = []
    for k in range(n_sub):
        chains = []
        for i, (forward, refs, y_ref) in enumerate(dirs):
            row0 = (k if forward else n_sub - 1 - k) * c
            rows = slice(row0, row0 + c)
            a = {name: ref[rows, :].astype(F32) for name, ref in refs.items()}
            f = _decay_factors(a["lw"], forward=forward)
            if forward:
                strict, incl = s_idx < t_idx, s_idx <= t_idx
            else:
                strict, incl = s_idx > t_idx, s_idx >= t_idx
            kap, rt = a["kk"] * f["e_prev"], a["r"] * f["e_t"]
            kt, bt = a["kd"] * f["e_inv"], a["b"] * f["e_inv"]
            kh, bh = a["kd"] * f["e_end"], a["b"] * f["e_end"]
            n_quad = a["r"].shape[-1] // MXU_DIM
            for q in range(n_quad):
                sl = slice(q * MXU_DIM, (q + 1) * MXU_DIM)
                chains.append(dict(
                    strict=strict, incl=incl, y_ref=y_ref, rows=rows, sl=sl, idx=i * n_quad + q,
                    kr=jnp.concatenate([kap[:, sl], rt[:, sl]], axis=0).astype(BF16),
                    kt=kt[:, sl], bt=bt[:, sl], v=a["v"][:, sl],
                    w_col=jnp.broadcast_to(f["w_c"][:, sl], (MXU_DIM // 2, MXU_DIM)).T,
                    lhs_t=jnp.concatenate([kh[:, sl], bh[:, sl]], axis=0).T.astype(BF16)))
        subs.append(chains)
    every = [ch for chains in subs for ch in chains]

    for ch in every:
        g_b = _dot_nt(ch["kr"], bd(ch["bt"]))
        g_k = _dot_nt(ch["kr"], bd(ch["kt"]))
        ch["arb"] = jnp.where(ch["incl"], g_b[c:], 0.0)
        ch["aa"] = jnp.concatenate([jnp.where(ch["strict"], g_k[:c], 0.0),
                                    jnp.where(ch["incl"], g_k[c:], 0.0)], axis=0)
        ch["x"] = -jnp.where(ch["strict"], g_b[:c], 0.0)
        ch["t"] = eye + ch["x"]
    for ch in every:
        ch["av"] = _dot(ch["aa"], bd(ch["v"]))

    n_factors = CHUNK.bit_length() - 1
    for ch in every:
        ch["x"] = _dot(ch["x"], bd(ch["x"]))
    for _ in range(n_factors - 2):
        for ch in every:
            rr = _dot(jnp.concatenate([ch["x"], ch["t"]], axis=0), bd(ch["x"]))
            ch["x"] = rr[:c]
            ch["t"] = ch["t"] + rr[c:]
    for ch in every:
        ch["t"] = ch["t"] + _dot(ch["t"], bd(ch["x"]))

    for chains in subs:
        for ch in chains:
            ch["m"] = st_ref[ch["idx"]]
            ch["krm"] = _dot(ch["kr"], ch["m"])
        for ch in chains:
            ch["sa"] = _dot(ch["t"], bd(ch["krm"][:c] + ch["av"][:c]))
        for ch in chains:
            sa = ch["sa"]
            y = ch["krm"][c:] + ch["av"][c:] - _dot(ch["arb"], bd(sa))
            ch["y_ref"][ch["rows"], ch["sl"]] = y.astype(ch["y_ref"].dtype)
            upd = _dot(ch["lhs_t"], jnp.concatenate([ch["v"], -sa], axis=0))
            decay = jnp.concatenate([ch["w_col"], ch["w_col"]], axis=1)
            st_ref[ch["idx"]] = decay * ch["m"] + jnp.where(bd_mask, upd, 0.0)


def _scan_body(rf, vf, kkf, bf, kdf, lwf, rb, vb, kkb, bb, kdb, lwb, yf_ref, yb_ref, st_ref,
               *, n_sub):
    @pl.when(pl.program_id(1) == 0)
    def _():
        st_ref[...] = jnp.zeros_like(st_ref)

    def operands(r, v, kk, b, kd, lw):
        return dict(r=r, v=v, kk=kk, b=b, kd=kd, lw=lw)

    _scan_step([(True, operands(rf, vf, kkf, bf, kdf, lwf), yf_ref),
                (False, operands(rb, vb, kkb, bb, kdb, lwb), yb_ref)], st_ref, n_sub)


def _scan(rkvl, prep, logw, *, dr):
    bsz, seq, _ = prep.shape
    rows = SCAN_SUB * CHUNK
    assert seq % rows == 0
    ns = seq // rows
    n_quad = dr // MXU_DIM

    def fwd(col, slot0=0):
        return pl.BlockSpec((None, rows, dr), lambda b, c: (b + slot0, c, col))

    def bwd(col, slot0=0):
        return pl.BlockSpec((None, rows, dr), lambda b, c: (b + slot0, ns - 1 - c, col))

    return pl.pallas_call(
        functools.partial(_scan_body, n_sub=SCAN_SUB),
        grid=(bsz, ns),
        in_specs=[fwd(0, 1), fwd(2, 1), fwd(P_KK), fwd(P_BF), fwd(P_KDF), fwd(L_F),
                  bwd(0, 1), bwd(2, 1), bwd(P_KK), bwd(P_BB), bwd(P_KDB), bwd(L_B)],
        out_specs=[pl.BlockSpec((None, rows, dr), lambda b, c: (b, c, 0)),
                   pl.BlockSpec((None, rows, dr), lambda b, c: (b, ns - 1 - c, 0))],
        out_shape=[jax.ShapeDtypeStruct((bsz, seq, dr), BF16)] * 2,
        scratch_shapes=[pltpu.VMEM((2 * n_quad, MXU_DIM, MXU_DIM), F32)],
        compiler_params=pltpu.CompilerParams(
            dimension_semantics=("arbitrary", "arbitrary"), vmem_limit_bytes=VMEM_LIMIT),
        name="scan",
    )(rkvl, rkvl, prep, prep, prep, logw, rkvl, rkvl, prep, prep, prep, logw)


def _post_ffn_body(yf_ref, yb_ref, g_ref, bonus_ref, oconv_ref, x_ref, wout_ref, gn_ref, ones_ref,
                   n2_ref, wg_ref, wu_ref, wd_ref, nf_ref, out_ref, *, dr, final_norm):
    ones_bd = ones_ref[...]
    y = yf_ref[...].astype(F32) + yb_ref[...].astype(F32)
    mean = jnp.dot(y.astype(BF16), ones_bd, preferred_element_type=F32) * (1.0 / HEAD)
    d = y - mean
    var = jnp.dot((d * d).astype(BF16), ones_bd, preferred_element_type=F32) * (1.0 / HEAD)
    yn = (d * lax.rsqrt(var + GN_EPS) * gn_ref[0:1, :] + gn_ref[1:2, :]
          + bonus_ref[...].astype(F32))
    o = (yn * g_ref[...].astype(F32)).astype(BF16)
    x1 = (x_ref[...]
          + jnp.dot(o, wout_ref[0:dr, :], preferred_element_type=F32)
          + jnp.dot(oconv_ref[...], wout_ref[dr:, :], preferred_element_type=F32))

    h = _rmsnorm(x1, n2_ref[...]).astype(BF16)
    a = jnp.dot(h, wg_ref[...], preferred_element_type=F32)
    u = jnp.dot(h, wu_ref[...], preferred_element_type=F32)
    z = (a * jax.nn.sigmoid(a) * u).astype(BF16)
    x2 = x1 + jnp.dot(z, wd_ref[...], preferred_element_type=F32)
    out_ref[...] = _rmsnorm(x2, nf_ref[...]) if final_norm else x2


def _post_ffn(yf, yb, prep, oconv, x, wout, gn, ones_bd, n2, wg, wu, wd, nf, *, dr, final_norm,
              row_tile):
    n, d = x.shape

    def rows(width, col=0):
        return pl.BlockSpec((row_tile, width), lambda i: (i, col))

    def full(a):
        return pl.BlockSpec(a.shape, lambda i: (0,) * a.ndim, pipeline_mode=pl.Buffered(1))

    return pl.pallas_call(
        functools.partial(_post_ffn_body, dr=dr, final_norm=final_norm),
        grid=(n // row_tile,),
        in_specs=[rows(dr), rows(dr), rows(dr, P_G), rows(dr, P_BONUS), rows(oconv.shape[-1]),
                  rows(d), full(wout), full(gn), full(ones_bd),
                  full(n2), full(wg), full(wu), full(wd), full(nf)],
        out_specs=rows(d),
        out_shape=jax.ShapeDtypeStruct((n, d), F32),
        compiler_params=pltpu.CompilerParams(
            dimension_semantics=("arbitrary",), vmem_limit_bytes=VMEM_LIMIT),
        name="post_ffn",
    )(yf, yb, prep, prep, oconv, x, wout, gn, ones_bd, n2, wg, wu, wd, nf)


def _pad_cols(a, width):
    return jnp.pad(a, ((0, 0), (0, width - a.shape[1])))


def _pad_rows(a, height):
    return jnp.pad(a, ((0, height - a.shape[0]), (0, 0)))


def kernel(x, norm1_w, w_in, mu_shift, w_up_f, w0_f, w_up_b, w0_b, a_up_f, a0_f, a_up_b, a0_b,
           g_up, k_k, k_a_f, k_a_b, r_k_f, r_k_b, gn_w, gn_b, conv_w, w_out, norm2_w, w_gate,
           w_up, w_down, norm_f_w):
    bsz, seq, d = x.shape
    depth = w_in.shape[0]
    dr = w0_f.shape[-1]
    n_dec, n_aaa, n_gate = w_up_f.shape[1], a_up_f.shape[1], g_up.shape[1]
    d_conv = conv_w.shape[-1]
    assert dr % MXU_DIM == 0 and d_conv % COL_TILE == 0 and seq % CHUNK == 0
    assert n_dec + n_aaa <= 128 and n_gate <= 256
    o_xw = 3 * dr
    o_xa = o_xw + n_dec
    o_xg = o_xa + n_aaa
    o_conv = o_xg + n_gate

    head_id = jnp.arange(dr) // HEAD
    ones_bd = (head_id[:, None] == head_id[None, :]).astype(BF16)

    for l in range(depth):
        w = w_in[l].astype(BF16)
        w_shift = jnp.concatenate(
            [w[:, :o_xw], _pad_cols(w[:, o_xw:o_xg], COL_TILE), _pad_cols(w[:, o_xg:o_conv], COL_TILE)],
            axis=1)
        w_conv = (w[:, o_conv:].reshape(d, 3, d_conv // COL_TILE, COL_TILE)
                  .transpose(0, 2, 1, 3).reshape(d, 3 * d_conv))
        mu = mu_shift[l][None, :]
        mu_p = jnp.concatenate(
            [mu[:, :o_xw], _pad_cols(mu[:, o_xw:o_xg], COL_TILE), _pad_cols(mu[:, o_xg:o_conv], COL_TILE)],
            axis=1)

        rkvl, oconv = _inproj(x, norm1_w[l][None, :], w_shift, mu_p, w_conv, conv_w[l])

        vecs = jnp.stack([w0_f[l], w0_b[l], a0_f[l], a0_b[l], k_k[l], k_a_f[l], k_a_b[l],
                          r_k_f[l].reshape(dr), r_k_b[l].reshape(dr)], axis=0)
        vecs = _pad_rows(vecs, 16)
        wup = _pad_rows(jnp.concatenate([w_up_f[l], w_up_b[l]], axis=1), 128).astype(BF16)
        aup = jnp.concatenate([a_up_f[l], a_up_b[l]], axis=1)
        aup = jnp.pad(aup, ((n_dec, 128 - n_dec - n_aaa), (0, 0))).astype(BF16)
        gup = _pad_rows(g_up[l], 256).astype(BF16)
        prep, logw = _prep(rkvl, vecs, wup, aup, gup, ones_bd, dr=dr, row_tile=min(1024, seq))

        yf, yb = _scan(rkvl, prep, logw, dr=dr)

        gn = jnp.stack([gn_w[l], gn_b[l]], axis=0)
        n_tok = bsz * seq
        x = _post_ffn(yf.reshape(n_tok, dr), yb.reshape(n_tok, dr), prep.reshape(n_tok, P_NUM * dr),
                      oconv.reshape(n_tok, d_conv), x.reshape(n_tok, d), w_out[l].astype(BF16), gn,
                      ones_bd, norm2_w[l][None, :], w_gate[l].astype(BF16), w_up[l].astype(BF16),
                      w_down[l].astype(BF16), norm_f_w[None, :], dr=dr,
                      final_norm=(l == depth - 1), row_tile=512).reshape(bsz, seq, d)
    return x
```

```python
import functools

import jax
import jax.numpy as jnp
from jax import lax
from jax.experimental import pallas as pl
from jax.experimental.pallas import tpu as pltpu

F32 = jnp.float32
BF16 = jnp.bfloat16

HEAD = 64
CHUNK = 64
INV_BASE = 8
MXU_DIM = 256
QUAD = MXU_DIM // CHUNK
SCAN_SUB = 8
COL_TILE = 256
SHIFT_TILE = 512
ROW_CHUNK = 512
HALO = 8
LOG_DECAY_SCALE = 0.606531
RMS_EPS = 1e-6
GN_EPS = 64e-5
NORM_EPS = 1e-12
VMEM_LIMIT = 56 * 1024 * 1024

P_KK, P_BF, P_BB, P_KDF, P_KDB, P_G, P_BONUS, P_NUM = range(8)
L_F, L_B, L_NUM = range(3)
V_W0F, V_W0B, V_A0F, V_A0B, V_KK, V_KAF, V_KAB, V_RKF, V_RKB, V_NUM = range(10)


def _dot(a, b):
    return jnp.dot(a.astype(BF16), b.astype(BF16), preferred_element_type=F32)


def _dot_nt(a, b):
    return lax.dot_general(a.astype(BF16), b.astype(BF16), (((1,), (1,)), ((), ())),
                           preferred_element_type=F32)


def _rmsnorm(x, w):
    ms = jnp.mean(x * x, axis=-1, keepdims=True)
    return x * lax.rsqrt(ms + RMS_EPS) * w


def _zero_halo(s_ref, seq):
    z = jnp.zeros((HALO, s_ref.shape[1]), s_ref.dtype)
    s_ref[0:HALO, :] = z
    s_ref[HALO + seq:2 * HALO + seq, :] = z


def _windows(s_ref, c0, rows):
    width = s_ref.shape[-1]
    win = s_ref[c0:c0 + rows + 2 * HALO, :].reshape(rows // HALO + 2, HALO, width)
    down = pltpu.roll(win, 1, 1)
    up = pltpu.roll(win, HALO - 1, 1)
    sub = lax.broadcasted_iota(jnp.int32, (1, HALO, width), 1)
    prv = jnp.where(sub == 0, down[:-2], down[1:-1])
    nxt = jnp.where(sub == HALO - 1, up[2:], up[1:-1])
    return tuple(a.reshape(rows, width) for a in (win[1:-1], prv, nxt))


def _shift_body(x_ref, n1_ref, w_ref, mu_ref, out_ref, hout_ref, h_s, p_s, *, seq):
    bb = pl.program_id(0)
    j = pl.program_id(1)
    new = lax.rem(bb, 2)
    old = 1 - new

    @pl.when((bb == 0) & (j == 0))
    def _():
        _zero_halo(p_s, seq)
        h_s[1] = jnp.zeros(h_s.shape[1:], h_s.dtype)

    n_rows = x_ref.shape[0]
    h = _rmsnorm(x_ref[...], n1_ref[...]).astype(BF16)
    h_s[new, pl.ds(pl.multiple_of(j * n_rows, n_rows), n_rows), :] = h
    hout_ref[...] = h

    w = w_ref[...]
    mu = mu_ref[...]
    keep, mix = 1.0 - mu, 0.5 * mu

    def shift(c0):
        cur, prv, nxt = _windows(p_s, c0, ROW_CHUNK)
        out_ref[c0:c0 + ROW_CHUNK, :] = (keep * cur + mix * (prv + nxt)).astype(out_ref.dtype)

    for c0 in range(0, seq, ROW_CHUNK):
        p_s[HALO + c0:HALO + c0 + ROW_CHUNK, :] = jnp.dot(
            h_s[old, c0:c0 + ROW_CHUNK, :], w, preferred_element_type=F32)
        if c0:
            shift(c0 - ROW_CHUNK)
    shift(seq - ROW_CHUNK)


def _conv_body(h_ref, wb_ref, wc_ref, wh_ref, cw_ref, out_ref, u_s, gb_s, *, seq):
    @pl.when(pl.program_id(1) == 0)
    def _():
        _zero_halo(u_s, seq)

    w = jnp.concatenate([wb_ref[...], wc_ref[...], wh_ref[...]], axis=1)
    cw = cw_ref[...]

    def conv(c0):
        cur, prv, nxt = _windows(u_s, c0, ROW_CHUNK)
        y = cw[0:1, :] * prv + cw[1:2, :] * cur + cw[2:3, :] * nxt
        out_ref[c0:c0 + ROW_CHUNK, :] = (gb_s[c0:c0 + ROW_CHUNK, :] * y).astype(out_ref.dtype)

    for c0 in range(0, seq, ROW_CHUNK):
        p = jnp.dot(h_ref[c0:c0 + ROW_CHUNK, :], w, preferred_element_type=F32)
        gb_s[c0:c0 + ROW_CHUNK, :] = p[:, :COL_TILE]
        u_s[HALO + c0:HALO + c0 + ROW_CHUNK, :] = p[:, COL_TILE:2 * COL_TILE] * p[:, 2 * COL_TILE:]
        if c0:
            conv(c0 - ROW_CHUNK)
    conv(seq - ROW_CHUNK)


def _inproj(x, n1, w_shift, mu_p, w_conv, conv_w):
    bsz, seq, d = x.shape
    assert seq % ROW_CHUNK == 0
    params = pltpu.CompilerParams(
        dimension_semantics=("arbitrary", "arbitrary"), vmem_limit_bytes=VMEM_LIMIT)
    n_tiles = w_shift.shape[1] // SHIFT_TILE
    assert seq % n_tiles == 0
    norm_rows = seq // n_tiles

    def nxt(b):
        return jnp.minimum(b, bsz - 1)

    rkvl, h = pl.pallas_call(
        functools.partial(_shift_body, seq=seq),
        grid=(bsz + 1, n_tiles),
        in_specs=[
            pl.BlockSpec((None, norm_rows, d), lambda b, j: (nxt(b), j, 0)),
            pl.BlockSpec((1, d), lambda b, j: (0, 0)),
            pl.BlockSpec((d, SHIFT_TILE), lambda b, j: (0, j)),
            pl.BlockSpec((1, SHIFT_TILE), lambda b, j: (0, j)),
        ],
        out_specs=[
            pl.BlockSpec((None, seq, SHIFT_TILE), lambda b, j: (b, 0, j)),
            pl.BlockSpec((None, norm_rows, d), lambda b, j: (b, j, 0)),
        ],
        out_shape=[
            jax.ShapeDtypeStruct((bsz + 1, seq, w_shift.shape[1]), BF16),
            jax.ShapeDtypeStruct((bsz + 1, seq, d), BF16),
        ],
        scratch_shapes=[pltpu.VMEM((2, seq, d), BF16),
                        pltpu.VMEM((seq + 2 * HALO, SHIFT_TILE), F32)],
        compiler_params=params,
        name="inproj_shift",
    )(x, n1, w_shift, mu_p)
    d_conv = conv_w.shape[-1]
    n_conv = d_conv // COL_TILE
    oconv = pl.pallas_call(
        functools.partial(_conv_body, seq=seq),
        grid=(bsz, n_conv),
        in_specs=[
            pl.BlockSpec((None, seq, d), lambda b, q: (b, 0, 0)),
            pl.BlockSpec((d, COL_TILE), lambda b, q: (0, q)),
            pl.BlockSpec((d, COL_TILE), lambda b, q: (0, n_conv + q)),
            pl.BlockSpec((d, COL_TILE), lambda b, q: (0, 2 * n_conv + q)),
            pl.BlockSpec((3, COL_TILE), lambda b, q: (0, q)),
        ],
        out_specs=pl.BlockSpec((None, seq, COL_TILE), lambda b, q: (b, 0, q)),
        out_shape=jax.ShapeDtypeStruct((bsz, seq, d_conv), BF16),
        scratch_shapes=[pltpu.VMEM((seq + 2 * HALO, COL_TILE), F32),
                        pltpu.VMEM((seq, COL_TILE), F32)],
        compiler_params=params,
        name="inproj_conv",
    )(h, w_conv, w_conv, w_conv, conv_w)
    return rkvl, oconv


def _prep_body(r_ref, k_ref, v_ref, l_ref, vec_ref, wup_ref, aup_ref, gup_ref, ones_ref,
               out_ref, lw_ref, *, dr):
    r = r_ref[...].astype(F32)
    k = k_ref[...].astype(F32)
    v = v_ref[...].astype(F32)
    vec = vec_ref[...]
    ones_bd = ones_ref[...]

    def vrow(i):
        return vec[i:i + 1, :]

    t128 = l_ref[:, 0:128]
    lw = jnp.dot(jnp.tanh(t128.astype(F32)).astype(BF16), wup_ref[...],
                 preferred_element_type=F32)
    aa = jnp.dot(t128, aup_ref[...], preferred_element_type=F32)
    lw_f = -LOG_DECAY_SCALE * jax.nn.sigmoid(vrow(V_W0F) + lw[:, :dr])
    lw_b = -LOG_DECAY_SCALE * jax.nn.sigmoid(vrow(V_W0B) + lw[:, dr:])
    a_f = jax.nn.sigmoid(vrow(V_A0F) + aa[:, :dr])
    a_b = jax.nn.sigmoid(vrow(V_A0B) + aa[:, dr:])
    g = jnp.dot(jax.nn.sigmoid(l_ref[:, 256:512].astype(F32)).astype(BF16), gup_ref[...],
                preferred_element_type=F32)

    def head_sum(t):
        return jnp.dot(t.astype(BF16), ones_bd, preferred_element_type=F32)

    kkr = k * vrow(V_KK)
    kk = kkr * lax.rsqrt(jnp.maximum(head_sum(kkr * kkr), NORM_EPS * NORM_EPS))
    kd_f = k * (1.0 + (a_f - 1.0) * vrow(V_KAF))
    kd_b = k * (1.0 + (a_b - 1.0) * vrow(V_KAB))
    bonus = head_sum(r * kd_f * vrow(V_RKF) + r * kd_b * vrow(V_RKB)) * v

    def put(i, val):
        out_ref[:, i * dr:(i + 1) * dr] = val.astype(out_ref.dtype)

    put(P_KK, kk)
    put(P_BF, a_f * kk)
    put(P_BB, a_b * kk)
    put(P_KDF, kd_f)
    put(P_KDB, kd_b)
    put(P_G, g)
    put(P_BONUS, bonus)
    lw_ref[:, L_F * dr:(L_F + 1) * dr] = lw_f
    lw_ref[:, L_B * dr:(L_B + 1) * dr] = lw_b


def _prep(rkvl, vecs, wup, aup, gup, ones_bd, *, dr, row_tile):
    bsz, seq = rkvl.shape[0] - 1, rkvl.shape[1]

    def col(c):
        return pl.BlockSpec((None, row_tile, dr), lambda b, i: (b + 1, i, c))

    def full(a):
        return pl.BlockSpec(a.shape, lambda b, i: (0,) * a.ndim)

    return pl.pallas_call(
        functools.partial(_prep_body, dr=dr),
        grid=(bsz, seq // row_tile),
        in_specs=[col(0), col(1), col(2), col(3),
                  full(vecs), full(wup), full(aup), full(gup), full(ones_bd)],
        out_specs=[pl.BlockSpec((None, row_tile, P_NUM * dr), lambda b, i: (b, i, 0)),
                   pl.BlockSpec((None, row_tile, L_NUM * dr), lambda b, i: (b, i, 0))],
        out_shape=[jax.ShapeDtypeStruct((bsz, seq, P_NUM * dr), BF16),
                   jax.ShapeDtypeStruct((bsz, seq, L_NUM * dr), F32)],
        compiler_params=pltpu.CompilerParams(
            dimension_semantics=("arbitrary", "arbitrary"), vmem_limit_bytes=VMEM_LIMIT),
        name="prep",
    )(rkvl, rkvl, rkvl, rkvl, vecs, wup, aup, gup, ones_bd)


def _block_diag(x, lane_head):
    return jnp.concatenate([jnp.where(lane_head == h, x, 0.0) for h in range(QUAD)],
                           axis=0).astype(BF16)


def _cumsum_time(x, *, forward):
    rows, n = x.shape
    tiles = rows // HALO
    x = x.reshape(tiles, HALO, n)
    sub = lax.broadcasted_iota(jnp.int32, (1, HALO, n), 1)
    step = 1
    while step < HALO:
        if forward:
            x = x + jnp.where(sub >= step, pltpu.roll(x, step, 1), 0.0)
        else:
            x = x + jnp.where(sub < HALO - step, pltpu.roll(x, HALO - step, 1), 0.0)
        step *= 2
    edge = HALO - 1 if forward else 0
    parts = [None] * tiles
    carry = None
    for i in (range(tiles) if forward else reversed(range(tiles))):
        parts[i] = x[i:i + 1] if carry is None else x[i:i + 1] + carry
        total = x[i:i + 1, edge:edge + 1, :]
        carry = total if carry is None else carry + total
    return jnp.concatenate(parts, axis=0).reshape(rows, n)


def _decay_factors(lw, *, forward):
    c = CHUNK
    cs = _cumsum_time(lw, forward=forward)
    cs_end = cs[c - 1:c, :] if forward else cs[0:1, :]
    return dict(e_t=jnp.exp(cs), e_prev=jnp.exp(cs - lw), e_inv=jnp.exp(-cs),
                e_end=jnp.exp(cs_end - cs), w_c=jnp.exp(cs_end))


def _scan_step(dirs, st_ref, n_sub):
    c = CHUNK
    t_idx = lax.broadcasted_iota(jnp.int32, (c, MXU_DIM), 0)
    lane = lax.broadcasted_iota(jnp.int32, (c, MXU_DIM), 1)
    s_idx = lane & (c - 1)
    lane_head = lane >> 6
    eye = (s_idx == t_idx).astype(F32)
    bd_mask = ((lax.broadcasted_iota(jnp.int32, (MXU_DIM, MXU_DIM), 0) >> 6)
               == (lax.broadcasted_iota(jnp.int32, (MXU_DIM, MXU_DIM), 1) >> 6))

    def bd(x):
        return _block_diag(x, lane_head)

    subs = []
    for k in range(n_sub):
        chains = []
        for i, (forward, refs, y_ref) in enumerate(dirs):
            row0 = (k if forward else n_sub - 1 - k) * c
            rows = slice(row0, row0 + c)
            a = {name: ref[rows, :].astype(F32) for name, ref in refs.items()}
            f = _decay_factors(a["lw"], forward=forward)
            if forward:
                strict, incl = s_idx < t_idx, s_idx <= t_idx
            else:
                strict, incl = s_idx > t_idx, s_idx >= t_idx
            kap, rt = a["kk"] * f["e_prev"], a["r"] * f["e_t"]
            kt, bt = a["kd"] * f["e_inv"], a["b"] * f["e_inv"]
            kh, bh = a["kd"] * f["e_end"], a["b"] * f["e_end"]
            n_quad = a["r"].shape[-1] // MXU_DIM
            for q in range(n_quad):
                sl = slice(q * MXU_DIM, (q + 1) * MXU_DIM)
                chains.append(dict(
                    strict=strict, incl=incl, y_ref=y_ref, rows=rows, sl=sl, idx=i * n_quad + q,
                    kr=jnp.concatenate([kap[:, sl], rt[:, sl]], axis=0).astype(BF16),
                    kt=kt[:, sl], bt=bt[:, sl], v=a["v"][:, sl],
                    w_col=jnp.broadcast_to(f["w_c"][:, sl], (MXU_DIM // 2, MXU_DIM)).T,
                    lhs_t=jnp.concatenate([kh[:, sl], bh[:, sl]], axis=0).T.astype(BF16)))
        subs.append(chains)
    every = [ch for chains in subs for ch in chains]

    for ch in every:
        g_b = _dot_nt(ch["kr"], bd(ch["bt"]))
        g_k = _dot_nt(ch["kr"], bd(ch["kt"]))
        ch["arb"] = jnp.where(ch["incl"], g_b[c:], 0.0)
        ch["aa"] = jnp.concatenate([jnp.where(ch["strict"], g_k[:c], 0.0),
                                    jnp.where(ch["incl"], g_k[c:], 0.0)], axis=0)
        ch["akb"] = jnp.where(ch["strict"], g_b[:c], 0.0)
    for ch in every:
        ch["av"] = _dot(ch["aa"], bd(ch["v"]))

    base_bits = INV_BASE.bit_length() - 1
    same_base = (t_idx >> base_bits) == (s_idx >> base_bits)
    for ch in every:
        ch["x"] = -jnp.where(same_base, ch["akb"], 0.0)
        ch["t"] = eye + ch["x"]
    for ch in every:
        ch["x"] = _dot(ch["x"], bd(ch["x"]))
    for _ in range(base_bits - 2):
        for ch in every:
            rr = _dot(jnp.concatenate([ch["x"], ch["t"]], axis=0), bd(ch["x"]))
            ch["x"] = rr[:c]
            ch["t"] = ch["t"] + rr[c:]
    for ch in every:
        ch["t"] = ch["t"] + _dot(ch["t"], bd(ch["x"]))
    for bits in range(base_bits, CHUNK.bit_length() - 1):
        pair = ((t_idx >> (bits + 1)) == (s_idx >> (bits + 1))) & ((t_idx >> bits) != (s_idx >> bits))
        for ch in every:
            ch["e"] = _dot(jnp.where(pair, ch["akb"], 0.0), bd(ch["t"]))
        for ch in every:
            ch["t"] = ch["t"] - _dot(ch["t"], bd(ch["e"]))

    for chains in subs:
        for ch in chains:
            ch["m"] = st_ref[ch["idx"]]
            ch["krm"] = _dot(ch["kr"], ch["m"])
        for ch in chains:
            ch["sa"] = _dot(ch["t"], bd(ch["krm"][:c] + ch["av"][:c]))
        for ch in chains:
            sa = ch["sa"]
            y = ch["krm"][c:] + ch["av"][c:] - _dot(ch["arb"], bd(sa))
            ch["y_ref"][ch["rows"], ch["sl"]] = y.astype(ch["y_ref"].dtype)
            upd = _dot(ch["lhs_t"], jnp.concatenate([ch["v"], -sa], axis=0))
            decay = jnp.concatenate([ch["w_col"], ch["w_col"]], axis=1)
            st_ref[ch["idx"]] = decay * ch["m"] + jnp.where(bd_mask, upd, 0.0)


def _scan_body(rf, vf, kkf, bf, kdf, lwf, rb, vb, kkb, bb, kdb, lwb, yf_ref, yb_ref, st_ref,
               *, n_sub):
    @pl.when(pl.program_id(1) == 0)
    def _():
        st_ref[...] = jnp.zeros_like(st_ref)

    def operands(r, v, kk, b, kd, lw):
        return dict(r=r, v=v, kk=kk, b=b, kd=kd, lw=lw)

    _scan_step([(True, operands(rf, vf, kkf, bf, kdf, lwf), yf_ref),
                (False, operands(rb, vb, kkb, bb, kdb, lwb), yb_ref)], st_ref, n_sub)


def _scan(rkvl, prep, logw, *, dr):
    bsz, seq, _ = prep.shape
    rows = SCAN_SUB * CHUNK
    assert seq % rows == 0
    ns = seq // rows
    n_quad = dr // MXU_DIM

    def fwd(col, slot0=0):
        return pl.BlockSpec((None, rows, dr), lambda b, c: (b + slot0, c, col))

    def bwd(col, slot0=0):
        return pl.BlockSpec((None, rows, dr), lambda b, c: (b + slot0, ns - 1 - c, col))

    return pl.pallas_call(
        functools.partial(_scan_body, n_sub=SCAN_SUB),
        grid=(bsz, ns),
        in_specs=[fwd(0, 1), fwd(2, 1), fwd(P_KK), fwd(P_BF), fwd(P_KDF), fwd(L_F),
                  bwd(0, 1), bwd(2, 1), bwd(P_KK), bwd(P_BB), bwd(P_KDB), bwd(L_B)],
        out_specs=[pl.BlockSpec((None, rows, dr), lambda b, c: (b, c, 0)),
                   pl.BlockSpec((None, rows, dr), lambda b, c: (b, ns - 1 - c, 0))],
        out_shape=[jax.ShapeDtypeStruct((bsz, seq, dr), BF16)] * 2,
        scratch_shapes=[pltpu.VMEM((2 * n_quad, MXU_DIM, MXU_DIM), F32)],
        compiler_params=pltpu.CompilerParams(
            dimension_semantics=("arbitrary", "arbitrary"), vmem_limit_bytes=VMEM_LIMIT),
        name="scan",
    )(rkvl, rkvl, prep, prep, prep, logw, rkvl, rkvl, prep, prep, prep, logw)


def _post_ffn_body(yf_ref, yb_ref, g_ref, bonus_ref, oconv_ref, x_ref, wout_ref, gn_ref, ones_ref,
                   n2_ref, wg_ref, wu_ref, wd_ref, nf_ref, out_ref, *, dr, final_norm):
    ones_bd = ones_ref[...]
    y = yf_ref[...].astype(F32) + yb_ref[...].astype(F32)
    mean = jnp.dot(y.astype(BF16), ones_bd, preferred_element_type=F32) * (1.0 / HEAD)
    d = y - mean
    var = jnp.dot((d * d).astype(BF16), ones_bd, preferred_element_type=F32) * (1.0 / HEAD)
    yn = (d * lax.rsqrt(var + GN_EPS) * gn_ref[0:1, :] + gn_ref[1:2, :]
          + bonus_ref[...].astype(F32))
    o = (yn * g_ref[...].astype(F32)).astype(BF16)
    x1 = (x_ref[...]
          + jnp.dot(o, wout_ref[0:dr, :], preferred_element_type=F32)
          + jnp.dot(oconv_ref[...], wout_ref[dr:, :], preferred_element_type=F32))

    h = _rmsnorm(x1, n2_ref[...]).astype(BF16)
    a = jnp.dot(h, wg_ref[...], preferred_element_type=F32)
    u = jnp.dot(h, wu_ref[...], preferred_element_type=F32)
    z = (a * jax.nn.sigmoid(a) * u).astype(BF16)
    x2 = x1 + jnp.dot(z, wd_ref[...], preferred_element_type=F32)
    out_ref[...] = _rmsnorm(x2, nf_ref[...]) if final_norm else x2


def _post_ffn(yf, yb, prep, oconv, x, wout, gn, ones_bd, n2, wg, wu, wd, nf, *, dr, final_norm,
              row_tile):
    n, d = x.shape

    def rows(width, col=0):
        return pl.BlockSpec((row_tile, width), lambda i: (i, col))

    def full(a):
        return pl.BlockSpec(a.shape, lambda i: (0,) * a.ndim, pipeline_mode=pl.Buffered(1))

    return pl.pallas_call(
        functools.partial(_post_ffn_body, dr=dr, final_norm=final_norm),
        grid=(n // row_tile,),
        in_specs=[rows(dr), rows(dr), rows(dr, P_G), rows(dr, P_BONUS), rows(oconv.shape[-1]),
                  rows(d), full(wout), full(gn), full(ones_bd),
                  full(n2), full(wg), full(wu), full(wd), full(nf)],
        out_specs=rows(d),
        out_shape=jax.ShapeDtypeStruct((n, d), F32),
        compiler_params=pltpu.CompilerParams(
            dimension_semantics=("arbitrary",), vmem_limit_bytes=VMEM_LIMIT),
        name="post_ffn",
    )(yf, yb, prep, prep, oconv, x, wout, gn, ones_bd, n2, wg, wu, wd, nf)


def _pad_cols(a, width):
    return jnp.pad(a, ((0, 0), (0, width - a.shape[1])))


def _pad_rows(a, height):
    return jnp.pad(a, ((0, height - a.shape[0]), (0, 0)))


def kernel(x, norm1_w, w_in, mu_shift, w_up_f, w0_f, w_up_b, w0_b, a_up_f, a0_f, a_up_b, a0_b,
           g_up, k_k, k_a_f, k_a_b, r_k_f, r_k_b, gn_w, gn_b, conv_w, w_out, norm2_w, w_gate,
           w_up, w_down, norm_f_w):
    bsz, seq, d = x.shape
    depth = w_in.shape[0]
    dr = w0_f.shape[-1]
    n_dec, n_aaa, n_gate = w_up_f.shape[1], a_up_f.shape[1], g_up.shape[1]
    d_conv = conv_w.shape[-1]
    assert dr % MXU_DIM == 0 and d_conv % COL_TILE == 0 and seq % CHUNK == 0
    assert n_dec + n_aaa <= 128 and n_gate <= 256
    o_xw = 3 * dr
    o_xa = o_xw + n_dec
    o_xg = o_xa + n_aaa
    o_conv = o_xg + n_gate

    head_id = jnp.arange(dr) // HEAD
    ones_bd = (head_id[:, None] == head_id[None, :]).astype(BF16)

    for l in range(depth):
        w = w_in[l].astype(BF16)
        w_shift = jnp.concatenate(
            [w[:, :o_xw], _pad_cols(w[:, o_xw:o_xg], COL_TILE), _pad_cols(w[:, o_xg:o_conv], COL_TILE)],
            axis=1)
        w_conv = w[:, o_conv:]
        mu = mu_shift[l][None, :]
        mu_p = jnp.concatenate(
            [mu[:, :o_xw], _pad_cols(mu[:, o_xw:o_xg], COL_TILE), _pad_cols(mu[:, o_xg:o_conv], COL_TILE)],
            axis=1)

        rkvl, oconv = _inproj(x, norm1_w[l][None, :], w_shift, mu_p, w_conv, conv_w[l])

        vecs = jnp.stack([w0_f[l], w0_b[l], a0_f[l], a0_b[l], k_k[l], k_a_f[l], k_a_b[l],
                          r_k_f[l].reshape(dr), r_k_b[l].reshape(dr)], axis=0)
        vecs = _pad_rows(vecs, 16)
        wup = _pad_rows(jnp.concatenate([w_up_f[l], w_up_b[l]], axis=1), 128).astype(BF16)
        aup = jnp.concatenate([a_up_f[l], a_up_b[l]], axis=1)
        aup = jnp.pad(aup, ((n_dec, 128 - n_dec - n_aaa), (0, 0))).astype(BF16)
        gup = _pad_rows(g_up[l], 256).astype(BF16)
        prep, logw = _prep(rkvl, vecs, wup, aup, gup, ones_bd, dr=dr, row_tile=min(1024, seq))

        yf, yb = _scan(rkvl, prep, logw, dr=dr)

        gn = jnp.stack([gn_w[l], gn_b[l]], axis=0)
        n_tok = bsz * seq
        x = _post_ffn(yf.reshape(n_tok, dr), yb.reshape(n_tok, dr), prep.reshape(n_tok, P_NUM * dr),
                      oconv.reshape(n_tok, d_conv), x.reshape(n_tok, d), w_out[l].astype(BF16), gn,
                      ones_bd, norm2_w[l][None, :], w_gate[l].astype(BF16), w_up[l].astype(BF16),
                      w_down[l].astype(BF16), norm_f_w[None, :], dr=dr,
                      final_norm=(l == depth - 1), row_tile=512).reshape(bsz, seq, d)
    return x
```

```python
import functools

import jax
import jax.numpy as jnp
from jax import lax
from jax.experimental import pallas as pl
from jax.experimental.pallas import tpu as pltpu

F32 = jnp.float32
BF16 = jnp.bfloat16

HEAD = 64
CHUNK = 64
INV_BASE = 8
MXU_DIM = 256
QUAD = MXU_DIM // CHUNK
SCAN_SUB = 8
COL_TILE = 256
SHIFT_TILE = 512
ROW_CHUNK = 512
SUBLANES = 8
LANES = 128
HALO = SUBLANES
HEAD_BITS = HEAD.bit_length() - 1
assert CHUNK == HEAD
LOG_DECAY_SCALE = 0.606531
RMS_EPS = 1e-6
GN_EPS = 64e-5
NORM_EPS = 1e-12
VMEM_LIMIT = 56 * 1024 * 1024

P_KK, P_BF, P_BB, P_KDF, P_KDB, P_G, P_BONUS, P_NUM = range(8)
L_F, L_B, L_NUM = range(3)
V_W0F, V_W0B, V_A0F, V_A0B, V_KK, V_KAF, V_KAB, V_RKF, V_RKB, V_NUM = range(10)


def _dot(a, b):
    return jnp.dot(a.astype(BF16), b.astype(BF16), preferred_element_type=F32)


def _dot_nt(a, b):
    return lax.dot_general(a.astype(BF16), b.astype(BF16), (((1,), (1,)), ((), ())),
                           preferred_element_type=F32)


def _rmsnorm(x, w):
    ms = jnp.mean(x * x, axis=-1, keepdims=True)
    return x * lax.rsqrt(ms + RMS_EPS) * w


def _zero_halo(s_ref, seq):
    z = jnp.zeros((HALO, s_ref.shape[1]), s_ref.dtype)
    s_ref[0:HALO, :] = z
    s_ref[HALO + seq:2 * HALO + seq, :] = z


def _windows(s_ref, c0, rows):
    width = s_ref.shape[-1]
    win = s_ref[c0:c0 + rows + 2 * HALO, :].reshape(rows // HALO + 2, HALO, width)
    down = pltpu.roll(win, 1, 1)
    up = pltpu.roll(win, HALO - 1, 1)
    sub = lax.broadcasted_iota(jnp.int32, (1, HALO, width), 1)
    prv = jnp.where(sub == 0, down[:-2], down[1:-1])
    nxt = jnp.where(sub == HALO - 1, up[2:], up[1:-1])
    return tuple(a.reshape(rows, width) for a in (win[1:-1], prv, nxt))


def _shift_body(x_ref, n1_ref, w_ref, mu_ref, out_ref, hout_ref, h_s, p_s, *, seq):
    bb = pl.program_id(0)
    j = pl.program_id(1)
    new = lax.rem(bb, 2)
    old = 1 - new

    @pl.when((bb == 0) & (j == 0))
    def _():
        _zero_halo(p_s, seq)
        h_s[1] = jnp.zeros(h_s.shape[1:], h_s.dtype)

    n_rows = x_ref.shape[0]
    h = _rmsnorm(x_ref[...], n1_ref[...]).astype(BF16)
    h_s[new, pl.ds(pl.multiple_of(j * n_rows, n_rows), n_rows), :] = h
    hout_ref[...] = h

    w = w_ref[...]
    mu = mu_ref[...]
    keep, mix = 1.0 - mu, 0.5 * mu

    def shift(c0):
        cur, prv, nxt = _windows(p_s, c0, ROW_CHUNK)
        out_ref[c0:c0 + ROW_CHUNK, :] = (keep * cur + mix * (prv + nxt)).astype(out_ref.dtype)

    for c0 in range(0, seq, ROW_CHUNK):
        p_s[HALO + c0:HALO + c0 + ROW_CHUNK, :] = jnp.dot(
            h_s[old, c0:c0 + ROW_CHUNK, :], w, preferred_element_type=F32)
        if c0:
            shift(c0 - ROW_CHUNK)
    shift(seq - ROW_CHUNK)


def _conv_body(h_ref, wb_ref, wc_ref, wh_ref, cw_ref, out_ref, u_s, gb_s, *, seq):
    @pl.when(pl.program_id(1) == 0)
    def _():
        _zero_halo(u_s, seq)

    w = jnp.concatenate([wb_ref[...], wc_ref[...], wh_ref[...]], axis=1)
    cw = cw_ref[...]

    def conv(c0):
        cur, prv, nxt = _windows(u_s, c0, ROW_CHUNK)
        y = cw[0:1, :] * prv + cw[1:2, :] * cur + cw[2:3, :] * nxt
        out_ref[c0:c0 + ROW_CHUNK, :] = (gb_s[c0:c0 + ROW_CHUNK, :] * y).astype(out_ref.dtype)

    for c0 in range(0, seq, ROW_CHUNK):
        p = jnp.dot(h_ref[c0:c0 + ROW_CHUNK, :], w, preferred_element_type=F32)
        gb_s[c0:c0 + ROW_CHUNK, :] = p[:, :COL_TILE]
        u_s[HALO + c0:HALO + c0 + ROW_CHUNK, :] = p[:, COL_TILE:2 * COL_TILE] * p[:, 2 * COL_TILE:]
        if c0:
            conv(c0 - ROW_CHUNK)
    conv(seq - ROW_CHUNK)


def _inproj(x, n1, w_shift, mu_p, w_conv, conv_w):
    bsz, seq, d = x.shape
    assert seq % ROW_CHUNK == 0
    params = pltpu.CompilerParams(
        dimension_semantics=("arbitrary", "arbitrary"), vmem_limit_bytes=VMEM_LIMIT)
    n_tiles = w_shift.shape[1] // SHIFT_TILE
    assert seq % n_tiles == 0
    norm_rows = seq // n_tiles

    def nxt(b):
        return jnp.minimum(b, bsz - 1)

    rkvl, h = pl.pallas_call(
        functools.partial(_shift_body, seq=seq),
        grid=(bsz + 1, n_tiles),
        in_specs=[
            pl.BlockSpec((None, norm_rows, d), lambda b, j: (nxt(b), j, 0)),
            pl.BlockSpec((1, d), lambda b, j: (0, 0)),
            pl.BlockSpec((d, SHIFT_TILE), lambda b, j: (0, j)),
            pl.BlockSpec((1, SHIFT_TILE), lambda b, j: (0, j)),
        ],
        out_specs=[
            pl.BlockSpec((None, seq, SHIFT_TILE), lambda b, j: (b, 0, j)),
            pl.BlockSpec((None, norm_rows, d), lambda b, j: (b, j, 0)),
        ],
        out_shape=[
            jax.ShapeDtypeStruct((bsz + 1, seq, w_shift.shape[1]), BF16),
            jax.ShapeDtypeStruct((bsz + 1, seq, d), BF16),
        ],
        scratch_shapes=[pltpu.VMEM((2, seq, d), BF16),
                        pltpu.VMEM((seq + 2 * HALO, SHIFT_TILE), F32)],
        compiler_params=params,
        name="inproj_shift",
    )(x, n1, w_shift, mu_p)
    d_conv = conv_w.shape[-1]
    n_conv = d_conv // COL_TILE
    oconv = pl.pallas_call(
        functools.partial(_conv_body, seq=seq),
        grid=(bsz, n_conv),
        in_specs=[
            pl.BlockSpec((None, seq, d), lambda b, q: (b, 0, 0)),
            pl.BlockSpec((d, COL_TILE), lambda b, q: (0, q)),
            pl.BlockSpec((d, COL_TILE), lambda b, q: (0, n_conv + q)),
            pl.BlockSpec((d, COL_TILE), lambda b, q: (0, 2 * n_conv + q)),
            pl.BlockSpec((3, COL_TILE), lambda b, q: (0, q)),
        ],
        out_specs=pl.BlockSpec((None, seq, COL_TILE), lambda b, q: (b, 0, q)),
        out_shape=jax.ShapeDtypeStruct((bsz, seq, d_conv), BF16),
        scratch_shapes=[pltpu.VMEM((seq + 2 * HALO, COL_TILE), F32),
                        pltpu.VMEM((seq, COL_TILE), F32)],
        compiler_params=params,
        name="inproj_conv",
    )(h, w_conv, w_conv, w_conv, conv_w)
    return rkvl, oconv


def _prep_body(r_ref, k_ref, v_ref, l_ref, vec_ref, wup_ref, aup_ref, gup_ref, ones_ref,
               out_ref, lw_ref, *, dr):
    r = r_ref[...].astype(F32)
    k = k_ref[...].astype(F32)
    v = v_ref[...].astype(F32)
    vec = vec_ref[...]
    ones_bd = ones_ref[...]

    def vrow(i):
        return vec[i:i + 1, :]

    t128 = l_ref[:, 0:LANES]
    lw = jnp.dot(jnp.tanh(t128.astype(F32)).astype(BF16), wup_ref[...],
                 preferred_element_type=F32)
    aa = jnp.dot(t128, aup_ref[...], preferred_element_type=F32)
    lw_f = -LOG_DECAY_SCALE * jax.nn.sigmoid(vrow(V_W0F) + lw[:, :dr])
    lw_b = -LOG_DECAY_SCALE * jax.nn.sigmoid(vrow(V_W0B) + lw[:, dr:])
    a_f = jax.nn.sigmoid(vrow(V_A0F) + aa[:, :dr])
    a_b = jax.nn.sigmoid(vrow(V_A0B) + aa[:, dr:])
    g = jnp.dot(jax.nn.sigmoid(l_ref[:, COL_TILE:2 * COL_TILE].astype(F32)).astype(BF16), gup_ref[...],
                preferred_element_type=F32)

    def head_sum(t):
        return jnp.dot(t.astype(BF16), ones_bd, preferred_element_type=F32)

    kkr = k * vrow(V_KK)
    kk = kkr * lax.rsqrt(jnp.maximum(head_sum(kkr * kkr), NORM_EPS * NORM_EPS))
    kd_f = k * (1.0 + (a_f - 1.0) * vrow(V_KAF))
    kd_b = k * (1.0 + (a_b - 1.0) * vrow(V_KAB))
    bonus = head_sum(r * kd_f * vrow(V_RKF) + r * kd_b * vrow(V_RKB)) * v

    def put(i, val):
        out_ref[:, i * dr:(i + 1) * dr] = val.astype(out_ref.dtype)

    put(P_KK, kk)
    put(P_BF, a_f * kk)
    put(P_BB, a_b * kk)
    put(P_KDF, kd_f)
    put(P_KDB, kd_b)
    put(P_G, g)
    put(P_BONUS, bonus)
    lw_ref[:, L_F * dr:(L_F + 1) * dr] = lw_f
    lw_ref[:, L_B * dr:(L_B + 1) * dr] = lw_b


def _prep(rkvl, vecs, wup, aup, gup, ones_bd, *, dr, row_tile):
    bsz, seq = rkvl.shape[0] - 1, rkvl.shape[1]

    def col(c):
        return pl.BlockSpec((None, row_tile, dr), lambda b, i: (b + 1, i, c))

    def full(a):
        return pl.BlockSpec(a.shape, lambda b, i: (0,) * a.ndim)

    return pl.pallas_call(
        functools.partial(_prep_body, dr=dr),
        grid=(bsz, seq // row_tile),
        in_specs=[col(0), col(1), col(2), col(3),
                  full(vecs), full(wup), full(aup), full(gup), full(ones_bd)],
        out_specs=[pl.BlockSpec((None, row_tile, P_NUM * dr), lambda b, i: (b, i, 0)),
                   pl.BlockSpec((None, row_tile, L_NUM * dr), lambda b, i: (b, i, 0))],
        out_shape=[jax.ShapeDtypeStruct((bsz, seq, P_NUM * dr), BF16),
                   jax.ShapeDtypeStruct((bsz, seq, L_NUM * dr), F32)],
        compiler_params=pltpu.CompilerParams(
            dimension_semantics=("arbitrary", "arbitrary"), vmem_limit_bytes=VMEM_LIMIT),
        name="prep",
    )(rkvl, rkvl, rkvl, rkvl, vecs, wup, aup, gup, ones_bd)


def _block_diag(x, lane_head):
    return jnp.concatenate([jnp.where(lane_head == h, x, 0.0) for h in range(QUAD)],
                           axis=0).astype(BF16)


def _cumsum_time(x, *, forward):
    rows, n = x.shape
    tiles = rows // SUBLANES
    x = x.reshape(tiles, SUBLANES, n)
    sub = lax.broadcasted_iota(jnp.int32, (1, SUBLANES, n), 1)
    step = 1
    while step < SUBLANES:
        if forward:
            x = x + jnp.where(sub >= step, pltpu.roll(x, step, 1), 0.0)
        else:
            x = x + jnp.where(sub < SUBLANES - step, pltpu.roll(x, SUBLANES - step, 1), 0.0)
        step *= 2
    edge = SUBLANES - 1 if forward else 0
    parts = [None] * tiles
    carry = None
    for i in (range(tiles) if forward else reversed(range(tiles))):
        parts[i] = x[i:i + 1] if carry is None else x[i:i + 1] + carry
        total = x[i:i + 1, edge:edge + 1, :]
        carry = total if carry is None else carry + total
    return jnp.concatenate(parts, axis=0).reshape(rows, n)


def _decay_factors(lw, *, forward):
    c = CHUNK
    cs = _cumsum_time(lw, forward=forward)
    cs_end = cs[c - 1:c, :] if forward else cs[0:1, :]
    return dict(e_t=jnp.exp(cs), e_prev=jnp.exp(cs - lw), e_inv=jnp.exp(-cs),
                e_end=jnp.exp(cs_end - cs), w_c=jnp.exp(cs_end))


def _scan_step(dirs, st_ref, n_sub):
    c = CHUNK
    t_idx = lax.broadcasted_iota(jnp.int32, (c, MXU_DIM), 0)
    lane = lax.broadcasted_iota(jnp.int32, (c, MXU_DIM), 1)
    s_idx = lane & (c - 1)
    lane_head = lane >> HEAD_BITS
    eye = (s_idx == t_idx).astype(F32)
    bd_mask = ((lax.broadcasted_iota(jnp.int32, (MXU_DIM, MXU_DIM), 0) >> HEAD_BITS)
               == (lax.broadcasted_iota(jnp.int32, (MXU_DIM, MXU_DIM), 1) >> HEAD_BITS))

    def bd(x):
        return _block_diag(x, lane_head)

    subs = []
    for k in range(n_sub):
        chains = []
        for i, (forward, refs, y_ref) in enumerate(dirs):
            row0 = (k if forward else n_sub - 1 - k) * c
            rows = slice(row0, row0 + c)
            a = {name: ref[rows, :].astype(F32) for name, ref in refs.items()}
            f = _decay_factors(a["lw"], forward=forward)
            if forward:
                strict, incl = s_idx < t_idx, s_idx <= t_idx
            else:
                strict, incl = s_idx > t_idx, s_idx >= t_idx
            kap, rt = a["kk"] * f["e_prev"], a["r"] * f["e_t"]
            kt, bt = a["kd"] * f["e_inv"], a["b"] * f["e_inv"]
            kh, bh = a["kd"] * f["e_end"], a["b"] * f["e_end"]
            n_quad = a["r"].shape[-1] // MXU_DIM
            for q in range(n_quad):
                sl = slice(q * MXU_DIM, (q + 1) * MXU_DIM)
                chains.append(dict(
                    strict=strict, incl=incl, y_ref=y_ref, rows=rows, sl=sl, idx=i * n_quad + q,
                    kr=jnp.concatenate([kap[:, sl], rt[:, sl]], axis=0).astype(BF16),
                    kt=kt[:, sl], bt=bt[:, sl], v=a["v"][:, sl],
                    w_col=jnp.broadcast_to(f["w_c"][:, sl], (MXU_DIM // 2, MXU_DIM)).T,
                    lhs_t=jnp.concatenate([kh[:, sl], bh[:, sl]], axis=0).T.astype(BF16)))
        subs.append(chains)
    every = [ch for chains in subs for ch in chains]

    for ch in every:
        g_b = _dot_nt(ch["kr"], bd(ch["bt"]))
        g_k = _dot_nt(ch["kr"], bd(ch["kt"]))
        ch["arb"] = jnp.where(ch["incl"], g_b[c:], 0.0)
        ch["aa"] = jnp.concatenate([jnp.where(ch["strict"], g_k[:c], 0.0),
                                    jnp.where(ch["incl"], g_k[c:], 0.0)], axis=0)
        ch["akb"] = jnp.where(ch["strict"], g_b[:c], 0.0)
    for ch in every:
        ch["av"] = _dot(ch["aa"], bd(ch["v"]))

    base_bits = INV_BASE.bit_length() - 1
    same_base = (t_idx >> base_bits) == (s_idx >> base_bits)
    for ch in every:
        ch["x"] = -jnp.where(same_base, ch["akb"], 0.0)
        ch["t"] = eye + ch["x"]
    for ch in every:
        ch["x"] = _dot(ch["x"], bd(ch["x"]))
    for _ in range(base_bits - 2):
        for ch in every:
            rr = _dot(jnp.concatenate([ch["x"], ch["t"]], axis=0), bd(ch["x"]))
            ch["x"] = rr[:c]
            ch["t"] = ch["t"] + rr[c:]
    for ch in every:
        ch["t"] = ch["t"] + _dot(ch["t"], bd(ch["x"]))
    for bits in range(base_bits, CHUNK.bit_length() - 1):
        pair = ((t_idx >> (bits + 1)) == (s_idx >> (bits + 1))) & ((t_idx >> bits) != (s_idx >> bits))
        for ch in every:
            ch["e"] = _dot(jnp.where(pair, ch["akb"], 0.0), bd(ch["t"]))
        for ch in every:
            ch["t"] = ch["t"] - _dot(ch["t"], bd(ch["e"]))

    for chains in subs:
        for ch in chains:
            ch["m"] = st_ref[ch["idx"]]
            ch["krm"] = _dot(ch["kr"], ch["m"])
        for ch in chains:
            ch["sa"] = _dot(ch["t"], bd(ch["krm"][:c] + ch["av"][:c]))
        for ch in chains:
            sa = ch["sa"]
            y = ch["krm"][c:] + ch["av"][c:] - _dot(ch["arb"], bd(sa))
            ch["y_ref"][ch["rows"], ch["sl"]] = y.astype(ch["y_ref"].dtype)
            upd = _dot(ch["lhs_t"], jnp.concatenate([ch["v"], -sa], axis=0))
            decay = jnp.concatenate([ch["w_col"], ch["w_col"]], axis=1)
            st_ref[ch["idx"]] = decay * ch["m"] + jnp.where(bd_mask, upd, 0.0)


def _scan_body(rf, vf, kkf, bf, kdf, lwf, rb, vb, kkb, bb, kdb, lwb, yf_ref, yb_ref, st_ref,
               *, n_sub):
    @pl.when(pl.program_id(1) == 0)
    def _():
        st_ref[...] = jnp.zeros_like(st_ref)

    def operands(r, v, kk, b, kd, lw):
        return dict(r=r, v=v, kk=kk, b=b, kd=kd, lw=lw)

    _scan_step([(True, operands(rf, vf, kkf, bf, kdf, lwf), yf_ref),
                (False, operands(rb, vb, kkb, bb, kdb, lwb), yb_ref)], st_ref, n_sub)


def _scan(rkvl, prep, logw, *, dr):
    bsz, seq, _ = prep.shape
    rows = SCAN_SUB * CHUNK
    assert seq % rows == 0
    ns = seq // rows
    n_quad = dr // MXU_DIM

    def fwd(col, slot0=0):
        return pl.BlockSpec((None, rows, dr), lambda b, c: (b + slot0, c, col))

    def bwd(col, slot0=0):
        return pl.BlockSpec((None, rows, dr), lambda b, c: (b + slot0, ns - 1 - c, col))

    return pl.pallas_call(
        functools.partial(_scan_body, n_sub=SCAN_SUB),
        grid=(bsz, ns),
        in_specs=[fwd(0, 1), fwd(2, 1), fwd(P_KK), fwd(P_BF), fwd(P_KDF), fwd(L_F),
                  bwd(0, 1), bwd(2, 1), bwd(P_KK), bwd(P_BB), bwd(P_KDB), bwd(L_B)],
        out_specs=[pl.BlockSpec((None, rows, dr), lambda b, c: (b, c, 0)),
                   pl.BlockSpec((None, rows, dr), lambda b, c: (b, ns - 1 - c, 0))],
        out_shape=[jax.ShapeDtypeStruct((bsz, seq, dr), BF16)] * 2,
        scratch_shapes=[pltpu.VMEM((2 * n_quad, MXU_DIM, MXU_DIM), F32)],
        compiler_params=pltpu.CompilerParams(
            dimension_semantics=("arbitrary", "arbitrary"), vmem_limit_bytes=VMEM_LIMIT),
        name="scan",
    )(rkvl, rkvl, prep, prep, prep, logw, rkvl, rkvl, prep, prep, prep, logw)


def _post_ffn_body(yf_ref, yb_ref, g_ref, bonus_ref, oconv_ref, x_ref, wout_ref, gn_ref, ones_ref,
                   n2_ref, wg_ref, wu_ref, wd_ref, nf_ref, out_ref, *, dr, final_norm):
    ones_bd = ones_ref[...]
    y = yf_ref[...].astype(F32) + yb_ref[...].astype(F32)
    mean = jnp.dot(y.astype(BF16), ones_bd, preferred_element_type=F32) * (1.0 / HEAD)
    d = y - mean
    var = jnp.dot((d * d).astype(BF16), ones_bd, preferred_element_type=F32) * (1.0 / HEAD)
    yn = (d * lax.rsqrt(var + GN_EPS) * gn_ref[0:1, :] + gn_ref[1:2, :]
          + bonus_ref[...].astype(F32))
    o = (yn * g_ref[...].astype(F32)).astype(BF16)
    x1 = (x_ref[...]
          + jnp.dot(o, wout_ref[0:dr, :], preferred_element_type=F32)
          + jnp.dot(oconv_ref[...], wout_ref[dr:, :], preferred_element_type=F32))

    h = _rmsnorm(x1, n2_ref[...]).astype(BF16)
    a = jnp.dot(h, wg_ref[...], preferred_element_type=F32)
    u = jnp.dot(h, wu_ref[...], preferred_element_type=F32)
    z = (a * jax.nn.sigmoid(a) * u).astype(BF16)
    x2 = x1 + jnp.dot(z, wd_ref[...], preferred_element_type=F32)
    out_ref[...] = _rmsnorm(x2, nf_ref[...]) if final_norm else x2


def _post_ffn(yf, yb, prep, oconv, x, wout, gn, ones_bd, n2, wg, wu, wd, nf, *, dr, final_norm,
              row_tile):
    n, d = x.shape

    def rows(width, col=0):
        return pl.BlockSpec((row_tile, width), lambda i: (i, col))

    def full(a):
        return pl.BlockSpec(a.shape, lambda i: (0,) * a.ndim, pipeline_mode=pl.Buffered(1))

    return pl.pallas_call(
        functools.partial(_post_ffn_body, dr=dr, final_norm=final_norm),
        grid=(n // row_tile,),
        in_specs=[rows(dr), rows(dr), rows(dr, P_G), rows(dr, P_BONUS), rows(oconv.shape[-1]),
                  rows(d), full(wout), full(gn), full(ones_bd),
                  full(n2), full(wg), full(wu), full(wd), full(nf)],
        out_specs=rows(d),
        out_shape=jax.ShapeDtypeStruct((n, d), F32),
        compiler_params=pltpu.CompilerParams(
            dimension_semantics=("arbitrary",), vmem_limit_bytes=VMEM_LIMIT),
        name="post_ffn",
    )(yf, yb, prep, prep, oconv, x, wout, gn, ones_bd, n2, wg, wu, wd, nf)


def _pad_cols(a, width):
    return jnp.pad(a, ((0, 0), (0, width - a.shape[1])))


def _pad_rows(a, height):
    return jnp.pad(a, ((0, height - a.shape[0]), (0, 0)))


def kernel(x, norm1_w, w_in, mu_shift, w_up_f, w0_f, w_up_b, w0_b, a_up_f, a0_f, a_up_b, a0_b,
           g_up, k_k, k_a_f, k_a_b, r_k_f, r_k_b, gn_w, gn_b, conv_w, w_out, norm2_w, w_gate,
           w_up, w_down, norm_f_w):
    bsz, seq, d = x.shape
    depth = w_in.shape[0]
    dr = w0_f.shape[-1]
    n_dec, n_aaa, n_gate = w_up_f.shape[1], a_up_f.shape[1], g_up.shape[1]
    d_conv = conv_w.shape[-1]
    assert dr % MXU_DIM == 0 and d_conv % COL_TILE == 0 and seq % CHUNK == 0
    assert n_dec + n_aaa <= LANES and n_gate <= COL_TILE
    o_xw = 3 * dr
    o_xa = o_xw + n_dec
    o_xg = o_xa + n_aaa
    o_conv = o_xg + n_gate

    head_id = jnp.arange(dr) // HEAD
    ones_bd = (head_id[:, None] == head_id[None, :]).astype(BF16)

    for l in range(depth):
        w = w_in[l].astype(BF16)
        w_shift = jnp.concatenate(
            [w[:, :o_xw], _pad_cols(w[:, o_xw:o_xg], COL_TILE), _pad_cols(w[:, o_xg:o_conv], COL_TILE)],
            axis=1)
        w_conv = w[:, o_conv:]
        mu = mu_shift[l][None, :]
        mu_p = jnp.concatenate(
            [mu[:, :o_xw], _pad_cols(mu[:, o_xw:o_xg], COL_TILE), _pad_cols(mu[:, o_xg:o_conv], COL_TILE)],
            axis=1)

        rkvl, oconv = _inproj(x, norm1_w[l][None, :], w_shift, mu_p, w_conv, conv_w[l])

        vecs = jnp.stack([w0_f[l], w0_b[l], a0_f[l], a0_b[l], k_k[l], k_a_f[l], k_a_b[l],
                          r_k_f[l].reshape(dr), r_k_b[l].reshape(dr)], axis=0)
        vecs = _pad_rows(vecs, 2 * SUBLANES)
        wup = _pad_rows(jnp.concatenate([w_up_f[l], w_up_b[l]], axis=1), LANES).astype(BF16)
        aup = jnp.concatenate([a_up_f[l], a_up_b[l]], axis=1)
        aup = jnp.pad(aup, ((n_dec, LANES - n_dec - n_aaa), (0, 0))).astype(BF16)
        gup = _pad_rows(g_up[l], COL_TILE).astype(BF16)
        prep, logw = _prep(rkvl, vecs, wup, aup, gup, ones_bd, dr=dr, row_tile=min(1024, seq))

        yf, yb = _scan(rkvl, prep, logw, dr=dr)

        gn = jnp.stack([gn_w[l], gn_b[l]], axis=0)
        n_tok = bsz * seq
        x = _post_ffn(yf.reshape(n_tok, dr), yb.reshape(n_tok, dr), prep.reshape(n_tok, P_NUM * dr),
                      oconv.reshape(n_tok, d_conv), x.reshape(n_tok, d), w_out[l].astype(BF16), gn,
                      ones_bd, norm2_w[l][None, :], w_gate[l].astype(BF16), w_up[l].astype(BF16),
                      w_down[l].astype(BF16), norm_f_w[None, :], dr=dr,
                      final_norm=(l == depth - 1), row_tile=512).reshape(bsz, seq, d)
    return x
```

```python
import functools

import jax
import jax.numpy as jnp
from jax import lax
from jax.experimental import pallas as pl
from jax.experimental.pallas import tpu as pltpu

F32 = jnp.float32
BF16 = jnp.bfloat16

HEAD = 64
CHUNK = 64
INV_BASE = 8
MXU_DIM = 256
QUAD = MXU_DIM // CHUNK
SCAN_SUB = 8
A_GROUP = 4
COL_TILE = 256
SHIFT_TILE = 512
ROW_CHUNK = 512
SUBLANES = 8
LANES = 128
HALO = SUBLANES
HEAD_BITS = HEAD.bit_length() - 1
assert CHUNK == HEAD
LOG_DECAY_SCALE = 0.606531
RMS_EPS = 1e-6
GN_EPS = 64e-5
NORM_EPS = 1e-12
VMEM_LIMIT = 56 * 1024 * 1024

P_KK, P_BF, P_BB, P_KDF, P_KDB, P_G, P_BONUS, P_NUM = range(8)
L_F, L_B, L_NUM = range(3)
V_W0F, V_W0B, V_A0F, V_A0B, V_KK, V_KAF, V_KAB, V_RKF, V_RKB, V_NUM = range(10)


def _dot(a, b):
    return jnp.dot(a.astype(BF16), b.astype(BF16), preferred_element_type=F32)


def _dot_nt(a, b):
    return lax.dot_general(a.astype(BF16), b.astype(BF16), (((1,), (1,)), ((), ())),
                           preferred_element_type=F32)


def _rmsnorm(x, w):
    ms = jnp.mean(x * x, axis=-1, keepdims=True)
    return x * lax.rsqrt(ms + RMS_EPS) * w


def _zero_halo(s_ref, seq):
    z = jnp.zeros((HALO, s_ref.shape[1]), s_ref.dtype)
    s_ref[0:HALO, :] = z
    s_ref[HALO + seq:2 * HALO + seq, :] = z


def _windows(s_ref, c0, rows):
    width = s_ref.shape[-1]
    win = s_ref[c0:c0 + rows + 2 * HALO, :].reshape(rows // HALO + 2, HALO, width)
    down = pltpu.roll(win, 1, 1)
    up = pltpu.roll(win, HALO - 1, 1)
    sub = lax.broadcasted_iota(jnp.int32, (1, HALO, width), 1)
    prv = jnp.where(sub == 0, down[:-2], down[1:-1])
    nxt = jnp.where(sub == HALO - 1, up[2:], up[1:-1])
    return tuple(a.reshape(rows, width) for a in (win[1:-1], prv, nxt))


def _shift_body(x_ref, n1_ref, w_ref, mu_ref, out_ref, hout_ref, h_s, p_s, *, seq):
    bb = pl.program_id(0)
    j = pl.program_id(1)
    new = lax.rem(bb, 2)
    old = 1 - new

    @pl.when((bb == 0) & (j == 0))
    def _():
        _zero_halo(p_s, seq)
        h_s[1] = jnp.zeros(h_s.shape[1:], h_s.dtype)

    n_rows = x_ref.shape[0]
    h = _rmsnorm(x_ref[...], n1_ref[...]).astype(BF16)
    h_s[new, pl.ds(pl.multiple_of(j * n_rows, n_rows), n_rows), :] = h
    hout_ref[...] = h

    w = w_ref[...]
    mu = mu_ref[...]
    keep, mix = 1.0 - mu, 0.5 * mu

    def shift(c0):
        cur, prv, nxt = _windows(p_s, c0, ROW_CHUNK)
        out_ref[c0:c0 + ROW_CHUNK, :] = (keep * cur + mix * (prv + nxt)).astype(out_ref.dtype)

    for c0 in range(0, seq, ROW_CHUNK):
        p_s[HALO + c0:HALO + c0 + ROW_CHUNK, :] = jnp.dot(
            h_s[old, c0:c0 + ROW_CHUNK, :], w, preferred_element_type=F32)
        if c0:
            shift(c0 - ROW_CHUNK)
    shift(seq - ROW_CHUNK)


def _conv_body(h_ref, wb_ref, wc_ref, wh_ref, cw_ref, out_ref, u_s, gb_s, *, seq):
    @pl.when(pl.program_id(1) == 0)
    def _():
        _zero_halo(u_s, seq)

    w = jnp.concatenate([wb_ref[...], wc_ref[...], wh_ref[...]], axis=1)
    cw = cw_ref[...]

    def conv(c0):
        cur, prv, nxt = _windows(u_s, c0, ROW_CHUNK)
        y = cw[0:1, :] * prv + cw[1:2, :] * cur + cw[2:3, :] * nxt
        out_ref[c0:c0 + ROW_CHUNK, :] = (gb_s[c0:c0 + ROW_CHUNK, :] * y).astype(out_ref.dtype)

    for c0 in range(0, seq, ROW_CHUNK):
        p = jnp.dot(h_ref[c0:c0 + ROW_CHUNK, :], w, preferred_element_type=F32)
        gb_s[c0:c0 + ROW_CHUNK, :] = p[:, :COL_TILE]
        u_s[HALO + c0:HALO + c0 + ROW_CHUNK, :] = p[:, COL_TILE:2 * COL_TILE] * p[:, 2 * COL_TILE:]
        if c0:
            conv(c0 - ROW_CHUNK)
    conv(seq - ROW_CHUNK)


def _inproj(x, n1, w_shift, mu_p, w_conv, conv_w):
    bsz, seq, d = x.shape
    assert seq % ROW_CHUNK == 0
    params = pltpu.CompilerParams(
        dimension_semantics=("arbitrary", "arbitrary"), vmem_limit_bytes=VMEM_LIMIT)
    n_tiles = w_shift.shape[1] // SHIFT_TILE
    assert seq % n_tiles == 0
    norm_rows = seq // n_tiles

    def nxt(b):
        return jnp.minimum(b, bsz - 1)

    rkvl, h = pl.pallas_call(
        functools.partial(_shift_body, seq=seq),
        grid=(bsz + 1, n_tiles),
        in_specs=[
            pl.BlockSpec((None, norm_rows, d), lambda b, j: (nxt(b), j, 0)),
            pl.BlockSpec((1, d), lambda b, j: (0, 0)),
            pl.BlockSpec((d, SHIFT_TILE), lambda b, j: (0, j)),
            pl.BlockSpec((1, SHIFT_TILE), lambda b, j: (0, j)),
        ],
        out_specs=[
            pl.BlockSpec((None, seq, SHIFT_TILE), lambda b, j: (b, 0, j)),
            pl.BlockSpec((None, norm_rows, d), lambda b, j: (b, j, 0)),
        ],
        out_shape=[
            jax.ShapeDtypeStruct((bsz + 1, seq, w_shift.shape[1]), BF16),
            jax.ShapeDtypeStruct((bsz + 1, seq, d), BF16),
        ],
        scratch_shapes=[pltpu.VMEM((2, seq, d), BF16),
                        pltpu.VMEM((seq + 2 * HALO, SHIFT_TILE), F32)],
        compiler_params=params,
        name="inproj_shift",
    )(x, n1, w_shift, mu_p)
    d_conv = conv_w.shape[-1]
    n_conv = d_conv // COL_TILE
    oconv = pl.pallas_call(
        functools.partial(_conv_body, seq=seq),
        grid=(bsz, n_conv),
        in_specs=[
            pl.BlockSpec((None, seq, d), lambda b, q: (b, 0, 0)),
            pl.BlockSpec((d, COL_TILE), lambda b, q: (0, q)),
            pl.BlockSpec((d, COL_TILE), lambda b, q: (0, n_conv + q)),
            pl.BlockSpec((d, COL_TILE), lambda b, q: (0, 2 * n_conv + q)),
            pl.BlockSpec((3, COL_TILE), lambda b, q: (0, q)),
        ],
        out_specs=pl.BlockSpec((None, seq, COL_TILE), lambda b, q: (b, 0, q)),
        out_shape=jax.ShapeDtypeStruct((bsz, seq, d_conv), BF16),
        scratch_shapes=[pltpu.VMEM((seq + 2 * HALO, COL_TILE), F32),
                        pltpu.VMEM((seq, COL_TILE), F32)],
        compiler_params=params,
        name="inproj_conv",
    )(h, w_conv, w_conv, w_conv, conv_w)
    return rkvl, oconv


def _prep_body(r_ref, k_ref, v_ref, l_ref, vec_ref, wup_ref, aup_ref, gup_ref, ones_ref,
               out_ref, lw_ref, *, dr):
    r = r_ref[...].astype(F32)
    k = k_ref[...].astype(F32)
    v = v_ref[...].astype(F32)
    vec = vec_ref[...]
    ones_bd = ones_ref[...]

    def vrow(i):
        return vec[i:i + 1, :]

    t128 = l_ref[:, 0:LANES]
    lw = jnp.dot(jnp.tanh(t128.astype(F32)).astype(BF16), wup_ref[...],
                 preferred_element_type=F32)
    aa = jnp.dot(t128, aup_ref[...], preferred_element_type=F32)
    lw_f = -LOG_DECAY_SCALE * jax.nn.sigmoid(vrow(V_W0F) + lw[:, :dr])
    lw_b = -LOG_DECAY_SCALE * jax.nn.sigmoid(vrow(V_W0B) + lw[:, dr:])
    a_f = jax.nn.sigmoid(vrow(V_A0F) + aa[:, :dr])
    a_b = jax.nn.sigmoid(vrow(V_A0B) + aa[:, dr:])
    g = jnp.dot(jax.nn.sigmoid(l_ref[:, COL_TILE:2 * COL_TILE].astype(F32)).astype(BF16), gup_ref[...],
                preferred_element_type=F32)

    def head_sum(t):
        return jnp.dot(t.astype(BF16), ones_bd, preferred_element_type=F32)

    kkr = k * vrow(V_KK)
    kk = kkr * lax.rsqrt(jnp.maximum(head_sum(kkr * kkr), NORM_EPS * NORM_EPS))
    kd_f = k * (1.0 + (a_f - 1.0) * vrow(V_KAF))
    kd_b = k * (1.0 + (a_b - 1.0) * vrow(V_KAB))
    bonus = head_sum(r * kd_f * vrow(V_RKF) + r * kd_b * vrow(V_RKB)) * v

    def put(i, val):
        out_ref[:, i * dr:(i + 1) * dr] = val.astype(out_ref.dtype)

    put(P_KK, kk)
    put(P_BF, a_f * kk)
    put(P_BB, a_b * kk)
    put(P_KDF, kd_f)
    put(P_KDB, kd_b)
    put(P_G, g)
    put(P_BONUS, bonus)
    lw_ref[:, L_F * dr:(L_F + 1) * dr] = lw_f
    lw_ref[:, L_B * dr:(L_B + 1) * dr] = lw_b


def _prep(rkvl, vecs, wup, aup, gup, ones_bd, *, dr, row_tile):
    bsz, seq = rkvl.shape[0] - 1, rkvl.shape[1]

    def col(c):
        return pl.BlockSpec((None, row_tile, dr), lambda b, i: (b + 1, i, c))

    def full(a):
        return pl.BlockSpec(a.shape, lambda b, i: (0,) * a.ndim)

    return pl.pallas_call(
        functools.partial(_prep_body, dr=dr),
        grid=(bsz, seq // row_tile),
        in_specs=[col(0), col(1), col(2), col(3),
                  full(vecs), full(wup), full(aup), full(gup), full(ones_bd)],
        out_specs=[pl.BlockSpec((None, row_tile, P_NUM * dr), lambda b, i: (b, i, 0)),
                   pl.BlockSpec((None, row_tile, L_NUM * dr), lambda b, i: (b, i, 0))],
        out_shape=[jax.ShapeDtypeStruct((bsz, seq, P_NUM * dr), BF16),
                   jax.ShapeDtypeStruct((bsz, seq, L_NUM * dr), F32)],
        compiler_params=pltpu.CompilerParams(
            dimension_semantics=("arbitrary", "arbitrary"), vmem_limit_bytes=VMEM_LIMIT),
        name="prep",
    )(rkvl, rkvl, rkvl, rkvl, vecs, wup, aup, gup, ones_bd)


def _block_diag(x, lane_head):
    return jnp.concatenate([jnp.where(lane_head == h, x, 0.0) for h in range(QUAD)],
                           axis=0).astype(BF16)


def _cumsum_time(x, *, forward):
    rows, n = x.shape
    tiles = rows // SUBLANES
    x = x.reshape(tiles, SUBLANES, n)
    sub = lax.broadcasted_iota(jnp.int32, (1, SUBLANES, n), 1)
    step = 1
    while step < SUBLANES:
        if forward:
            x = x + jnp.where(sub >= step, pltpu.roll(x, step, 1), 0.0)
        else:
            x = x + jnp.where(sub < SUBLANES - step, pltpu.roll(x, SUBLANES - step, 1), 0.0)
        step *= 2
    edge = SUBLANES - 1 if forward else 0
    parts = [None] * tiles
    carry = None
    for i in (range(tiles) if forward else reversed(range(tiles))):
        parts[i] = x[i:i + 1] if carry is None else x[i:i + 1] + carry
        total = x[i:i + 1, edge:edge + 1, :]
        carry = total if carry is None else carry + total
    return jnp.concatenate(parts, axis=0).reshape(rows, n)


def _decay_factors(lw, *, forward):
    c = CHUNK
    cs = _cumsum_time(lw, forward=forward)
    cs_end = cs[c - 1:c, :] if forward else cs[0:1, :]
    return dict(e_t=jnp.exp(cs), e_prev=jnp.exp(cs - lw), e_inv=jnp.exp(-cs),
                e_end=jnp.exp(cs_end - cs), w_c=jnp.exp(cs_end))


def _scan_step(dirs, st_ref, n_sub):
    c = CHUNK
    t_idx = lax.broadcasted_iota(jnp.int32, (c, MXU_DIM), 0)
    lane = lax.broadcasted_iota(jnp.int32, (c, MXU_DIM), 1)
    s_idx = lane & (c - 1)
    lane_head = lane >> HEAD_BITS
    eye = (s_idx == t_idx).astype(F32)
    bd_mask = ((lax.broadcasted_iota(jnp.int32, (MXU_DIM, MXU_DIM), 0) >> HEAD_BITS)
               == (lax.broadcasted_iota(jnp.int32, (MXU_DIM, MXU_DIM), 1) >> HEAD_BITS))

    def bd(x):
        return _block_diag(x, lane_head)

    subs = []
    for k in range(n_sub):
        chains = []
        for i, (forward, refs, y_ref) in enumerate(dirs):
            row0 = (k if forward else n_sub - 1 - k) * c
            rows = slice(row0, row0 + c)
            a = {name: ref[rows, :].astype(F32) for name, ref in refs.items()}
            f = _decay_factors(a["lw"], forward=forward)
            if forward:
                strict, incl = s_idx < t_idx, s_idx <= t_idx
            else:
                strict, incl = s_idx > t_idx, s_idx >= t_idx
            kap, rt = a["kk"] * f["e_prev"], a["r"] * f["e_t"]
            kt, bt = a["kd"] * f["e_inv"], a["b"] * f["e_inv"]
            kh, bh = a["kd"] * f["e_end"], a["b"] * f["e_end"]
            n_quad = a["r"].shape[-1] // MXU_DIM
            for q in range(n_quad):
                sl = slice(q * MXU_DIM, (q + 1) * MXU_DIM)
                chains.append(dict(
                    strict=strict, incl=incl, y_ref=y_ref, rows=rows, sl=sl, idx=i * n_quad + q,
                    kr=jnp.concatenate([kap[:, sl], rt[:, sl]], axis=0).astype(BF16),
                    kt=kt[:, sl], bt=bt[:, sl], v=a["v"][:, sl],
                    w_col=jnp.broadcast_to(f["w_c"][:, sl], (MXU_DIM // 2, MXU_DIM)).T,
                    lhs_t=jnp.concatenate([kh[:, sl], bh[:, sl]], axis=0).T.astype(BF16)))
        subs.append(chains)
    base_bits = INV_BASE.bit_length() - 1
    same_base = (t_idx >> base_bits) == (s_idx >> base_bits)
    for g0 in range(0, n_sub, A_GROUP):
        every = [ch for chains in subs[g0:g0 + A_GROUP] for ch in chains]

        for ch in every:
            g_b = _dot_nt(ch["kr"], bd(ch["bt"]))
            g_k = _dot_nt(ch["kr"], bd(ch["kt"]))
            ch["arb"] = jnp.where(ch["incl"], g_b[c:], 0.0)
            ch["aa"] = jnp.concatenate([jnp.where(ch["strict"], g_k[:c], 0.0),
                                        jnp.where(ch["incl"], g_k[c:], 0.0)], axis=0)
            ch["akb"] = jnp.where(ch["strict"], g_b[:c], 0.0)
        for ch in every:
            ch["av"] = _dot(ch["aa"], bd(ch["v"]))

        for ch in every:
            ch["x"] = -jnp.where(same_base, ch["akb"], 0.0)
            ch["t"] = eye + ch["x"]
        for ch in every:
            ch["x"] = _dot(ch["x"], bd(ch["x"]))
        for _ in range(base_bits - 2):
            for ch in every:
                rr = _dot(jnp.concatenate([ch["x"], ch["t"]], axis=0), bd(ch["x"]))
                ch["x"] = rr[:c]
                ch["t"] = ch["t"] + rr[c:]
        for ch in every:
            ch["t"] = ch["t"] + _dot(ch["t"], bd(ch["x"]))
        for bits in range(base_bits, CHUNK.bit_length() - 1):
            pair = ((t_idx >> (bits + 1)) == (s_idx >> (bits + 1))) & ((t_idx >> bits) != (s_idx >> bits))
            for ch in every:
                ch["e"] = _dot(jnp.where(pair, ch["akb"], 0.0), bd(ch["t"]))
            for ch in every:
                ch["t"] = ch["t"] - _dot(ch["t"], bd(ch["e"]))

    for chains in subs:
        for ch in chains:
            ch["m"] = st_ref[ch["idx"]]
            ch["krm"] = _dot(ch["kr"], ch["m"])
        for ch in chains:
            ch["sa"] = _dot(ch["t"], bd(ch["krm"][:c] + ch["av"][:c]))
        for ch in chains:
            sa = ch["sa"]
            y = ch["krm"][c:] + ch["av"][c:] - _dot(ch["arb"], bd(sa))
            ch["y_ref"][ch["rows"], ch["sl"]] = y.astype(ch["y_ref"].dtype)
            upd = _dot(ch["lhs_t"], jnp.concatenate([ch["v"], -sa], axis=0))
            decay = jnp.concatenate([ch["w_col"], ch["w_col"]], axis=1)
            st_ref[ch["idx"]] = decay * ch["m"] + jnp.where(bd_mask, upd, 0.0)


def _scan_body(rf, vf, kkf, bf, kdf, lwf, rb, vb, kkb, bb, kdb, lwb, yf_ref, yb_ref, st_ref,
               *, n_sub):
    @pl.when(pl.program_id(1) == 0)
    def _():
        st_ref[...] = jnp.zeros_like(st_ref)

    def operands(r, v, kk, b, kd, lw):
        return dict(r=r, v=v, kk=kk, b=b, kd=kd, lw=lw)

    _scan_step([(True, operands(rf, vf, kkf, bf, kdf, lwf), yf_ref),
                (False, operands(rb, vb, kkb, bb, kdb, lwb), yb_ref)], st_ref, n_sub)


def _scan(rkvl, prep, logw, *, dr):
    bsz, seq, _ = prep.shape
    rows = SCAN_SUB * CHUNK
    assert seq % rows == 0
    ns = seq // rows
    n_quad = dr // MXU_DIM

    def fwd(col, slot0=0):
        return pl.BlockSpec((None, rows, dr), lambda b, c: (b + slot0, c, col))

    def bwd(col, slot0=0):
        return pl.BlockSpec((None, rows, dr), lambda b, c: (b + slot0, ns - 1 - c, col))

    return pl.pallas_call(
        functools.partial(_scan_body, n_sub=SCAN_SUB),
        grid=(bsz, ns),
        in_specs=[fwd(0, 1), fwd(2, 1), fwd(P_KK), fwd(P_BF), fwd(P_KDF), fwd(L_F),
                  bwd(0, 1), bwd(2, 1), bwd(P_KK), bwd(P_BB), bwd(P_KDB), bwd(L_B)],
        out_specs=[pl.BlockSpec((None, rows, dr), lambda b, c: (b, c, 0)),
                   pl.BlockSpec((None, rows, dr), lambda b, c: (b, ns - 1 - c, 0))],
        out_shape=[jax.ShapeDtypeStruct((bsz, seq, dr), BF16)] * 2,
        scratch_shapes=[pltpu.VMEM((2 * n_quad, MXU_DIM, MXU_DIM), F32)],
        compiler_params=pltpu.CompilerParams(
            dimension_semantics=("arbitrary", "arbitrary"), vmem_limit_bytes=VMEM_LIMIT),
        name="scan",
    )(rkvl, rkvl, prep, prep, prep, logw, rkvl, rkvl, prep, prep, prep, logw)


def _post_ffn_body(yf_ref, yb_ref, g_ref, bonus_ref, oconv_ref, x_ref, wout_ref, gn_ref, ones_ref,
                   n2_ref, wg_ref, wu_ref, wd_ref, nf_ref, out_ref, *, dr, final_norm):
    ones_bd = ones_ref[...]
    y = yf_ref[...].astype(F32) + yb_ref[...].astype(F32)
    mean = jnp.dot(y.astype(BF16), ones_bd, preferred_element_type=F32) * (1.0 / HEAD)
    d = y - mean
    var = jnp.dot((d * d).astype(BF16), ones_bd, preferred_element_type=F32) * (1.0 / HEAD)
    yn = (d * lax.rsqrt(var + GN_EPS) * gn_ref[0:1, :] + gn_ref[1:2, :]
          + bonus_ref[...].astype(F32))
    o = (yn * g_ref[...].astype(F32)).astype(BF16)
    x1 = (x_ref[...]
          + jnp.dot(o, wout_ref[0:dr, :], preferred_element_type=F32)
          + jnp.dot(oconv_ref[...], wout_ref[dr:, :], preferred_element_type=F32))

    h = _rmsnorm(x1, n2_ref[...]).astype(BF16)
    a = jnp.dot(h, wg_ref[...], preferred_element_type=F32)
    u = jnp.dot(h, wu_ref[...], preferred_element_type=F32)
    z = (a * jax.nn.sigmoid(a) * u).astype(BF16)
    x2 = x1 + jnp.dot(z, wd_ref[...], preferred_element_type=F32)
    out_ref[...] = _rmsnorm(x2, nf_ref[...]) if final_norm else x2


def _post_ffn(yf, yb, prep, oconv, x, wout, gn, ones_bd, n2, wg, wu, wd, nf, *, dr, final_norm,
              row_tile):
    n, d = x.shape

    def rows(width, col=0):
        return pl.BlockSpec((row_tile, width), lambda i: (i, col))

    def full(a):
        return pl.BlockSpec(a.shape, lambda i: (0,) * a.ndim, pipeline_mode=pl.Buffered(1))

    return pl.pallas_call(
        functools.partial(_post_ffn_body, dr=dr, final_norm=final_norm),
        grid=(n // row_tile,),
        in_specs=[rows(dr), rows(dr), rows(dr, P_G), rows(dr, P_BONUS), rows(oconv.shape[-1]),
                  rows(d), full(wout), full(gn), full(ones_bd),
                  full(n2), full(wg), full(wu), full(wd), full(nf)],
        out_specs=rows(d),
        out_shape=jax.ShapeDtypeStruct((n, d), F32),
        compiler_params=pltpu.CompilerParams(
            dimension_semantics=("arbitrary",), vmem_limit_bytes=VMEM_LIMIT),
        name="post_ffn",
    )(yf, yb, prep, prep, oconv, x, wout, gn, ones_bd, n2, wg, wu, wd, nf)


def _pad_cols(a, width):
    return jnp.pad(a, ((0, 0), (0, width - a.shape[1])))


def _pad_rows(a, height):
    return jnp.pad(a, ((0, height - a.shape[0]), (0, 0)))


def kernel(x, norm1_w, w_in, mu_shift, w_up_f, w0_f, w_up_b, w0_b, a_up_f, a0_f, a_up_b, a0_b,
           g_up, k_k, k_a_f, k_a_b, r_k_f, r_k_b, gn_w, gn_b, conv_w, w_out, norm2_w, w_gate,
           w_up, w_down, norm_f_w):
    bsz, seq, d = x.shape
    depth = w_in.shape[0]
    dr = w0_f.shape[-1]
    n_dec, n_aaa, n_gate = w_up_f.shape[1], a_up_f.shape[1], g_up.shape[1]
    d_conv = conv_w.shape[-1]
    assert dr % MXU_DIM == 0 and d_conv % COL_TILE == 0 and seq % CHUNK == 0
    assert n_dec + n_aaa <= LANES and n_gate <= COL_TILE
    o_xw = 3 * dr
    o_xa = o_xw + n_dec
    o_xg = o_xa + n_aaa
    o_conv = o_xg + n_gate

    head_id = jnp.arange(dr) // HEAD
    ones_bd = (head_id[:, None] == head_id[None, :]).astype(BF16)

    for l in range(depth):
        w = w_in[l].astype(BF16)
        w_shift = jnp.concatenate(
            [w[:, :o_xw], _pad_cols(w[:, o_xw:o_xg], COL_TILE), _pad_cols(w[:, o_xg:o_conv], COL_TILE)],
            axis=1)
        w_conv = w[:, o_conv:]
        mu = mu_shift[l][None, :]
        mu_p = jnp.concatenate(
            [mu[:, :o_xw], _pad_cols(mu[:, o_xw:o_xg], COL_TILE), _pad_cols(mu[:, o_xg:o_conv], COL_TILE)],
            axis=1)

        rkvl, oconv = _inproj(x, norm1_w[l][None, :], w_shift, mu_p, w_conv, conv_w[l])

        vecs = jnp.stack([w0_f[l], w0_b[l], a0_f[l], a0_b[l], k_k[l], k_a_f[l], k_a_b[l],
                          r_k_f[l].reshape(dr), r_k_b[l].reshape(dr)], axis=0)
        vecs = _pad_rows(vecs, 2 * SUBLANES)
        wup = _pad_rows(jnp.concatenate([w_up_f[l], w_up_b[l]], axis=1), LANES).astype(BF16)
        aup = jnp.concatenate([a_up_f[l], a_up_b[l]], axis=1)
        aup = jnp.pad(aup, ((n_dec, LANES - n_dec - n_aaa), (0, 0))).astype(BF16)
        gup = _pad_rows(g_up[l], COL_TILE).astype(BF16)
        prep, logw = _prep(rkvl, vecs, wup, aup, gup, ones_bd, dr=dr, row_tile=min(1024, seq))

        yf, yb = _scan(rkvl, prep, logw, dr=dr)

        gn = jnp.stack([gn_w[l], gn_b[l]], axis=0)
        n_tok = bsz * seq
        x = _post_ffn(yf.reshape(n_tok, dr), yb.reshape(n_tok, dr), prep.reshape(n_tok, P_NUM * dr),
                      oconv.reshape(n_tok, d_conv), x.reshape(n_tok, d), w_out[l].astype(BF16), gn,
                      ones_bd, norm2_w[l][None, :], w_gate[l].astype(BF16), w_up[l].astype(BF16),
                      w_down[l].astype(BF16), norm_f_w[None, :], dr=dr,
                      final_norm=(l == depth - 1), row_tile=512).reshape(bsz, seq, d)
    return x
```

```python
import functools

import jax
import jax.numpy as jnp
from jax import lax
from jax.experimental import pallas as pl
from jax.experimental.pallas import tpu as pltpu

F32 = jnp.float32
BF16 = jnp.bfloat16

HEAD = 64
CHUNK = 64
INV_BASE = 8
MXU_DIM = 256
QUAD = MXU_DIM // CHUNK
SCAN_SUB = 8
A_GROUP = 2
COL_TILE = 256
SHIFT_TILE = 512
ROW_CHUNK = 512
SUBLANES = 8
LANES = 128
HALO = SUBLANES
HEAD_BITS = HEAD.bit_length() - 1
assert CHUNK == HEAD
LOG_DECAY_SCALE = 0.606531
RMS_EPS = 1e-6
GN_EPS = 64e-5
NORM_EPS = 1e-12
VMEM_LIMIT = 56 * 1024 * 1024

P_KK, P_BF, P_BB, P_KDF, P_KDB, P_G, P_BONUS, P_NUM = range(8)
L_F, L_B, L_NUM = range(3)
V_W0F, V_W0B, V_A0F, V_A0B, V_KK, V_KAF, V_KAB, V_RKF, V_RKB, V_NUM = range(10)


def _dot(a, b):
    return jnp.dot(a.astype(BF16), b.astype(BF16), preferred_element_type=F32)


def _dot_nt(a, b):
    return lax.dot_general(a.astype(BF16), b.astype(BF16), (((1,), (1,)), ((), ())),
                           preferred_element_type=F32)


def _rmsnorm(x, w):
    ms = jnp.mean(x * x, axis=-1, keepdims=True)
    return x * lax.rsqrt(ms + RMS_EPS) * w


def _zero_halo(s_ref, seq):
    z = jnp.zeros((HALO, s_ref.shape[1]), s_ref.dtype)
    s_ref[0:HALO, :] = z
    s_ref[HALO + seq:2 * HALO + seq, :] = z


def _windows(s_ref, c0, rows):
    width = s_ref.shape[-1]
    win = s_ref[c0:c0 + rows + 2 * HALO, :].reshape(rows // HALO + 2, HALO, width)
    down = pltpu.roll(win, 1, 1)
    up = pltpu.roll(win, HALO - 1, 1)
    sub = lax.broadcasted_iota(jnp.int32, (1, HALO, width), 1)
    prv = jnp.where(sub == 0, down[:-2], down[1:-1])
    nxt = jnp.where(sub == HALO - 1, up[2:], up[1:-1])
    return tuple(a.reshape(rows, width) for a in (win[1:-1], prv, nxt))


def _shift_body(x_ref, n1_ref, w_ref, mu_ref, out_ref, hout_ref, h_s, p_s, *, seq):
    bb = pl.program_id(0)
    j = pl.program_id(1)
    new = lax.rem(bb, 2)
    old = 1 - new

    @pl.when((bb == 0) & (j == 0))
    def _():
        _zero_halo(p_s, seq)
        h_s[1] = jnp.zeros(h_s.shape[1:], h_s.dtype)

    n_rows = x_ref.shape[0]
    h = _rmsnorm(x_ref[...], n1_ref[...]).astype(BF16)
    h_s[new, pl.ds(pl.multiple_of(j * n_rows, n_rows), n_rows), :] = h
    hout_ref[...] = h

    w = w_ref[...]
    mu = mu_ref[...]
    keep, mix = 1.0 - mu, 0.5 * mu

    def shift(c0):
        cur, prv, nxt = _windows(p_s, c0, ROW_CHUNK)
        out_ref[c0:c0 + ROW_CHUNK, :] = (keep * cur + mix * (prv + nxt)).astype(out_ref.dtype)

    for c0 in range(0, seq, ROW_CHUNK):
        p_s[HALO + c0:HALO + c0 + ROW_CHUNK, :] = jnp.dot(
            h_s[old, c0:c0 + ROW_CHUNK, :], w, preferred_element_type=F32)
        if c0:
            shift(c0 - ROW_CHUNK)
    shift(seq - ROW_CHUNK)


def _conv_body(h_ref, wb_ref, wc_ref, wh_ref, cw_ref, out_ref, u_s, gb_s, *, seq):
    @pl.when(pl.program_id(1) == 0)
    def _():
        _zero_halo(u_s, seq)

    w = jnp.concatenate([wb_ref[...], wc_ref[...], wh_ref[...]], axis=1)
    cw = cw_ref[...]

    def conv(c0):
        cur, prv, nxt = _windows(u_s, c0, ROW_CHUNK)
        y = cw[0:1, :] * prv + cw[1:2, :] * cur + cw[2:3, :] * nxt
        out_ref[c0:c0 + ROW_CHUNK, :] = (gb_s[c0:c0 + ROW_CHUNK, :] * y).astype(out_ref.dtype)

    for c0 in range(0, seq, ROW_CHUNK):
        p = jnp.dot(h_ref[c0:c0 + ROW_CHUNK, :], w, preferred_element_type=F32)
        gb_s[c0:c0 + ROW_CHUNK, :] = p[:, :COL_TILE]
        u_s[HALO + c0:HALO + c0 + ROW_CHUNK, :] = p[:, COL_TILE:2 * COL_TILE] * p[:, 2 * COL_TILE:]
        if c0:
            conv(c0 - ROW_CHUNK)
    conv(seq - ROW_CHUNK)


def _inproj(x, n1, w_shift, mu_p, w_conv, conv_w):
    bsz, seq, d = x.shape
    assert seq % ROW_CHUNK == 0
    params = pltpu.CompilerParams(
        dimension_semantics=("arbitrary", "arbitrary"), vmem_limit_bytes=VMEM_LIMIT)
    n_tiles = w_shift.shape[1] // SHIFT_TILE
    assert seq % n_tiles == 0
    norm_rows = seq // n_tiles

    def nxt(b):
        return jnp.minimum(b, bsz - 1)

    rkvl, h = pl.pallas_call(
        functools.partial(_shift_body, seq=seq),
        grid=(bsz + 1, n_tiles),
        in_specs=[
            pl.BlockSpec((None, norm_rows, d), lambda b, j: (nxt(b), j, 0)),
            pl.BlockSpec((1, d), lambda b, j: (0, 0)),
            pl.BlockSpec((d, SHIFT_TILE), lambda b, j: (0, j)),
            pl.BlockSpec((1, SHIFT_TILE), lambda b, j: (0, j)),
        ],
        out_specs=[
            pl.BlockSpec((None, seq, SHIFT_TILE), lambda b, j: (b, 0, j)),
            pl.BlockSpec((None, norm_rows, d), lambda b, j: (b, j, 0)),
        ],
        out_shape=[
            jax.ShapeDtypeStruct((bsz + 1, seq, w_shift.shape[1]), BF16),
            jax.ShapeDtypeStruct((bsz + 1, seq, d), BF16),
        ],
        scratch_shapes=[pltpu.VMEM((2, seq, d), BF16),
                        pltpu.VMEM((seq + 2 * HALO, SHIFT_TILE), F32)],
        compiler_params=params,
        name="inproj_shift",
    )(x, n1, w_shift, mu_p)
    d_conv = conv_w.shape[-1]
    n_conv = d_conv // COL_TILE
    oconv = pl.pallas_call(
        functools.partial(_conv_body, seq=seq),
        grid=(bsz, n_conv),
        in_specs=[
            pl.BlockSpec((None, seq, d), lambda b, q: (b, 0, 0)),
            pl.BlockSpec((d, COL_TILE), lambda b, q: (0, q)),
            pl.BlockSpec((d, COL_TILE), lambda b, q: (0, n_conv + q)),
            pl.BlockSpec((d, COL_TILE), lambda b, q: (0, 2 * n_conv + q)),
            pl.BlockSpec((3, COL_TILE), lambda b, q: (0, q)),
        ],
        out_specs=pl.BlockSpec((None, seq, COL_TILE), lambda b, q: (b, 0, q)),
        out_shape=jax.ShapeDtypeStruct((bsz, seq, d_conv), BF16),
        scratch_shapes=[pltpu.VMEM((seq + 2 * HALO, COL_TILE), F32),
                        pltpu.VMEM((seq, COL_TILE), F32)],
        compiler_params=params,
        name="inproj_conv",
    )(h, w_conv, w_conv, w_conv, conv_w)
    return rkvl, oconv


def _prep_body(r_ref, k_ref, v_ref, l_ref, vec_ref, wup_ref, aup_ref, gup_ref, ones_ref,
               out_ref, lw_ref, *, dr):
    r = r_ref[...].astype(F32)
    k = k_ref[...].astype(F32)
    v = v_ref[...].astype(F32)
    vec = vec_ref[...]
    ones_bd = ones_ref[...]

    def vrow(i):
        return vec[i:i + 1, :]

    t128 = l_ref[:, 0:LANES]
    lw = jnp.dot(jnp.tanh(t128.astype(F32)).astype(BF16), wup_ref[...],
                 preferred_element_type=F32)
    aa = jnp.dot(t128, aup_ref[...], preferred_element_type=F32)
    lw_f = -LOG_DECAY_SCALE * jax.nn.sigmoid(vrow(V_W0F) + lw[:, :dr])
    lw_b = -LOG_DECAY_SCALE * jax.nn.sigmoid(vrow(V_W0B) + lw[:, dr:])
    a_f = jax.nn.sigmoid(vrow(V_A0F) + aa[:, :dr])
    a_b = jax.nn.sigmoid(vrow(V_A0B) + aa[:, dr:])
    g = jnp.dot(jax.nn.sigmoid(l_ref[:, COL_TILE:2 * COL_TILE].astype(F32)).astype(BF16), gup_ref[...],
                preferred_element_type=F32)

    def head_sum(t):
        return jnp.dot(t.astype(BF16), ones_bd, preferred_element_type=F32)

    kkr = k * vrow(V_KK)
    kk = kkr * lax.rsqrt(jnp.maximum(head_sum(kkr * kkr), NORM_EPS * NORM_EPS))
    kd_f = k * (1.0 + (a_f - 1.0) * vrow(V_KAF))
    kd_b = k * (1.0 + (a_b - 1.0) * vrow(V_KAB))
    bonus = head_sum(r * kd_f * vrow(V_RKF) + r * kd_b * vrow(V_RKB)) * v

    def put(i, val):
        out_ref[:, i * dr:(i + 1) * dr] = val.astype(out_ref.dtype)

    put(P_KK, kk)
    put(P_BF, a_f * kk)
    put(P_BB, a_b * kk)
    put(P_KDF, kd_f)
    put(P_KDB, kd_b)
    put(P_G, g)
    put(P_BONUS, bonus)
    lw_ref[:, L_F * dr:(L_F + 1) * dr] = lw_f
    lw_ref[:, L_B * dr:(L_B + 1) * dr] = lw_b


def _prep(rkvl, vecs, wup, aup, gup, ones_bd, *, dr, row_tile):
    bsz, seq = rkvl.shape[0] - 1, rkvl.shape[1]

    def col(c):
        return pl.BlockSpec((None, row_tile, dr), lambda b, i: (b + 1, i, c))

    def full(a):
        return pl.BlockSpec(a.shape, lambda b, i: (0,) * a.ndim)

    return pl.pallas_call(
        functools.partial(_prep_body, dr=dr),
        grid=(bsz, seq // row_tile),
        in_specs=[col(0), col(1), col(2), col(3),
                  full(vecs), full(wup), full(aup), full(gup), full(ones_bd)],
        out_specs=[pl.BlockSpec((None, row_tile, P_NUM * dr), lambda b, i: (b, i, 0)),
                   pl.BlockSpec((None, row_tile, L_NUM * dr), lambda b, i: (b, i, 0))],
        out_shape=[jax.ShapeDtypeStruct((bsz, seq, P_NUM * dr), BF16),
                   jax.ShapeDtypeStruct((bsz, seq, L_NUM * dr), F32)],
        compiler_params=pltpu.CompilerParams(
            dimension_semantics=("arbitrary", "arbitrary"), vmem_limit_bytes=VMEM_LIMIT),
        name="prep",
    )(rkvl, rkvl, rkvl, rkvl, vecs, wup, aup, gup, ones_bd)


def _block_diag(x, lane_head):
    return jnp.concatenate([jnp.where(lane_head == h, x, 0.0) for h in range(QUAD)],
                           axis=0).astype(BF16)


def _cumsum_time(x, *, forward):
    rows, n = x.shape
    tiles = rows // SUBLANES
    x = x.reshape(tiles, SUBLANES, n)
    sub = lax.broadcasted_iota(jnp.int32, (1, SUBLANES, n), 1)
    step = 1
    while step < SUBLANES:
        if forward:
            x = x + jnp.where(sub >= step, pltpu.roll(x, step, 1), 0.0)
        else:
            x = x + jnp.where(sub < SUBLANES - step, pltpu.roll(x, SUBLANES - step, 1), 0.0)
        step *= 2
    edge = SUBLANES - 1 if forward else 0
    parts = [None] * tiles
    carry = None
    for i in (range(tiles) if forward else reversed(range(tiles))):
        parts[i] = x[i:i + 1] if carry is None else x[i:i + 1] + carry
        total = x[i:i + 1, edge:edge + 1, :]
        carry = total if carry is None else carry + total
    return jnp.concatenate(parts, axis=0).reshape(rows, n)


def _decay_factors(lw, *, forward):
    c = CHUNK
    cs = _cumsum_time(lw, forward=forward)
    cs_end = cs[c - 1:c, :] if forward else cs[0:1, :]
    return dict(e_t=jnp.exp(cs), e_prev=jnp.exp(cs - lw), e_inv=jnp.exp(-cs),
                e_end=jnp.exp(cs_end - cs), w_c=jnp.exp(cs_end))


def _scan_step(dirs, st_ref, n_sub):
    c = CHUNK
    t_idx = lax.broadcasted_iota(jnp.int32, (c, MXU_DIM), 0)
    lane = lax.broadcasted_iota(jnp.int32, (c, MXU_DIM), 1)
    s_idx = lane & (c - 1)
    lane_head = lane >> HEAD_BITS
    eye = (s_idx == t_idx).astype(F32)
    bd_mask = ((lax.broadcasted_iota(jnp.int32, (MXU_DIM, MXU_DIM), 0) >> HEAD_BITS)
               == (lax.broadcasted_iota(jnp.int32, (MXU_DIM, MXU_DIM), 1) >> HEAD_BITS))

    def bd(x):
        return _block_diag(x, lane_head)

    subs = []
    for k in range(n_sub):
        chains = []
        for i, (forward, refs, y_ref) in enumerate(dirs):
            row0 = (k if forward else n_sub - 1 - k) * c
            rows = slice(row0, row0 + c)
            a = {name: ref[rows, :].astype(F32) for name, ref in refs.items()}
            f = _decay_factors(a["lw"], forward=forward)
            if forward:
                strict, incl = s_idx < t_idx, s_idx <= t_idx
            else:
                strict, incl = s_idx > t_idx, s_idx >= t_idx
            kap, rt = a["kk"] * f["e_prev"], a["r"] * f["e_t"]
            kt, bt = a["kd"] * f["e_inv"], a["b"] * f["e_inv"]
            kh, bh = a["kd"] * f["e_end"], a["b"] * f["e_end"]
            n_quad = a["r"].shape[-1] // MXU_DIM
            for q in range(n_quad):
                sl = slice(q * MXU_DIM, (q + 1) * MXU_DIM)
                chains.append(dict(
                    strict=strict, incl=incl, y_ref=y_ref, rows=rows, sl=sl, idx=i * n_quad + q,
                    kr=jnp.concatenate([kap[:, sl], rt[:, sl]], axis=0).astype(BF16),
                    kt=kt[:, sl], bt=bt[:, sl], v=a["v"][:, sl],
                    w_col=jnp.broadcast_to(f["w_c"][:, sl], (MXU_DIM // 2, MXU_DIM)).T,
                    lhs_t=jnp.concatenate([kh[:, sl], bh[:, sl]], axis=0).T.astype(BF16)))
        subs.append(chains)
    base_bits = INV_BASE.bit_length() - 1
    same_base = (t_idx >> base_bits) == (s_idx >> base_bits)
    for g0 in range(0, n_sub, A_GROUP):
        every = [ch for chains in subs[g0:g0 + A_GROUP] for ch in chains]

        for ch in every:
            g_b = _dot_nt(ch["kr"], bd(ch["bt"]))
            g_k = _dot_nt(ch["kr"], bd(ch["kt"]))
            ch["arb"] = jnp.where(ch["incl"], g_b[c:], 0.0)
            ch["aa"] = jnp.concatenate([jnp.where(ch["strict"], g_k[:c], 0.0),
                                        jnp.where(ch["incl"], g_k[c:], 0.0)], axis=0)
            ch["akb"] = jnp.where(ch["strict"], g_b[:c], 0.0)
        for ch in every:
            ch["av"] = _dot(ch["aa"], bd(ch["v"]))

        for ch in every:
            ch["x"] = -jnp.where(same_base, ch["akb"], 0.0)
            ch["t"] = eye + ch["x"]
        for ch in every:
            ch["x"] = _dot(ch["x"], bd(ch["x"]))
        for _ in range(base_bits - 2):
            for ch in every:
                rr = _dot(jnp.concatenate([ch["x"], ch["t"]], axis=0), bd(ch["x"]))
                ch["x"] = rr[:c]
                ch["t"] = ch["t"] + rr[c:]
        for ch in every:
            ch["t"] = ch["t"] + _dot(ch["t"], bd(ch["x"]))
        for bits in range(base_bits, CHUNK.bit_length() - 1):
            pair = ((t_idx >> (bits + 1)) == (s_idx >> (bits + 1))) & ((t_idx >> bits) != (s_idx >> bits))
            for ch in every:
                ch["e"] = _dot(jnp.where(pair, ch["akb"], 0.0), bd(ch["t"]))
            for ch in every:
                ch["t"] = ch["t"] - _dot(ch["t"], bd(ch["e"]))

    for chains in subs:
        for ch in chains:
            ch["m"] = st_ref[ch["idx"]]
            ch["krm"] = _dot(ch["kr"], ch["m"])
        for ch in chains:
            ch["sa"] = _dot(ch["t"], bd(ch["krm"][:c] + ch["av"][:c]))
        for ch in chains:
            sa = ch["sa"]
            y = ch["krm"][c:] + ch["av"][c:] - _dot(ch["arb"], bd(sa))
            ch["y_ref"][ch["rows"], ch["sl"]] = y.astype(ch["y_ref"].dtype)
            upd = _dot(ch["lhs_t"], jnp.concatenate([ch["v"], -sa], axis=0))
            decay = jnp.concatenate([ch["w_col"], ch["w_col"]], axis=1)
            st_ref[ch["idx"]] = decay * ch["m"] + jnp.where(bd_mask, upd, 0.0)


def _scan_body(rf, vf, kkf, bf, kdf, lwf, rb, vb, kkb, bb, kdb, lwb, yf_ref, yb_ref, st_ref,
               *, n_sub):
    @pl.when(pl.program_id(1) == 0)
    def _():
        st_ref[...] = jnp.zeros_like(st_ref)

    def operands(r, v, kk, b, kd, lw):
        return dict(r=r, v=v, kk=kk, b=b, kd=kd, lw=lw)

    _scan_step([(True, operands(rf, vf, kkf, bf, kdf, lwf), yf_ref),
                (False, operands(rb, vb, kkb, bb, kdb, lwb), yb_ref)], st_ref, n_sub)


def _scan(rkvl, prep, logw, *, dr):
    bsz, seq, _ = prep.shape
    rows = SCAN_SUB * CHUNK
    assert seq % rows == 0
    ns = seq // rows
    n_quad = dr // MXU_DIM

    def fwd(col, slot0=0):
        return pl.BlockSpec((None, rows, dr), lambda b, c: (b + slot0, c, col))

    def bwd(col, slot0=0):
        return pl.BlockSpec((None, rows, dr), lambda b, c: (b + slot0, ns - 1 - c, col))

    return pl.pallas_call(
        functools.partial(_scan_body, n_sub=SCAN_SUB),
        grid=(bsz, ns),
        in_specs=[fwd(0, 1), fwd(2, 1), fwd(P_KK), fwd(P_BF), fwd(P_KDF), fwd(L_F),
                  bwd(0, 1), bwd(2, 1), bwd(P_KK), bwd(P_BB), bwd(P_KDB), bwd(L_B)],
        out_specs=[pl.BlockSpec((None, rows, dr), lambda b, c: (b, c, 0)),
                   pl.BlockSpec((None, rows, dr), lambda b, c: (b, ns - 1 - c, 0))],
        out_shape=[jax.ShapeDtypeStruct((bsz, seq, dr), BF16)] * 2,
        scratch_shapes=[pltpu.VMEM((2 * n_quad, MXU_DIM, MXU_DIM), F32)],
        compiler_params=pltpu.CompilerParams(
            dimension_semantics=("arbitrary", "arbitrary"), vmem_limit_bytes=VMEM_LIMIT),
        name="scan",
    )(rkvl, rkvl, prep, prep, prep, logw, rkvl, rkvl, prep, prep, prep, logw)


def _post_ffn_body(yf_ref, yb_ref, g_ref, bonus_ref, oconv_ref, x_ref, wout_ref, gn_ref, ones_ref,
                   n2_ref, wg_ref, wu_ref, wd_ref, nf_ref, out_ref, *, dr, final_norm):
    ones_bd = ones_ref[...]
    y = yf_ref[...].astype(F32) + yb_ref[...].astype(F32)
    mean = jnp.dot(y.astype(BF16), ones_bd, preferred_element_type=F32) * (1.0 / HEAD)
    d = y - mean
    var = jnp.dot((d * d).astype(BF16), ones_bd, preferred_element_type=F32) * (1.0 / HEAD)
    yn = (d * lax.rsqrt(var + GN_EPS) * gn_ref[0:1, :] + gn_ref[1:2, :]
          + bonus_ref[...].astype(F32))
    o = (yn * g_ref[...].astype(F32)).astype(BF16)
    x1 = (x_ref[...]
          + jnp.dot(o, wout_ref[0:dr, :], preferred_element_type=F32)
          + jnp.dot(oconv_ref[...], wout_ref[dr:, :], preferred_element_type=F32))

    h = _rmsnorm(x1, n2_ref[...]).astype(BF16)
    a = jnp.dot(h, wg_ref[...], preferred_element_type=F32)
    u = jnp.dot(h, wu_ref[...], preferred_element_type=F32)
    z = (a * jax.nn.sigmoid(a) * u).astype(BF16)
    x2 = x1 + jnp.dot(z, wd_ref[...], preferred_element_type=F32)
    out_ref[...] = _rmsnorm(x2, nf_ref[...]) if final_norm else x2


def _post_ffn(yf, yb, prep, oconv, x, wout, gn, ones_bd, n2, wg, wu, wd, nf, *, dr, final_norm,
              row_tile):
    n, d = x.shape

    def rows(width, col=0):
        return pl.BlockSpec((row_tile, width), lambda i: (i, col))

    def full(a):
        return pl.BlockSpec(a.shape, lambda i: (0,) * a.ndim, pipeline_mode=pl.Buffered(1))

    return pl.pallas_call(
        functools.partial(_post_ffn_body, dr=dr, final_norm=final_norm),
        grid=(n // row_tile,),
        in_specs=[rows(dr), rows(dr), rows(dr, P_G), rows(dr, P_BONUS), rows(oconv.shape[-1]),
                  rows(d), full(wout), full(gn), full(ones_bd),
                  full(n2), full(wg), full(wu), full(wd), full(nf)],
        out_specs=rows(d),
        out_shape=jax.ShapeDtypeStruct((n, d), F32),
        compiler_params=pltpu.CompilerParams(
            dimension_semantics=("arbitrary",), vmem_limit_bytes=VMEM_LIMIT),
        name="post_ffn",
    )(yf, yb, prep, prep, oconv, x, wout, gn, ones_bd, n2, wg, wu, wd, nf)


def _pad_cols(a, width):
    return jnp.pad(a, ((0, 0), (0, width - a.shape[1])))


def _pad_rows(a, height):
    return jnp.pad(a, ((0, height - a.shape[0]), (0, 0)))


def kernel(x, norm1_w, w_in, mu_shift, w_up_f, w0_f, w_up_b, w0_b, a_up_f, a0_f, a_up_b, a0_b,
           g_up, k_k, k_a_f, k_a_b, r_k_f, r_k_b, gn_w, gn_b, conv_w, w_out, norm2_w, w_gate,
           w_up, w_down, norm_f_w):
    bsz, seq, d = x.shape
    depth = w_in.shape[0]
    dr = w0_f.shape[-1]
    n_dec, n_aaa, n_gate = w_up_f.shape[1], a_up_f.shape[1], g_up.shape[1]
    d_conv = conv_w.shape[-1]
    assert dr % MXU_DIM == 0 and d_conv % COL_TILE == 0 and seq % CHUNK == 0
    assert n_dec + n_aaa <= LANES and n_gate <= COL_TILE
    o_xw = 3 * dr
    o_xa = o_xw + n_dec
    o_xg = o_xa + n_aaa
    o_conv = o_xg + n_gate

    head_id = jnp.arange(dr) // HEAD
    ones_bd = (head_id[:, None] == head_id[None, :]).astype(BF16)

    for l in range(depth):
        w = w_in[l].astype(BF16)
        w_shift = jnp.concatenate(
            [w[:, :o_xw], _pad_cols(w[:, o_xw:o_xg], COL_TILE), _pad_cols(w[:, o_xg:o_conv], COL_TILE)],
            axis=1)
        w_conv = w[:, o_conv:]
        mu = mu_shift[l][None, :]
        mu_p = jnp.concatenate(
            [mu[:, :o_xw], _pad_cols(mu[:, o_xw:o_xg], COL_TILE), _pad_cols(mu[:, o_xg:o_conv], COL_TILE)],
            axis=1)

        rkvl, oconv = _inproj(x, norm1_w[l][None, :], w_shift, mu_p, w_conv, conv_w[l])

        vecs = jnp.stack([w0_f[l], w0_b[l], a0_f[l], a0_b[l], k_k[l], k_a_f[l], k_a_b[l],
                          r_k_f[l].reshape(dr), r_k_b[l].reshape(dr)], axis=0)
        vecs = _pad_rows(vecs, 2 * SUBLANES)
        wup = _pad_rows(jnp.concatenate([w_up_f[l], w_up_b[l]], axis=1), LANES).astype(BF16)
        aup = jnp.concatenate([a_up_f[l], a_up_b[l]], axis=1)
        aup = jnp.pad(aup, ((n_dec, LANES - n_dec - n_aaa), (0, 0))).astype(BF16)
        gup = _pad_rows(g_up[l], COL_TILE).astype(BF16)
        prep, logw = _prep(rkvl, vecs, wup, aup, gup, ones_bd, dr=dr, row_tile=min(1024, seq))

        yf, yb = _scan(rkvl, prep, logw, dr=dr)

        gn = jnp.stack([gn_w[l], gn_b[l]], axis=0)
        n_tok = bsz * seq
        x = _post_ffn(yf.reshape(n_tok, dr), yb.reshape(n_tok, dr), prep.reshape(n_tok, P_NUM * dr),
                      oconv.reshape(n_tok, d_conv), x.reshape(n_tok, d), w_out[l].astype(BF16), gn,
                      ones_bd, norm2_w[l][None, :], w_gate[l].astype(BF16), w_up[l].astype(BF16),
                      w_down[l].astype(BF16), norm_f_w[None, :], dr=dr,
                      final_norm=(l == depth - 1), row_tile=512).reshape(bsz, seq, d)
    return x
```

```python
import functools

import jax
import jax.numpy as jnp
from jax import lax
from jax.experimental import pallas as pl
from jax.experimental.pallas import tpu as pltpu

F32 = jnp.float32
BF16 = jnp.bfloat16

HEAD = 64
CHUNK = 64
INV_BASE = 8
MXU_DIM = 256
QUAD = MXU_DIM // CHUNK
SCAN_SUB = 8
A_GROUP = 4
COL_TILE = 256
SHIFT_TILE = 512
ROW_CHUNK = 512
SUBLANES = 8
LANES = 128
HALO = SUBLANES
HEAD_BITS = HEAD.bit_length() - 1
assert CHUNK == HEAD
LOG_DECAY_SCALE = 0.606531
RMS_EPS = 1e-6
GN_EPS = 64e-5
NORM_EPS = 1e-12
VMEM_LIMIT = 56 * 1024 * 1024

P_KK, P_BF, P_BB, P_KDF, P_KDB, P_G, P_BONUS, P_NUM = range(8)
L_F, L_B, L_NUM = range(3)
V_W0F, V_W0B, V_A0F, V_A0B, V_KK, V_KAF, V_KAB, V_RKF, V_RKB, V_NUM = range(10)


def _dot(a, b):
    return jnp.dot(a.astype(BF16), b.astype(BF16), preferred_element_type=F32)


def _dot_nt(a, b):
    return lax.dot_general(a.astype(BF16), b.astype(BF16), (((1,), (1,)), ((), ())),
                           preferred_element_type=F32)


def _rmsnorm(x, w):
    ms = jnp.mean(x * x, axis=-1, keepdims=True)
    return x * lax.rsqrt(ms + RMS_EPS) * w


def _zero_halo(s_ref, seq):
    z = jnp.zeros((HALO, s_ref.shape[1]), s_ref.dtype)
    s_ref[0:HALO, :] = z
    s_ref[HALO + seq:2 * HALO + seq, :] = z


def _windows(s_ref, c0, rows):
    width = s_ref.shape[-1]
    win = s_ref[c0:c0 + rows + 2 * HALO, :].reshape(rows // HALO + 2, HALO, width)
    down = pltpu.roll(win, 1, 1)
    up = pltpu.roll(win, HALO - 1, 1)
    sub = lax.broadcasted_iota(jnp.int32, (1, HALO, width), 1)
    prv = jnp.where(sub == 0, down[:-2], down[1:-1])
    nxt = jnp.where(sub == HALO - 1, up[2:], up[1:-1])
    return tuple(a.reshape(rows, width) for a in (win[1:-1], prv, nxt))


def _shift_body(x_ref, n1_ref, w_ref, mu_ref, out_ref, hout_ref, h_s, p_s, *, seq):
    bb = pl.program_id(0)
    j = pl.program_id(1)
    new = lax.rem(bb, 2)
    old = 1 - new

    @pl.when((bb == 0) & (j == 0))
    def _():
        _zero_halo(p_s, seq)
        h_s[1] = jnp.zeros(h_s.shape[1:], h_s.dtype)

    n_rows = x_ref.shape[0]
    h = _rmsnorm(x_ref[...], n1_ref[...]).astype(BF16)
    h_s[new, pl.ds(pl.multiple_of(j * n_rows, n_rows), n_rows), :] = h
    hout_ref[...] = h

    w = w_ref[...]
    mu = mu_ref[...]
    keep, mix = 1.0 - mu, 0.5 * mu

    def shift(c0):
        cur, prv, nxt = _windows(p_s, c0, ROW_CHUNK)
        out_ref[c0:c0 + ROW_CHUNK, :] = (keep * cur + mix * (prv + nxt)).astype(out_ref.dtype)

    for c0 in range(0, seq, ROW_CHUNK):
        p_s[HALO + c0:HALO + c0 + ROW_CHUNK, :] = jnp.dot(
            h_s[old, c0:c0 + ROW_CHUNK, :], w, preferred_element_type=F32)
        if c0:
            shift(c0 - ROW_CHUNK)
    shift(seq - ROW_CHUNK)


def _conv_body(h_ref, wb_ref, wc_ref, wh_ref, cw_ref, out_ref, u_s, gb_s, *, seq):
    @pl.when(pl.program_id(1) == 0)
    def _():
        _zero_halo(u_s, seq)

    w = jnp.concatenate([wb_ref[...], wc_ref[...], wh_ref[...]], axis=1)
    cw = cw_ref[...]

    def conv(c0):
        cur, prv, nxt = _windows(u_s, c0, ROW_CHUNK)
        y = cw[0:1, :] * prv + cw[1:2, :] * cur + cw[2:3, :] * nxt
        out_ref[c0:c0 + ROW_CHUNK, :] = (gb_s[c0:c0 + ROW_CHUNK, :] * y).astype(out_ref.dtype)

    for c0 in range(0, seq, ROW_CHUNK):
        p = jnp.dot(h_ref[c0:c0 + ROW_CHUNK, :], w, preferred_element_type=F32)
        gb_s[c0:c0 + ROW_CHUNK, :] = p[:, :COL_TILE]
        u_s[HALO + c0:HALO + c0 + ROW_CHUNK, :] = p[:, COL_TILE:2 * COL_TILE] * p[:, 2 * COL_TILE:]
        if c0:
            conv(c0 - ROW_CHUNK)
    conv(seq - ROW_CHUNK)


def _inproj(x, n1, w_shift, mu_p, w_conv, conv_w):
    bsz, seq, d = x.shape
    assert seq % ROW_CHUNK == 0
    params = pltpu.CompilerParams(
        dimension_semantics=("arbitrary", "arbitrary"), vmem_limit_bytes=VMEM_LIMIT)
    n_tiles = w_shift.shape[1] // SHIFT_TILE
    assert seq % n_tiles == 0
    norm_rows = seq // n_tiles

    def nxt(b):
        return jnp.minimum(b, bsz - 1)

    rkvl, h = pl.pallas_call(
        functools.partial(_shift_body, seq=seq),
        grid=(bsz + 1, n_tiles),
        in_specs=[
            pl.BlockSpec((None, norm_rows, d), lambda b, j: (nxt(b), j, 0)),
            pl.BlockSpec((1, d), lambda b, j: (0, 0)),
            pl.BlockSpec((d, SHIFT_TILE), lambda b, j: (0, j)),
            pl.BlockSpec((1, SHIFT_TILE), lambda b, j: (0, j)),
        ],
        out_specs=[
            pl.BlockSpec((None, seq, SHIFT_TILE), lambda b, j: (b, 0, j)),
            pl.BlockSpec((None, norm_rows, d), lambda b, j: (b, j, 0)),
        ],
        out_shape=[
            jax.ShapeDtypeStruct((bsz + 1, seq, w_shift.shape[1]), BF16),
            jax.ShapeDtypeStruct((bsz + 1, seq, d), BF16),
        ],
        scratch_shapes=[pltpu.VMEM((2, seq, d), BF16),
                        pltpu.VMEM((seq + 2 * HALO, SHIFT_TILE), F32)],
        compiler_params=params,
        name="inproj_shift",
    )(x, n1, w_shift, mu_p)
    d_conv = conv_w.shape[-1]
    n_conv = d_conv // COL_TILE
    oconv = pl.pallas_call(
        functools.partial(_conv_body, seq=seq),
        grid=(bsz, n_conv),
        in_specs=[
            pl.BlockSpec((None, seq, d), lambda b, q: (b, 0, 0)),
            pl.BlockSpec((d, COL_TILE), lambda b, q: (0, q)),
            pl.BlockSpec((d, COL_TILE), lambda b, q: (0, n_conv + q)),
            pl.BlockSpec((d, COL_TILE), lambda b, q: (0, 2 * n_conv + q)),
            pl.BlockSpec((3, COL_TILE), lambda b, q: (0, q)),
        ],
        out_specs=pl.BlockSpec((None, seq, COL_TILE), lambda b, q: (b, 0, q)),
        out_shape=jax.ShapeDtypeStruct((bsz, seq, d_conv), BF16),
        scratch_shapes=[pltpu.VMEM((seq + 2 * HALO, COL_TILE), F32),
                        pltpu.VMEM((seq, COL_TILE), F32)],
        compiler_params=params,
        name="inproj_conv",
    )(h, w_conv, w_conv, w_conv, conv_w)
    return rkvl, oconv


def _prep_body(r_ref, k_ref, v_ref, l_ref, vec_ref, wup_ref, aup_ref, gup_ref, ones_ref,
               out_ref, lw_ref, *, dr):
    r = r_ref[...].astype(F32)
    k = k_ref[...].astype(F32)
    v = v_ref[...].astype(F32)
    vec = vec_ref[...]
    ones_bd = ones_ref[...]

    def vrow(i):
        return vec[i:i + 1, :]

    t128 = l_ref[:, 0:LANES]
    lw = jnp.dot(jnp.tanh(t128.astype(F32)).astype(BF16), wup_ref[...],
                 preferred_element_type=F32)
    aa = jnp.dot(t128, aup_ref[...], preferred_element_type=F32)
    lw_f = -LOG_DECAY_SCALE * jax.nn.sigmoid(vrow(V_W0F) + lw[:, :dr])
    lw_b = -LOG_DECAY_SCALE * jax.nn.sigmoid(vrow(V_W0B) + lw[:, dr:])
    a_f = jax.nn.sigmoid(vrow(V_A0F) + aa[:, :dr])
    a_b = jax.nn.sigmoid(vrow(V_A0B) + aa[:, dr:])
    g = jnp.dot(jax.nn.sigmoid(l_ref[:, COL_TILE:2 * COL_TILE].astype(F32)).astype(BF16), gup_ref[...],
                preferred_element_type=F32)

    def head_sum(t):
        return jnp.dot(t.astype(BF16), ones_bd, preferred_element_type=F32)

    kkr = k * vrow(V_KK)
    kk = kkr * lax.rsqrt(jnp.maximum(head_sum(kkr * kkr), NORM_EPS * NORM_EPS))
    kd_f = k * (1.0 + (a_f - 1.0) * vrow(V_KAF))
    kd_b = k * (1.0 + (a_b - 1.0) * vrow(V_KAB))
    bonus = head_sum(r * kd_f * vrow(V_RKF) + r * kd_b * vrow(V_RKB)) * v

    def put(i, val):
        out_ref[:, i * dr:(i + 1) * dr] = val.astype(out_ref.dtype)

    put(P_KK, kk)
    put(P_BF, a_f * kk)
    put(P_BB, a_b * kk)
    put(P_KDF, kd_f)
    put(P_KDB, kd_b)
    put(P_G, g)
    put(P_BONUS, bonus)
    lw_ref[:, L_F * dr:(L_F + 1) * dr] = lw_f
    lw_ref[:, L_B * dr:(L_B + 1) * dr] = lw_b


def _prep(rkvl, vecs, wup, aup, gup, ones_bd, *, dr, row_tile):
    bsz, seq = rkvl.shape[0] - 1, rkvl.shape[1]

    def col(c):
        return pl.BlockSpec((None, row_tile, dr), lambda b, i: (b + 1, i, c))

    def full(a):
        return pl.BlockSpec(a.shape, lambda b, i: (0,) * a.ndim)

    return pl.pallas_call(
        functools.partial(_prep_body, dr=dr),
        grid=(bsz, seq // row_tile),
        in_specs=[col(0), col(1), col(2), col(3),
                  full(vecs), full(wup), full(aup), full(gup), full(ones_bd)],
        out_specs=[pl.BlockSpec((None, row_tile, P_NUM * dr), lambda b, i: (b, i, 0)),
                   pl.BlockSpec((None, row_tile, L_NUM * dr), lambda b, i: (b, i, 0))],
        out_shape=[jax.ShapeDtypeStruct((bsz, seq, P_NUM * dr), BF16),
                   jax.ShapeDtypeStruct((bsz, seq, L_NUM * dr), F32)],
        compiler_params=pltpu.CompilerParams(
            dimension_semantics=("arbitrary", "arbitrary"), vmem_limit_bytes=VMEM_LIMIT),
        name="prep",
    )(rkvl, rkvl, rkvl, rkvl, vecs, wup, aup, gup, ones_bd)


def _block_diag(x, lane_head):
    return jnp.concatenate([jnp.where(lane_head == h, x, 0.0) for h in range(QUAD)],
                           axis=0).astype(BF16)


def _cumsum_time(x, *, forward):
    rows, n = x.shape
    tiles = rows // SUBLANES
    x = x.reshape(tiles, SUBLANES, n)
    sub = lax.broadcasted_iota(jnp.int32, (1, SUBLANES, n), 1)
    step = 1
    while step < SUBLANES:
        if forward:
            x = x + jnp.where(sub >= step, pltpu.roll(x, step, 1), 0.0)
        else:
            x = x + jnp.where(sub < SUBLANES - step, pltpu.roll(x, SUBLANES - step, 1), 0.0)
        step *= 2
    edge = SUBLANES - 1 if forward else 0
    parts = [None] * tiles
    carry = None
    for i in (range(tiles) if forward else reversed(range(tiles))):
        parts[i] = x[i:i + 1] if carry is None else x[i:i + 1] + carry
        total = x[i:i + 1, edge:edge + 1, :]
        carry = total if carry is None else carry + total
    return jnp.concatenate(parts, axis=0).reshape(rows, n)


def _decay_factors(lw, *, forward):
    c = CHUNK
    cs = _cumsum_time(lw, forward=forward)
    cs_end = cs[c - 1:c, :] if forward else cs[0:1, :]
    return dict(e_t=jnp.exp(cs), e_prev=jnp.exp(cs - lw), e_inv=jnp.exp(-cs),
                e_end=jnp.exp(cs_end - cs), w_c=jnp.exp(cs_end))


def _scan_step(dirs, st_ref, n_sub):
    c = CHUNK
    t_idx = lax.broadcasted_iota(jnp.int32, (c, MXU_DIM), 0)
    lane = lax.broadcasted_iota(jnp.int32, (c, MXU_DIM), 1)
    s_idx = lane & (c - 1)
    lane_head = lane >> HEAD_BITS
    eye = (s_idx == t_idx).astype(F32)
    bd_mask = ((lax.broadcasted_iota(jnp.int32, (MXU_DIM, MXU_DIM), 0) >> HEAD_BITS)
               == (lax.broadcasted_iota(jnp.int32, (MXU_DIM, MXU_DIM), 1) >> HEAD_BITS))

    def bd(x):
        return _block_diag(x, lane_head)

    subs = []
    for k in range(n_sub):
        chains = []
        for i, (forward, refs, y_ref) in enumerate(dirs):
            row0 = (k if forward else n_sub - 1 - k) * c
            rows = slice(row0, row0 + c)
            a = {name: ref[rows, :].astype(F32) for name, ref in refs.items()}
            f = _decay_factors(a["lw"], forward=forward)
            if forward:
                strict, incl = s_idx < t_idx, s_idx <= t_idx
            else:
                strict, incl = s_idx > t_idx, s_idx >= t_idx
            kap, rt = a["kk"] * f["e_prev"], a["r"] * f["e_t"]
            kt, bt = a["kd"] * f["e_inv"], a["b"] * f["e_inv"]
            kh, bh = a["kd"] * f["e_end"], a["b"] * f["e_end"]
            n_quad = a["r"].shape[-1] // MXU_DIM
            for q in range(n_quad):
                sl = slice(q * MXU_DIM, (q + 1) * MXU_DIM)
                chains.append(dict(
                    strict=strict, incl=incl, y_ref=y_ref, rows=rows, sl=sl, idx=i * n_quad + q,
                    kr=jnp.concatenate([kap[:, sl], rt[:, sl]], axis=0).astype(BF16),
                    kt=kt[:, sl], bt=bt[:, sl], v=a["v"][:, sl],
                    w_col=jnp.broadcast_to(f["w_c"][:, sl], (MXU_DIM // 2, MXU_DIM)).T,
                    lhs_t=jnp.concatenate([kh[:, sl], bh[:, sl]], axis=0).T.astype(BF16)))
        subs.append(chains)
    base_bits = INV_BASE.bit_length() - 1
    same_base = (t_idx >> base_bits) == (s_idx >> base_bits)

    def stage_a_levels(every):
        def gram():
            for ch in every:
                g_b = _dot_nt(ch["kr"], bd(ch["bt"]))
                g_k = _dot_nt(ch["kr"], bd(ch["kt"]))
                ch["arb"] = jnp.where(ch["incl"], g_b[c:], 0.0)
                ch["aa"] = jnp.concatenate([jnp.where(ch["strict"], g_k[:c], 0.0),
                                            jnp.where(ch["incl"], g_k[c:], 0.0)], axis=0)
                ch["akb"] = jnp.where(ch["strict"], g_b[:c], 0.0)

        def av():
            for ch in every:
                ch["av"] = _dot(ch["aa"], bd(ch["v"]))

        def base_first():
            for ch in every:
                x = -jnp.where(same_base, ch["akb"], 0.0)
                ch["t"] = eye + x
                ch["x"] = _dot(x, bd(x))

        def base_double():
            for ch in every:
                rr = _dot(jnp.concatenate([ch["x"], ch["t"]], axis=0), bd(ch["x"]))
                ch["x"] = rr[:c]
                ch["t"] = ch["t"] + rr[c:]

        def base_last():
            for ch in every:
                ch["t"] = ch["t"] + _dot(ch["t"], bd(ch["x"]))

        def merge(bits):
            pair = (((t_idx >> (bits + 1)) == (s_idx >> (bits + 1)))
                    & ((t_idx >> bits) != (s_idx >> bits)))

            def e_level():
                for ch in every:
                    ch["e"] = _dot(jnp.where(pair, ch["akb"], 0.0), bd(ch["t"]))

            def t_level():
                for ch in every:
                    ch["t"] = ch["t"] - _dot(ch["t"], bd(ch["e"]))
            return [e_level, t_level]

        levels = [gram, av, base_first] + [base_double] * (base_bits - 2) + [base_last]
        for bits in range(base_bits, CHUNK.bit_length() - 1):
            levels += merge(bits)
        return levels

    def stage_b_levels(chains):
        def state_read():
            for ch in chains:
                ch["m"] = st_ref[ch["idx"]]
                ch["krm"] = _dot(ch["kr"], ch["m"])

        def sa_level():
            for ch in chains:
                ch["sa"] = _dot(ch["t"], bd(ch["krm"][:c] + ch["av"][:c]))

        def out_level():
            for ch in chains:
                sa = ch["sa"]
                y = ch["krm"][c:] + ch["av"][c:] - _dot(ch["arb"], bd(sa))
                ch["y_ref"][ch["rows"], ch["sl"]] = y.astype(ch["y_ref"].dtype)
                upd = _dot(ch["lhs_t"], jnp.concatenate([ch["v"], -sa], axis=0))
                decay = jnp.concatenate([ch["w_col"], ch["w_col"]], axis=1)
                st_ref[ch["idx"]] = decay * ch["m"] + jnp.where(bd_mask, upd, 0.0)
        return [state_read, sa_level, out_level]

    pending = []
    for g0 in range(0, n_sub, A_GROUP):
        group = subs[g0:g0 + A_GROUP]
        levels_a = stage_a_levels([ch for chains in group for ch in chains])
        done = 0
        for i, level in enumerate(levels_a, start=1):
            level()
            while done < len(pending) and done * len(levels_a) < i * len(pending):
                pending[done]()
                done += 1
        pending = [lvl for chains in group for lvl in stage_b_levels(chains)]
    for level in pending:
        level()


def _scan_body(rf, vf, kkf, bf, kdf, lwf, rb, vb, kkb, bb, kdb, lwb, yf_ref, yb_ref, st_ref,
               *, n_sub):
    @pl.when(pl.program_id(1) == 0)
    def _():
        st_ref[...] = jnp.zeros_like(st_ref)

    def operands(r, v, kk, b, kd, lw):
        return dict(r=r, v=v, kk=kk, b=b, kd=kd, lw=lw)

    _scan_step([(True, operands(rf, vf, kkf, bf, kdf, lwf), yf_ref),
                (False, operands(rb, vb, kkb, bb, kdb, lwb), yb_ref)], st_ref, n_sub)


def _scan(rkvl, prep, logw, *, dr):
    bsz, seq, _ = prep.shape
    rows = SCAN_SUB * CHUNK
    assert seq % rows == 0
    ns = seq // rows
    n_quad = dr // MXU_DIM

    def fwd(col, slot0=0):
        return pl.BlockSpec((None, rows, dr), lambda b, c: (b + slot0, c, col))

    def bwd(col, slot0=0):
        return pl.BlockSpec((None, rows, dr), lambda b, c: (b + slot0, ns - 1 - c, col))

    return pl.pallas_call(
        functools.partial(_scan_body, n_sub=SCAN_SUB),
        grid=(bsz, ns),
        in_specs=[fwd(0, 1), fwd(2, 1), fwd(P_KK), fwd(P_BF), fwd(P_KDF), fwd(L_F),
                  bwd(0, 1), bwd(2, 1), bwd(P_KK), bwd(P_BB), bwd(P_KDB), bwd(L_B)],
        out_specs=[pl.BlockSpec((None, rows, dr), lambda b, c: (b, c, 0)),
                   pl.BlockSpec((None, rows, dr), lambda b, c: (b, ns - 1 - c, 0))],
        out_shape=[jax.ShapeDtypeStruct((bsz, seq, dr), BF16)] * 2,
        scratch_shapes=[pltpu.VMEM((2 * n_quad, MXU_DIM, MXU_DIM), F32)],
        compiler_params=pltpu.CompilerParams(
            dimension_semantics=("arbitrary", "arbitrary"), vmem_limit_bytes=VMEM_LIMIT),
        name="scan",
    )(rkvl, rkvl, prep, prep, prep, logw, rkvl, rkvl, prep, prep, prep, logw)


def _post_ffn_body(yf_ref, yb_ref, g_ref, bonus_ref, oconv_ref, x_ref, wout_ref, gn_ref, ones_ref,
                   n2_ref, wg_ref, wu_ref, wd_ref, nf_ref, out_ref, *, dr, final_norm):
    ones_bd = ones_ref[...]
    y = yf_ref[...].astype(F32) + yb_ref[...].astype(F32)
    mean = jnp.dot(y.astype(BF16), ones_bd, preferred_element_type=F32) * (1.0 / HEAD)
    d = y - mean
    var = jnp.dot((d * d).astype(BF16), ones_bd, preferred_element_type=F32) * (1.0 / HEAD)
    yn = (d * lax.rsqrt(var + GN_EPS) * gn_ref[0:1, :] + gn_ref[1:2, :]
          + bonus_ref[...].astype(F32))
    o = (yn * g_ref[...].astype(F32)).astype(BF16)
    x1 = (x_ref[...]
          + jnp.dot(o, wout_ref[0:dr, :], preferred_element_type=F32)
          + jnp.dot(oconv_ref[...], wout_ref[dr:, :], preferred_element_type=F32))

    h = _rmsnorm(x1, n2_ref[...]).astype(BF16)
    a = jnp.dot(h, wg_ref[...], preferred_element_type=F32)
    u = jnp.dot(h, wu_ref[...], preferred_element_type=F32)
    z = (a * jax.nn.sigmoid(a) * u).astype(BF16)
    x2 = x1 + jnp.dot(z, wd_ref[...], preferred_element_type=F32)
    out_ref[...] = _rmsnorm(x2, nf_ref[...]) if final_norm else x2


def _post_ffn(yf, yb, prep, oconv, x, wout, gn, ones_bd, n2, wg, wu, wd, nf, *, dr, final_norm,
              row_tile):
    n, d = x.shape

    def rows(width, col=0):
        return pl.BlockSpec((row_tile, width), lambda i: (i, col))

    def full(a):
        return pl.BlockSpec(a.shape, lambda i: (0,) * a.ndim, pipeline_mode=pl.Buffered(1))

    return pl.pallas_call(
        functools.partial(_post_ffn_body, dr=dr, final_norm=final_norm),
        grid=(n // row_tile,),
        in_specs=[rows(dr), rows(dr), rows(dr, P_G), rows(dr, P_BONUS), rows(oconv.shape[-1]),
                  rows(d), full(wout), full(gn), full(ones_bd),
                  full(n2), full(wg), full(wu), full(wd), full(nf)],
        out_specs=rows(d),
        out_shape=jax.ShapeDtypeStruct((n, d), F32),
        compiler_params=pltpu.CompilerParams(
            dimension_semantics=("arbitrary",), vmem_limit_bytes=VMEM_LIMIT),
        name="post_ffn",
    )(yf, yb, prep, prep, oconv, x, wout, gn, ones_bd, n2, wg, wu, wd, nf)


def _pad_cols(a, width):
    return jnp.pad(a, ((0, 0), (0, width - a.shape[1])))


def _pad_rows(a, height):
    return jnp.pad(a, ((0, height - a.shape[0]), (0, 0)))


def kernel(x, norm1_w, w_in, mu_shift, w_up_f, w0_f, w_up_b, w0_b, a_up_f, a0_f, a_up_b, a0_b,
           g_up, k_k, k_a_f, k_a_b, r_k_f, r_k_b, gn_w, gn_b, conv_w, w_out, norm2_w, w_gate,
           w_up, w_down, norm_f_w):
    bsz, seq, d = x.shape
    depth = w_in.shape[0]
    dr = w0_f.shape[-1]
    n_dec, n_aaa, n_gate = w_up_f.shape[1], a_up_f.shape[1], g_up.shape[1]
    d_conv = conv_w.shape[-1]
    assert dr % MXU_DIM == 0 and d_conv % COL_TILE == 0 and seq % CHUNK == 0
    assert n_dec + n_aaa <= LANES and n_gate <= COL_TILE
    o_xw = 3 * dr
    o_xa = o_xw + n_dec
    o_xg = o_xa + n_aaa
    o_conv = o_xg + n_gate

    head_id = jnp.arange(dr) // HEAD
    ones_bd = (head_id[:, None] == head_id[None, :]).astype(BF16)

    for l in range(depth):
        w = w_in[l].astype(BF16)
        w_shift = jnp.concatenate(
            [w[:, :o_xw], _pad_cols(w[:, o_xw:o_xg], COL_TILE), _pad_cols(w[:, o_xg:o_conv], COL_TILE)],
            axis=1)
        w_conv = w[:, o_conv:]
        mu = mu_shift[l][None, :]
        mu_p = jnp.concatenate(
            [mu[:, :o_xw], _pad_cols(mu[:, o_xw:o_xg], COL_TILE), _pad_cols(mu[:, o_xg:o_conv], COL_TILE)],
            axis=1)

        rkvl, oconv = _inproj(x, norm1_w[l][None, :], w_shift, mu_p, w_conv, conv_w[l])

        vecs = jnp.stack([w0_f[l], w0_b[l], a0_f[l], a0_b[l], k_k[l], k_a_f[l], k_a_b[l],
                          r_k_f[l].reshape(dr), r_k_b[l].reshape(dr)], axis=0)
        vecs = _pad_rows(vecs, 2 * SUBLANES)
        wup = _pad_rows(jnp.concatenate([w_up_f[l], w_up_b[l]], axis=1), LANES).astype(BF16)
        aup = jnp.concatenate([a_up_f[l], a_up_b[l]], axis=1)
        aup = jnp.pad(aup, ((n_dec, LANES - n_dec - n_aaa), (0, 0))).astype(BF16)
        gup = _pad_rows(g_up[l], COL_TILE).astype(BF16)
        prep, logw = _prep(rkvl, vecs, wup, aup, gup, ones_bd, dr=dr, row_tile=min(1024, seq))

        yf, yb = _scan(rkvl, prep, logw, dr=dr)

        gn = jnp.stack([gn_w[l], gn_b[l]], axis=0)
        n_tok = bsz * seq
        x = _post_ffn(yf.reshape(n_tok, dr), yb.reshape(n_tok, dr), prep.reshape(n_tok, P_NUM * dr),
                      oconv.reshape(n_tok, d_conv), x.reshape(n_tok, d), w_out[l].astype(BF16), gn,
                      ones_bd, norm2_w[l][None, :], w_gate[l].astype(BF16), w_up[l].astype(BF16),
                      w_down[l].astype(BF16), norm_f_w[None, :], dr=dr,
                      final_norm=(l == depth - 1), row_tile=512).reshape(bsz, seq, d)
    return x
```
